```python
import math
import jax, jax.numpy as jnp
from jax import lax
import numpy as np

D_MODEL = 2048
BATCH = 4
SEQ = 8192
DEPTH = 1

CTX_LEN = 256
GRID_W = 64

ATTN_HEADS = 8
ATTN_HEAD_DIM = 64
ATTN_VALUE_DIM = 2 * ATTN_HEAD_DIM
ATTN_WIDTH = ATTN_HEADS * ATTN_VALUE_DIM
QK_WIDTH = 2 * ATTN_HEADS * ATTN_HEAD_DIM
ROPE_THETA = 10000.0
Q_BLOCK = 128

POOL_WINDOWS = (2, 4, 8, 16)
POOL_GROUPS = len(POOL_WINDOWS)
POOL_WIDTH = D_MODEL // 2
POOL_GROUP_DIM = POOL_WIDTH // POOL_GROUPS

N_BRANCHES = 2
D_FF = 4 * D_MODEL
EPS = 1e-6

Q_OFF = 0
K_OFF = Q_OFF + QK_WIDTH
V_OFF = K_OFF + QK_WIDTH
P_OFF = V_OFF + ATTN_WIDTH
G_OFF = P_OFF + POOL_WIDTH
IN_WIDTH = G_OFF + N_BRANCHES * D_MODEL

kernel_name = "hybrid_diffattn_pool_dit_layer"


def rmsnorm(x, w):
    xf = x.astype(jnp.float32)
    y = xf * lax.rsqrt(jnp.mean(xf * xf, axis=-1, keepdims=True) + EPS)
    return (y * w.astype(jnp.float32)).astype(x.dtype)


def modulate(x, w, shift, scale):
    return rmsnorm(x, w) * (1 + scale) + shift


def split_qk(t):
    B, L, _ = t.shape
    return t.reshape(B, L, 2 * ATTN_HEADS, ATTN_HEAD_DIM).transpose(0, 2, 1, 3)


def split_v(t):
    B, L, _ = t.shape
    return t.reshape(B, L, ATTN_HEADS, ATTN_VALUE_DIM).transpose(0, 2, 1, 3)


def axial_rope(x, row, col):
    half = x.shape[-1] // 2
    inv_freq = ROPE_THETA ** (-jnp.arange(0, half, 2, dtype=jnp.float32) / half)

    def rot(xa, pos):
        ang = pos.astype(jnp.float32)[:, None] * inv_freq[None, :]
        cos, sin = jnp.cos(ang), jnp.sin(ang)
        x1, x2 = xa[..., : half // 2], xa[..., half // 2:]
        return jnp.concatenate([x1 * cos - x2 * sin, x1 * sin + x2 * cos], axis=-1)

    xf = x.astype(jnp.float32)
    out = jnp.concatenate([rot(xf[..., :half], row), rot(xf[..., half:], col)], axis=-1)
    return out.astype(x.dtype)


def diff_attention(q, k, v, lam):
    B, _, Lq, dh = q.shape
    Lk = k.shape[2]
    nblk = Lq // Q_BLOCK
    qb = q.reshape(B, 2 * ATTN_HEADS, nblk, Q_BLOCK, dh).transpose(2, 0, 1, 3, 4)
    scale = dh ** -0.5

    def block(qi):
        s = jnp.einsum('bhqd,bhkd->bhqk', qi, k, preferred_element_type=jnp.float32) * scale
        p = jax.nn.softmax(s, axis=-1).reshape(B, ATTN_HEADS, 2, Q_BLOCK, Lk)
        a = (p[:, :, 0] - lam * p[:, :, 1]).astype(v.dtype)
        return jnp.einsum('bhqk,bhkd->bqhd', a, v)

    o = lax.map(block, qb)
    return o.transpose(1, 0, 2, 3, 4).reshape(B, Lq, ATTN_HEADS, ATTN_VALUE_DIM)


def multiscale_pool(u, pool_w, pool_scale):
    B, L, _ = u.shape
    uf = u.astype(jnp.float32)
    cs = jnp.pad(jnp.cumsum(uf, axis=1), ((0, 0), (1, 0), (0, 0)))
    t = jnp.arange(L)
    outs = []
    for g, w in enumerate(POOL_WINDOWS):
        sl = slice(g * POOL_GROUP_DIM, (g + 1) * POOL_GROUP_DIM)
        lo = jnp.clip(t - w // 2, 0, L)
        hi = jnp.clip(t + w - w // 2, 0, L)
        csg = cs[..., sl]
        mean = (csg[:, hi] - csg[:, lo]) / (hi - lo).astype(jnp.float32)[None, :, None]
        outs.append(mean - uf[..., sl])
    d = jnp.stack(outs, axis=2).astype(u.dtype)
    y = jnp.einsum('blgc,gcd->blgd', d, pool_w).reshape(B, L, POOL_WIDTH)
    return y * pool_scale


def token_mixer(q, k_all, v_all, pool_in, gate_logits, lam, lam_init, subln_w,
                pool_w, pool_scale, w_a_up, w_b_up, w_o):
    B, L = pool_in.shape[:2]
    heads = diff_attention(q, k_all, v_all, lam)
    heads = rmsnorm(heads, subln_w) * (1.0 - lam_init)
    y_a = heads.reshape(B, L, ATTN_WIDTH) @ w_a_up
    y_b = multiscale_pool(pool_in, pool_w, pool_scale) @ w_b_up
    g = jax.nn.sigmoid(gate_logits.astype(jnp.float32)).astype(y_a.dtype)
    g_a, g_b = g[..., :D_MODEL], g[..., D_MODEL:]
    return (g_a * y_a + g_b * y_b) @ w_o


def sq_relu_mlp(h, w1, w2):
    return jnp.square(jax.nn.relu(h @ w1)) @ w2


def setup_inputs(seed: int = 0) -> dict:
    key = jax.random.key(seed)
    ks = jax.random.split(key, 24)
    f32 = jnp.float32

    def nrm(k, shape, scale):
        return jax.random.normal(k, shape, f32) * scale

    return {
        "x": nrm(ks[0], (BATCH, SEQ, D_MODEL), 1.0),
        "c": nrm(ks[1], (BATCH, D_MODEL), 1.0),
        "ctx": nrm(ks[2], (BATCH, CTX_LEN, D_MODEL), 1.0),
        "c_ctx": nrm(ks[3], (D_MODEL,), 1.0),
        "w_mod": nrm(ks[4], (DEPTH, D_MODEL, 6 * D_MODEL), D_MODEL ** -0.5),
        "b_mod": nrm(ks[5], (DEPTH, 6 * D_MODEL), 0.02),
        "norm_attn_w": 1.0 + nrm(ks[6], (DEPTH, D_MODEL), 0.1),
        "w_in": nrm(ks[7], (DEPTH, D_MODEL, IN_WIDTH), D_MODEL ** -0.5),
        "q_norm_w": 1.0 + nrm(ks[8], (DEPTH, ATTN_HEAD_DIM), 0.1),
        "k_norm_w": 1.0 + nrm(ks[9], (DEPTH, ATTN_HEAD_DIM), 0.1),
        "lambda_q1": nrm(ks[10], (DEPTH, ATTN_HEAD_DIM), 0.1),
        "lambda_k1": nrm(ks[11], (DEPTH, ATTN_HEAD_DIM), 0.1),
        "lambda_q2": nrm(ks[12], (DEPTH, ATTN_HEAD_DIM), 0.1),
        "lambda_k2": nrm(ks[13], (DEPTH, ATTN_HEAD_DIM), 0.1),
        "subln_w": 1.0 + nrm(ks[14], (DEPTH, ATTN_VALUE_DIM), 0.1),
        "pool_w": nrm(ks[15], (DEPTH, POOL_GROUPS, POOL_GROUP_DIM, POOL_GROUP_DIM), POOL_GROUP_DIM ** -0.5),
        "pool_scale": 1.0 + nrm(ks[16], (DEPTH, POOL_WIDTH), 0.1),
        "w_a_up": nrm(ks[17], (DEPTH, ATTN_WIDTH, D_MODEL), ATTN_WIDTH ** -0.5),
        "w_b_up": nrm(ks[18], (DEPTH, POOL_WIDTH, D_MODEL), POOL_WIDTH ** -0.5),
        "w_o": nrm(ks[19], (DEPTH, D_MODEL, D_MODEL), D_MODEL ** -0.5),
        "norm_mlp_w": 1.0 + nrm(ks[20], (DEPTH, D_MODEL), 0.1),
        "w_ff1": nrm(ks[21], (DEPTH, D_MODEL, D_FF), D_MODEL ** -0.5),
        "w_ff2": nrm(ks[22], (DEPTH, D_FF, D_MODEL), D_FF ** -0.5),
    }


def reference(x, c, ctx, c_ctx, w_mod, b_mod, norm_attn_w, w_in, q_norm_w, k_norm_w,
              lambda_q1, lambda_k1, lambda_q2, lambda_k2, subln_w, pool_w, pool_scale,
              w_a_up, w_b_up, w_o, norm_mlp_w, w_ff1, w_ff2):
    L = x.shape[1]
    rows = L // GRID_W
    row = jnp.repeat(jnp.arange(rows), GRID_W)
    col = jnp.tile(jnp.arange(GRID_W), rows)

    x_ctx = ctx
    for l in range(DEPTH):
        last = l == DEPTH - 1
        lam_init = 0.8 - 0.6 * math.exp(-0.3 * l)
        lam = (jnp.exp(jnp.sum(lambda_q1[l].astype(jnp.float32) * lambda_k1[l].astype(jnp.float32)))
               - jnp.exp(jnp.sum(lambda_q2[l].astype(jnp.float32) * lambda_k2[l].astype(jnp.float32)))
               + lam_init)

        mod = (jax.nn.silu(c) @ w_mod[l] + b_mod[l])[:, None, :]
        sa, ca, ga, sm, cm, gm = jnp.split(mod, 6, axis=-1)
        mod_c = jax.nn.silu(c_ctx) @ w_mod[l] + b_mod[l]
        sa_c, ca_c, ga_c, sm_c, cm_c, gm_c = jnp.split(mod_c, 6, axis=-1)

        h_c = modulate(x_ctx, norm_attn_w[l], sa_c, ca_c)
        if last:
            p_c_kv = h_c @ w_in[l][:, K_OFF:P_OFF]
        else:
            p_c = h_c @ w_in[l]
            p_c_kv = p_c[..., K_OFF:P_OFF]
        k_c = rmsnorm(split_qk(p_c_kv[..., :QK_WIDTH]), k_norm_w[l])
        v_c = split_v(p_c_kv[..., QK_WIDTH:])

        h = modulate(x, norm_attn_w[l], sa, ca)
        p = h @ w_in[l]
        q = axial_rope(rmsnorm(split_qk(p[..., Q_OFF:K_OFF]), q_norm_w[l]), row, col)
        k = axial_rope(rmsnorm(split_qk(p[..., K_OFF:V_OFF]), k_norm_w[l]), row, col)
        v = split_v(p[..., V_OFF:P_OFF])
        k_all = jnp.concatenate([k, k_c], axis=2)
        v_all = jnp.concatenate([v, v_c], axis=2)
        mix = token_mixer(q, k_all, v_all, p[..., P_OFF:G_OFF], p[..., G_OFF:], lam, lam_init,
                          subln_w[l], pool_w[l], pool_scale[l], w_a_up[l], w_b_up[l], w_o[l])
        x = x + ga * mix
        x = x + gm * sq_relu_mlp(modulate(x, norm_mlp_w[l], sm, cm), w_ff1[l], w_ff2[l])

        if not last:
            q_c = rmsnorm(split_qk(p_c[..., Q_OFF:K_OFF]), q_norm_w[l])
            mix_c = token_mixer(q_c, k_c, v_c, p_c[..., P_OFF:G_OFF], p_c[..., G_OFF:], lam, lam_init,
                                subln_w[l], pool_w[l], pool_scale[l], w_a_up[l], w_b_up[l], w_o[l])
            x_ctx = x_ctx + ga_c * mix_c
            x_ctx = x_ctx + gm_c * sq_relu_mlp(modulate(x_ctx, norm_mlp_w[l], sm_c, cm_c),
                                               w_ff1[l], w_ff2[l])
    return x
```

```python
import functools
import math

import jax
import jax.numpy as jnp
from jax import lax
from jax.experimental import pallas as pl
from jax.experimental.pallas import tpu as pltpu

F32 = jnp.float32
BF16 = jnp.bfloat16

ATTN_HEADS = 8
HEAD_DIM = 64
VALUE_DIM = 2 * HEAD_DIM
GRID_W = 64
ROPE_THETA = 10000.0
POOL_WINDOWS = (2, 4, 8, 16)
POOL_HALO = 16
EPS = 1e-6

V7X_LANES = 128
V7X_VMEM_BYTES = 64 * 1024 * 1024
V7X_VMEM_CAP = V7X_VMEM_BYTES - 6 * 1024 * 1024


def _vmem_limit(block_bytes):
    return int(min(V7X_VMEM_CAP, block_bytes * 1.25 + 8 * 1024 * 1024))


def _nbytes(shape, dtype):
    return math.prod(shape) * jnp.dtype(dtype).itemsize


def _mod_kernel(c_ref, w_ref, b_ref, o_ref):
    cv = c_ref[...]
    act = cv * jax.nn.sigmoid(cv)
    o_ref[...] = jnp.dot(act, w_ref[...], preferred_element_type=F32) + b_ref[...]


def _modulation(cvec, w_mod, b_mod, tn=1024):
    rows, d = cvec.shape
    n = w_mod.shape[1]
    blocks = 2 * (_nbytes((d, tn), F32) + _nbytes((rows, tn), F32) * 2) + _nbytes((rows, d), F32)
    return pl.pallas_call(
        _mod_kernel,
        grid=(n // tn,),
        in_specs=[
            pl.BlockSpec((rows, d), lambda j: (0, 0)),
            pl.BlockSpec((d, tn), lambda j: (0, j)),
            pl.BlockSpec((1, tn), lambda j: (0, j)),
        ],
        out_specs=pl.BlockSpec((rows, tn), lambda j: (0, j)),
        out_shape=jax.ShapeDtypeStruct((rows, n), F32),
        compiler_params=pltpu.CompilerParams(
            dimension_semantics=("parallel",), vmem_limit_bytes=_vmem_limit(blocks)),
        name="modulation",
    )(cvec, w_mod, b_mod)


def _modulated_norm(x, norm_w, shift, scale):
    y = x * lax.rsqrt(jnp.mean(x * x, axis=-1, keepdims=True) + EPS)
    return (y * norm_w) * (1.0 + scale) + shift


def _head_norm_rope(acc, w128, cos, sin_prev, sin_next):
    lane = lax.broadcasted_iota(jnp.int32, acc.shape, 1)
    first = lane < HEAD_DIM
    sq = acc * acc
    tot = jnp.sum(sq, axis=-1, keepdims=True)
    lo = jnp.sum(jnp.where(first, sq, 0.0), axis=-1, keepdims=True)
    ms = jnp.where(first, lo, tot - lo) * (1.0 / HEAD_DIM)
    y = acc * lax.rsqrt(ms + EPS) * w128
    quarter = HEAD_DIM // 4
    return (y * cos + pltpu.roll(y, quarter, 1) * sin_prev
            + pltpu.roll(y, V7X_LANES - quarter, 1) * sin_next)


def _inproj_kernel(x_ref, nw_ref, shift_ref, scale_ref, w_ref, qw_ref, kw_ref,
                   cos_ref, sp_ref, sn_ref, *rest, kinds):
    out_names = tuple(dict.fromkeys(kinds))
    outs = dict(zip(out_names, rest[:len(out_names)]))
    h_ref = rest[len(out_names)]
    j = pl.program_id(2)

    @pl.when(j == 0)
    def _():
        h = _modulated_norm(x_ref[0], nw_ref[...], shift_ref[0], scale_ref[0])
        h_ref[...] = h.astype(BF16)

    def project():
        return jnp.dot(h_ref[...], w_ref[...], preferred_element_type=F32)

    def qk_epilogue(o_ref, w128_ref):
        acc = project()
        cos, sp, sn = cos_ref[...], sp_ref[...], sn_ref[...]
        w128 = w128_ref[...]
        for c in range(acc.shape[1] // V7X_LANES):
            sl = slice(c * V7X_LANES, (c + 1) * V7X_LANES)
            o_ref[0, :, sl] = _head_norm_rope(acc[:, sl], w128, cos, sp, sn).astype(BF16)

    for jj, kind in enumerate(kinds):
        if kind == "g" and kinds.index("g") != jj:
            continue

        if kind == "q":
            pl.when(j == jj)(functools.partial(qk_epilogue, outs["q"], qw_ref))
        elif kind == "k":
            pl.when(j == jj)(functools.partial(qk_epilogue, outs["k"], kw_ref))
        elif kind == "g":
            @pl.when(j >= jj)
            def _():
                outs["g"][0] = jax.nn.sigmoid(project()).astype(BF16)
        else:
            def plain(o_ref):
                o_ref[0] = project().astype(BF16)
            pl.when(j == jj)(functools.partial(plain, outs[kind]))


def _input_projection(x, norm_w, shift, scale, w_in_bf, col_block0, kinds, qw128, kw128,
                      cos, sin_prev, sin_next, tm, tn=1024):
    B, L, D = x.shape
    out_names = tuple(dict.fromkeys(kinds))
    first = {n: kinds.index(n) for n in out_names}
    count = {n: kinds.count(n) for n in out_names}

    def out_map(name):
        j0, nblk = first[name], count[name]
        return lambda b, i, j: (b, i, jnp.clip(j - j0, 0, nblk - 1))

    vec = lambda: pl.BlockSpec((1, 1, D), lambda b, i, j: (b, 0, 0))
    tab = lambda: pl.BlockSpec((tm, V7X_LANES), lambda b, i, j: (i, 0))
    in_specs = [
        pl.BlockSpec((1, tm, D), lambda b, i, j: (b, i, 0)),
        pl.BlockSpec((1, D), lambda b, i, j: (0, 0)),
        vec(), vec(),
        pl.BlockSpec((D, tn), lambda b, i, j: (0, col_block0 + j)),
        pl.BlockSpec((1, V7X_LANES), lambda b, i, j: (0, 0)),
        pl.BlockSpec((1, V7X_LANES), lambda b, i, j: (0, 0)),
        tab(), tab(), tab(),
    ]
    out_specs = [pl.BlockSpec((1, tm, tn), out_map(n)) for n in out_names]
    out_shape = [jax.ShapeDtypeStruct((B, L, tn * count[n]), BF16) for n in out_names]
    blocks = (2 * (_nbytes((tm, D), F32) + _nbytes((D, tn), BF16) + 3 * _nbytes((tm, V7X_LANES), F32)
                   + len(out_names) * _nbytes((tm, tn), BF16))
              + _nbytes((tm, D), BF16) + 2 * _nbytes((tm, tn), F32))
    outs = pl.pallas_call(
        functools.partial(_inproj_kernel, kinds=tuple(kinds)),
        grid=(B, L // tm, len(kinds)),
        in_specs=in_specs,
        out_specs=out_specs,
        out_shape=out_shape,
        scratch_shapes=[pltpu.VMEM((tm, D), BF16)],
        compiler_params=pltpu.CompilerParams(
            dimension_semantics=("parallel", "parallel", "arbitrary"),
            vmem_limit_bytes=_vmem_limit(blocks)),
        name="input_projection",
    )(x, norm_w, shift, scale, w_in_bf, qw128, kw128, cos, sin_prev, sin_next)
    return dict(zip(out_names, outs))


def _attn_kernel(lq1_ref, lk1_ref, lq2_ref, lk2_ref, q_ref, k_ref, v_ref, kc_ref, vc_ref, sw_ref,
                 o_ref, m_ref, l_ref, acc_ref, *, tq, tk, lam_init):
    q = q_ref[0]
    lane = lax.broadcasted_iota(jnp.int32, q.shape, 1)
    zero = jnp.zeros_like(q)
    qs = jnp.concatenate([jnp.where(lane < HEAD_DIM, q, zero),
                          jnp.where(lane >= HEAD_DIM, q, zero)], axis=0)

    m_ref[...] = jnp.full_like(m_ref, -jnp.inf)
    l_ref[...] = jnp.zeros_like(l_ref)
    acc_ref[...] = jnp.zeros_like(acc_ref)

    def step(kt, vt):
        s = lax.dot_general(qs, kt, (((1,), (1,)), ((), ())), preferred_element_type=F32)
        m_prev = m_ref[...]
        m_new = jnp.maximum(m_prev, jnp.max(s, axis=-1, keepdims=True))
        alpha = jnp.exp(m_prev - m_new)
        p = jnp.exp(s - m_new)
        l_ref[...] = alpha * l_ref[...] + jnp.sum(p, axis=-1, keepdims=True)
        acc_ref[...] = alpha * acc_ref[...] + jnp.dot(p.astype(BF16), vt, preferred_element_type=F32)
        m_ref[...] = m_new

    def body(i, carry):
        off = pl.multiple_of(i * tk, tk)
        step(k_ref[0, pl.ds(off, tk), :], v_ref[0, pl.ds(off, tk), :])
        return carry

    lax.fori_loop(0, k_ref.shape[1] // tk, body, 0)
    step(kc_ref[0], vc_ref[0])

    lam = (jnp.exp(jnp.sum(lq1_ref[...] * lk1_ref[...])) - jnp.exp(jnp.sum(lq2_ref[...] * lk2_ref[...]))
           + lam_init)
    o = acc_ref[...] / l_ref[...]
    od = o[:tq] - lam * o[tq:]
    y = od * lax.rsqrt(jnp.mean(od * od, axis=-1, keepdims=True) + EPS) * sw_ref[...]
    o_ref[0] = (y * (1.0 - lam_init)).astype(BF16)


def _diff_attention(q, k, v, k_c, v_c, lams, subln_w, lam_init, tq, tk):
    B, L, W = q.shape
    Lc = k_c.shape[1]
    H = W // VALUE_DIM
    lam_spec = pl.BlockSpec((1, HEAD_DIM), lambda b, h, i: (0, 0))
    blocks = (2 * (2 * _nbytes((tq, VALUE_DIM), BF16) + 2 * _nbytes((L, VALUE_DIM), BF16)
                   + 2 * _nbytes((Lc, VALUE_DIM), BF16))
              + 3 * _nbytes((2 * tq, VALUE_DIM), F32) + 4 * _nbytes((2 * tq, tk), F32))
    return pl.pallas_call(
        functools.partial(_attn_kernel, tq=tq, tk=tk, lam_init=lam_init),
        grid=(B, H, L // tq),
        in_specs=[
            lam_spec, lam_spec, lam_spec, lam_spec,
            pl.BlockSpec((1, tq, VALUE_DIM), lambda b, h, i: (b, i, h)),
            pl.BlockSpec((1, L, VALUE_DIM), lambda b, h, i: (b, 0, h)),
            pl.BlockSpec((1, L, VALUE_DIM), lambda b, h, i: (b, 0, h)),
            pl.BlockSpec((1, Lc, VALUE_DIM), lambda b, h, i: (b, 0, h)),
            pl.BlockSpec((1, Lc, VALUE_DIM), lambda b, h, i: (b, 0, h)),
            pl.BlockSpec((1, VALUE_DIM), lambda b, h, i: (0, 0)),
        ],
        out_specs=pl.BlockSpec((1, tq, VALUE_DIM), lambda b, h, i: (b, i, h)),
        out_shape=jax.ShapeDtypeStruct((B, L, W), BF16),
        scratch_shapes=[pltpu.VMEM((2 * tq, 1), F32), pltpu.VMEM((2 * tq, 1), F32),
                        pltpu.VMEM((2 * tq, VALUE_DIM), F32)],
        compiler_params=pltpu.CompilerParams(
            dimension_semantics=("parallel", "parallel", "parallel"),
            vmem_limit_bytes=_vmem_limit(blocks)),
        name="diff_attention",
    )(*lams, q, k, v, k_c, v_c, subln_w)


def _mixer_kernel(heads_ref, u_ref, uprev_ref, unext_ref, g_ref, x_ref, gate_ref,
                  wa_ref, wb_ref, pw_ref, ps_ref, wo_ref, o_ref, ubuf_ref, y_ref, *, tm, seq_len):
    i = pl.program_id(1)
    nt = pl.num_programs(1)
    D = x_ref.shape[2]
    group = u_ref.shape[2] // len(POOL_WINDOWS)

    ubuf_ref[pl.ds(POOL_HALO, tm), :] = u_ref[0].astype(F32)
    ubuf_ref[pl.ds(0, POOL_HALO), :] = jnp.where(i > 0, uprev_ref[0].astype(F32), 0.0)
    ubuf_ref[pl.ds(POOL_HALO + tm, POOL_HALO), :] = jnp.where(i < nt - 1, unext_ref[0].astype(F32), 0.0)

    t = i * tm + lax.broadcasted_iota(jnp.int32, (tm, 1), 0)
    for gi, w in enumerate(POOL_WINDOWS):
        cols = slice(gi * group, (gi + 1) * group)
        total = None
        for off in range(-(w // 2), w - w // 2):
            part = ubuf_ref[pl.ds(POOL_HALO + off, tm), cols]
            total = part if total is None else total + part
        lo = jnp.maximum(t - w // 2, 0)
        hi = jnp.minimum(t + (w - w // 2), seq_len)
        d = total / (hi - lo).astype(F32) - ubuf_ref[pl.ds(POOL_HALO, tm), cols]
        yg = jnp.dot(d.astype(BF16), pw_ref[gi], preferred_element_type=F32)
        y_ref[:, cols] = (yg * ps_ref[:, cols]).astype(BF16)

    y_a = jnp.dot(heads_ref[0], wa_ref[...], preferred_element_type=F32)
    y_b = jnp.dot(y_ref[...], wb_ref[...], preferred_element_type=F32)
    mixed = g_ref[0, :, :D].astype(F32) * y_a + g_ref[0, :, D:].astype(F32) * y_b
    mix = jnp.dot(mixed.astype(BF16), wo_ref[...], preferred_element_type=F32)
    o_ref[0] = x_ref[0] + gate_ref[0] * mix


def _token_mixer(heads, u, g, x, gate, w_a_up, w_b_up, pool_w, pool_scale, w_o, tm):
    B, L, D = x.shape
    AW, PW = heads.shape[2], u.shape[2]
    hb = tm // POOL_HALO
    n_halo = L // POOL_HALO
    const = lambda shape: pl.BlockSpec(shape, lambda b, i: (0,) * len(shape),
                                       pipeline_mode=pl.Buffered(1))
    weights = (_nbytes((AW, D), BF16) + _nbytes((PW, D), BF16) + _nbytes(pool_w.shape, BF16)
               + _nbytes((D, D), BF16))
    blocks = (weights + 2 * (_nbytes((tm, AW), BF16) + _nbytes((tm, PW), BF16)
                             + _nbytes((tm, 2 * D), BF16) + 2 * _nbytes((tm, D), F32))
              + _nbytes((tm + 2 * POOL_HALO, PW), F32) + _nbytes((tm, PW), BF16)
              + 4 * _nbytes((tm, D), F32))
    return pl.pallas_call(
        functools.partial(_mixer_kernel, tm=tm, seq_len=L),
        grid=(B, L // tm),
        in_specs=[
            pl.BlockSpec((1, tm, AW), lambda b, i: (b, i, 0)),
            pl.BlockSpec((1, tm, PW), lambda b, i: (b, i, 0)),
            pl.BlockSpec((1, POOL_HALO, PW), lambda b, i: (b, jnp.maximum(i * hb - 1, 0), 0)),
            pl.BlockSpec((1, POOL_HALO, PW), lambda b, i: (b, jnp.minimum((i + 1) * hb, n_halo - 1), 0)),
            pl.BlockSpec((1, tm, 2 * D), lambda b, i: (b, i, 0)),
            pl.BlockSpec((1, tm, D), lambda b, i: (b, i, 0)),
            pl.BlockSpec((1, 1, D), lambda b, i: (b, 0, 0)),
            const((AW, D)), const((PW, D)), const(pool_w.shape), const((1, PW)), const((D, D)),
        ],
        out_specs=pl.BlockSpec((1, tm, D), lambda b, i: (b, i, 0)),
        out_shape=jax.ShapeDtypeStruct((B, L, D), F32),
        scratch_shapes=[pltpu.VMEM((tm + 2 * POOL_HALO, PW), F32), pltpu.VMEM((tm, PW), BF16)],
        compiler_params=pltpu.CompilerParams(
            dimension_semantics=("parallel", "parallel"),
            vmem_limit_bytes=_vmem_limit(blocks)),
        name="token_mixer",
    )(heads, u, u, u, g, x, gate, w_a_up, w_b_up, pool_w, pool_scale, w_o)


def _mlp_kernel(x_ref, nw_ref, shift_ref, scale_ref, gate_ref, w1_ref, w2_ref, o_ref, h_ref):
    j = pl.program_id(2)

    @pl.when(j == 0)
    def _():
        h = _modulated_norm(x_ref[0], nw_ref[...], shift_ref[0], scale_ref[0])
        h_ref[...] = h.astype(BF16)

        o_ref[...] = jnp.zeros_like(o_ref)

    a = jnp.maximum(jnp.dot(h_ref[...], w1_ref[...], preferred_element_type=F32), 0.0)
    a2 = (a * a).astype(BF16)
    tf = w1_ref.shape[1]
    for n in range(0, o_ref.shape[2], tf):
        o_ref[0, :, n:n + tf] += jnp.dot(a2, w2_ref[:, n:n + tf], preferred_element_type=F32)

    @pl.when(j == pl.num_programs(2) - 1)
    def _():
        o_ref[0] = x_ref[0] + gate_ref[0] * o_ref[0]


def _mlp(x, norm_w, shift, scale, gate, w1, w2, tm, tf):
    B, L, D = x.shape
    F = w1.shape[1]
    vec = lambda: pl.BlockSpec((1, 1, D), lambda b, i, j: (b, 0, 0))
    blocks = (2 * (2 * _nbytes((tm, D), F32) + _nbytes((D, tf), BF16) + _nbytes((tf, D), BF16))
              + _nbytes((tm, D), BF16) + 2 * _nbytes((tm, tf), F32) + _nbytes((tm, D), F32))
    return pl.pallas_call(
        _mlp_kernel,
        grid=(B, L // tm, F // tf),
        in_specs=[
            pl.BlockSpec((1, tm, D), lambda b, i, j: (b, i, 0)),
            pl.BlockSpec((1, D), lambda b, i, j: (0, 0)),
            vec(), vec(), vec(),
            pl.BlockSpec((D, tf), lambda b, i, j: (0, j)),
            pl.BlockSpec((tf, D), lambda b, i, j: (j, 0)),
        ],
        out_specs=pl.BlockSpec((1, tm, D), lambda b, i, j: (b, i, 0)),
        out_shape=jax.ShapeDtypeStruct((B, L, D), F32),
        scratch_shapes=[pltpu.VMEM((tm, D), BF16)],
        compiler_params=pltpu.CompilerParams(
            dimension_semantics=("parallel", "parallel", "arbitrary"),
            vmem_limit_bytes=_vmem_limit(blocks)),
        name="sq_relu_mlp",
    )(x, norm_w, shift, scale, gate, w1, w2)


def _rope_tables(seq_len):
    t = jnp.arange(seq_len)
    row, col = t // GRID_W, t % GRID_W
    half = HEAD_DIM // 2
    quarter = half // 2
    inv_freq = ROPE_THETA ** (-jnp.arange(0, half, 2, dtype=F32) / half)
    lane = jnp.arange(V7X_LANES)
    e = lane % HEAD_DIM
    pos = jnp.where((e // half)[None, :] == 0, row[:, None], col[:, None]).astype(F32)
    ang = pos * inv_freq[e % quarter][None, :]
    cos, sin = jnp.cos(ang), jnp.sin(ang)
    second = ((e % half) >= quarter)[None, :]
    return cos, jnp.where(second, sin, 0.0), jnp.where(second, 0.0, -sin)


def _identity_tables(seq_len):
    ones = jnp.ones((seq_len, V7X_LANES), F32)
    return ones, jnp.zeros_like(ones), jnp.zeros_like(ones)


def _pick(n, pref):
    t = min(pref, n)
    while n % t:
        t //= 2
    return t


def kernel(x, c, ctx, c_ctx, w_mod, b_mod, norm_attn_w, w_in, q_norm_w, k_norm_w, lambda_q1, lambda_k1,
           lambda_q2, lambda_k2, subln_w, pool_w, pool_scale, w_a_up, w_b_up, w_o, norm_mlp_w, w_ff1, w_ff2):
    B, L, D = x.shape
    Lc = ctx.shape[1]
    depth = w_mod.shape[0]
    qk_w = 2 * ATTN_HEADS * HEAD_DIM
    tn = qk_w
    assert depth == 1 and w_in.shape[2] == 4 * tn + 2 * D and L % GRID_W == 0

    for l in range(depth):
        lam_init = 0.8 - 0.6 * math.exp(-0.3 * l)

        rows = 8
        cvec = jnp.zeros((rows, D), F32).at[:B].set(c).at[B].set(c_ctx)
        mod = _modulation(cvec, w_mod[l], b_mod[l][None, :])
        sa, ca, ga, sm, cm, gm = [mod[:B, None, k * D:(k + 1) * D] for k in range(6)]
        sa_c, ca_c = [jnp.broadcast_to(mod[B, k * D:(k + 1) * D], (B, 1, D)) for k in range(2)]

        w_in_bf = w_in[l].astype(BF16)
        qw128 = (jnp.tile(q_norm_w[l], 2) * HEAD_DIM ** -0.5)[None, :]
        kw128 = jnp.tile(k_norm_w[l], 2)[None, :]
        nw = norm_attn_w[l][None, :]

        ctx_p = _input_projection(ctx, nw, sa_c, ca_c, w_in_bf, 1, ("k", "v"), qw128, kw128,
                                  *_identity_tables(Lc), tm=_pick(Lc, 256), tn=tn)
        p = _input_projection(x, nw, sa, ca, w_in_bf, 0, ("q", "k", "v", "u", "g", "g", "g", "g"),
                              qw128, kw128, *_rope_tables(L), tm=_pick(L, 512), tn=tn)

        lams = [v[l][None, :] for v in (lambda_q1, lambda_k1, lambda_q2, lambda_k2)]
        heads = _diff_attention(p["q"], p["k"], p["v"], ctx_p["k"], ctx_p["v"], lams,
                                subln_w[l][None, :], lam_init, tq=_pick(L, 256), tk=_pick(L, 512))

        x = _token_mixer(heads, p["u"], p["g"], x, ga, w_a_up[l].astype(BF16), w_b_up[l].astype(BF16),
                         pool_w[l].astype(BF16), pool_scale[l][None, :], w_o[l].astype(BF16),
                         tm=_pick(L, 256))
        x = _mlp(x, norm_mlp_w[l][None, :], sm, cm, gm, w_ff1[l].astype(BF16), w_ff2[l].astype(BF16),
                 tm=_pick(L, 1024), tf=512)
    return x
```

```python
import functools
import math

import jax
import jax.numpy as jnp
from jax import lax
from jax.experimental import pallas as pl
from jax.experimental.pallas import tpu as pltpu

F32 = jnp.float32
BF16 = jnp.bfloat16

ATTN_HEADS = 8
HEAD_DIM = 64
VALUE_DIM = 2 * HEAD_DIM
GRID_W = 64
ROPE_THETA = 10000.0
POOL_WINDOWS = (2, 4, 8, 16)
POOL_HALO = 16
EPS = 1e-6

V7X_LANES = 128
V7X_VMEM_BYTES = 64 * 1024 * 1024
V7X_VMEM_CAP = V7X_VMEM_BYTES - 6 * 1024 * 1024


def _vmem_limit(block_bytes):
    return int(min(V7X_VMEM_CAP, block_bytes * 1.25 + 8 * 1024 * 1024))


def _nbytes(shape, dtype):
    return math.prod(shape) * jnp.dtype(dtype).itemsize


def _mod_kernel(c_ref, w_ref, b_ref, o_ref):
    cv = c_ref[...]
    act = cv * jax.nn.sigmoid(cv)
    o_ref[...] = jnp.dot(act, w_ref[...], preferred_element_type=F32) + b_ref[...]


def _modulation(cvec, w_mod, b_mod, tn=1024):
    rows, d = cvec.shape
    n = w_mod.shape[1]
    blocks = 2 * (_nbytes((d, tn), F32) + _nbytes((rows, tn), F32) * 2) + _nbytes((rows, d), F32)
    return pl.pallas_call(
        _mod_kernel,
        grid=(n // tn,),
        in_specs=[
            pl.BlockSpec((rows, d), lambda j: (0, 0)),
            pl.BlockSpec((d, tn), lambda j: (0, j)),
            pl.BlockSpec((1, tn), lambda j: (0, j)),
        ],
        out_specs=pl.BlockSpec((rows, tn), lambda j: (0, j)),
        out_shape=jax.ShapeDtypeStruct((rows, n), F32),
        compiler_params=pltpu.CompilerParams(
            dimension_semantics=("parallel",), vmem_limit_bytes=_vmem_limit(blocks)),
        name="modulation",
    )(cvec, w_mod, b_mod)


def _modulated_norm(x, norm_w, shift, scale):
    y = x * lax.rsqrt(jnp.mean(x * x, axis=-1, keepdims=True) + EPS)
    return (y * norm_w) * (1.0 + scale) + shift


def _head_norm_rope(acc, w128, cos, sin_prev, sin_next):
    lane = lax.broadcasted_iota(jnp.int32, acc.shape, 1)
    first = lane < HEAD_DIM
    sq = acc * acc
    tot = jnp.sum(sq, axis=-1, keepdims=True)
    lo = jnp.sum(jnp.where(first, sq, 0.0), axis=-1, keepdims=True)
    ms = jnp.where(first, lo, tot - lo) * (1.0 / HEAD_DIM)
    y = acc * lax.rsqrt(ms + EPS) * w128
    quarter = HEAD_DIM // 4
    return (y * cos + pltpu.roll(y, quarter, 1) * sin_prev
            + pltpu.roll(y, V7X_LANES - quarter, 1) * sin_next)


def _inproj_kernel(x_ref, nw_ref, shift_ref, scale_ref, w_ref, qw_ref, kw_ref,
                   cos_ref, sp_ref, sn_ref, *rest, kinds):
    out_names = tuple(dict.fromkeys(kinds))
    outs = dict(zip(out_names, rest[:len(out_names)]))
    h_ref = rest[len(out_names)]
    j = pl.program_id(2)

    @pl.when(j == 0)
    def _():
        h = _modulated_norm(x_ref[0], nw_ref[...], shift_ref[0], scale_ref[0])
        h_ref[...] = h.astype(BF16)

    def project():
        return jnp.dot(h_ref[...], w_ref[...], preferred_element_type=F32)

    def qk_epilogue(o_ref, w128_ref):
        acc = project()
        cos, sp, sn = cos_ref[...], sp_ref[...], sn_ref[...]
        w128 = w128_ref[...]
        for c in range(acc.shape[1] // V7X_LANES):
            sl = slice(c * V7X_LANES, (c + 1) * V7X_LANES)
            o_ref[0, :, sl] = _head_norm_rope(acc[:, sl], w128, cos, sp, sn).astype(BF16)

    for jj, kind in enumerate(kinds):
        if kind == "g" and kinds.index("g") != jj:
            continue

        if kind == "q":
            pl.when(j == jj)(functools.partial(qk_epilogue, outs["q"], qw_ref))
        elif kind == "k":
            pl.when(j == jj)(functools.partial(qk_epilogue, outs["k"], kw_ref))
        elif kind == "g":
            @pl.when(j >= jj)
            def _():
                outs["g"][0] = jax.nn.sigmoid(project()).astype(BF16)
        else:
            def plain(o_ref):
                o_ref[0] = project().astype(BF16)
            pl.when(j == jj)(functools.partial(plain, outs[kind]))


def _input_projection(x, norm_w, shift, scale, w_in_bf, col_block0, kinds, qw128, kw128,
                      cos, sin_prev, sin_next, tm, tn=1024):
    B, L, D = x.shape
    out_names = tuple(dict.fromkeys(kinds))
    first = {n: kinds.index(n) for n in out_names}
    count = {n: kinds.count(n) for n in out_names}

    def out_map(name):
        j0, nblk = first[name], count[name]
        return lambda b, i, j: (b, i, jnp.clip(j - j0, 0, nblk - 1))

    vec = lambda: pl.BlockSpec((1, 1, D), lambda b, i, j: (b, 0, 0))
    tab = lambda: pl.BlockSpec((tm, V7X_LANES), lambda b, i, j: (i, 0))
    in_specs = [
        pl.BlockSpec((1, tm, D), lambda b, i, j: (b, i, 0)),
        pl.BlockSpec((1, D), lambda b, i, j: (0, 0)),
        vec(), vec(),
        pl.BlockSpec((D, tn), lambda b, i, j: (0, col_block0 + j)),
        pl.BlockSpec((1, V7X_LANES), lambda b, i, j: (0, 0)),
        pl.BlockSpec((1, V7X_LANES), lambda b, i, j: (0, 0)),
        tab(), tab(), tab(),
    ]
    out_specs = [pl.BlockSpec((1, tm, tn), out_map(n)) for n in out_names]
    out_shape = [jax.ShapeDtypeStruct((B, L, tn * count[n]), BF16) for n in out_names]
    blocks = (2 * (_nbytes((tm, D), F32) + _nbytes((D, tn), BF16) + 3 * _nbytes((tm, V7X_LANES), F32)
                   + len(out_names) * _nbytes((tm, tn), BF16))
              + _nbytes((tm, D), BF16) + 2 * _nbytes((tm, tn), F32))
    outs = pl.pallas_call(
        functools.partial(_inproj_kernel, kinds=tuple(kinds)),
        grid=(B, L // tm, len(kinds)),
        in_specs=in_specs,
        out_specs=out_specs,
        out_shape=out_shape,
        scratch_shapes=[pltpu.VMEM((tm, D), BF16)],
        compiler_params=pltpu.CompilerParams(
            dimension_semantics=("parallel", "parallel", "arbitrary"),
            vmem_limit_bytes=_vmem_limit(blocks)),
        name="input_projection",
    )(x, norm_w, shift, scale, w_in_bf, qw128, kw128, cos, sin_prev, sin_next)
    return dict(zip(out_names, outs))


def _attn_kernel(lq1_ref, lk1_ref, lq2_ref, lk2_ref, qt_ref, k_ref, vt_ref, kc_ref, vct_ref, sw_ref,
                 o_ref, rhs_ref, sa_ref, sb_ref, m_ref, l_ref, acc_ref, *, tq, tk, lam_init):
    qt = qt_ref[0]
    row = lax.broadcasted_iota(jnp.int32, qt.shape, 0)
    zero = jnp.zeros_like(qt)
    rhs_ref[:, :tq] = jnp.where(row < HEAD_DIM, qt, zero)
    rhs_ref[:, tq:] = jnp.where(row >= HEAD_DIM, qt, zero)

    def scores(kt):
        return jnp.dot(kt, rhs_ref[...], preferred_element_type=F32)

    def k_tile(t):
        return k_ref[0, pl.ds(pl.multiple_of(t * tk, tk), tk), :]

    def v_tile(t):
        return vt_ref[0, :, pl.ds(pl.multiple_of(t * tk, tk), tk)]

    s = scores(kc_ref[0])
    m0 = jnp.max(s, axis=0, keepdims=True)
    p = jnp.exp2(s - m0)
    m_ref[...] = m0
    l_ref[...] = jnp.sum(p, axis=0, keepdims=True)
    acc_ref[...] = jnp.dot(vct_ref[0], p.astype(BF16), preferred_element_type=F32)

    def update(s, vtt):
        m_prev = m_ref[...]
        m_new = jnp.maximum(m_prev, jnp.max(s, axis=0, keepdims=True))
        alpha = jnp.exp2(m_prev - m_new)
        p = jnp.exp2(s - m_new)
        l_ref[...] = alpha * l_ref[...] + jnp.sum(p, axis=0, keepdims=True)
        acc_ref[...] = alpha * acc_ref[...] + jnp.dot(vtt, p.astype(BF16), preferred_element_type=F32)
        m_ref[...] = m_new

    n_tiles = k_ref.shape[1] // tk
    sa_ref[...] = scores(k_tile(0))

    def body(j, carry):
        t = 2 * j
        sb_ref[...] = scores(k_tile(t + 1))
        update(sa_ref[...], v_tile(t))
        sa_ref[...] = scores(k_tile(jnp.minimum(t + 2, n_tiles - 1)))
        update(sb_ref[...], v_tile(t + 1))
        return carry

    lax.fori_loop(0, n_tiles // 2, body, 0)

    lam =(jnp.exp(jnp.sum(lq1_ref[...] * lk1_ref[...])) - jnp.exp(jnp.sum(lq2_ref[...] * lk2_ref[...]))
           + lam_init)
    o = acc_ref[...] / l_ref[...]
    od = o[:, :tq] - lam * o[:, tq:]
    y = od * lax.rsqrt(jnp.mean(od * od, axis=0, keepdims=True) + EPS) * sw_ref[...]
    o_ref[0] = (y * (1.0 - lam_init)).T.astype(BF16)


def _diff_attention(qt, k, vt, k_c, vt_c, lams, subln_w, lam_init, tq, tk):
    B, L, W = k.shape
    Lc = k_c.shape[1]
    H = W // VALUE_DIM
    assert (L // tk) % 2 == 0
    lam_spec = pl.BlockSpec((1, HEAD_DIM), lambda b, h, i: (0, 0))
    blocks = (2 * (2 * _nbytes((tq, VALUE_DIM), BF16) + 2 * _nbytes((L, VALUE_DIM), BF16)
                   + 2 * _nbytes((Lc, VALUE_DIM), BF16))
              + 3 * _nbytes((2 * tq, VALUE_DIM), F32) + 4 * _nbytes((2 * tq, tk), F32))
    return pl.pallas_call(
        functools.partial(_attn_kernel, tq=tq, tk=tk, lam_init=lam_init),
        grid=(B, H, L // tq),
        in_specs=[
            lam_spec, lam_spec, lam_spec, lam_spec,
            pl.BlockSpec((1, VALUE_DIM, tq), lambda b, h, i: (b, h, i)),
            pl.BlockSpec((1, L, VALUE_DIM), lambda b, h, i: (b, 0, h)),
            pl.BlockSpec((1, VALUE_DIM, L), lambda b, h, i: (b, h, 0)),
            pl.BlockSpec((1, Lc, VALUE_DIM), lambda b, h, i: (b, 0, h)),
            pl.BlockSpec((1, VALUE_DIM, Lc), lambda b, h, i: (b, h, 0)),
            pl.BlockSpec((VALUE_DIM, 1), lambda b, h, i: (0, 0)),
        ],
        out_specs=pl.BlockSpec((1, tq, VALUE_DIM), lambda b, h, i: (b, i, h)),
        out_shape=jax.ShapeDtypeStruct((B, L, W), BF16),
        scratch_shapes=[pltpu.VMEM((VALUE_DIM, 2 * tq), BF16),
                        pltpu.VMEM((tk, 2 * tq), F32), pltpu.VMEM((tk, 2 * tq), F32),
                        pltpu.VMEM((1, 2 * tq), F32), pltpu.VMEM((1, 2 * tq), F32),
                        pltpu.VMEM((VALUE_DIM, 2 * tq), F32)],
        compiler_params=pltpu.CompilerParams(
            dimension_semantics=("parallel", "parallel", "parallel"),
            vmem_limit_bytes=_vmem_limit(blocks)),
        name="diff_attention",
    )(*lams, qt, k, vt, k_c, vt_c, subln_w)


def _mixer_kernel(heads_ref, u_ref, uprev_ref, unext_ref, g_ref, x_ref, gate_ref,
                  wa_ref, wb_ref, pw_ref, ps_ref, wo_ref, o_ref, ubuf_ref, y_ref, *, tm, seq_len):
    i = pl.program_id(1)
    nt = pl.num_programs(1)
    D = x_ref.shape[2]
    group = u_ref.shape[2] // len(POOL_WINDOWS)

    ubuf_ref[pl.ds(POOL_HALO, tm), :] = u_ref[0].astype(F32)
    ubuf_ref[pl.ds(0, POOL_HALO), :] = jnp.where(i > 0, uprev_ref[0].astype(F32), 0.0)
    ubuf_ref[pl.ds(POOL_HALO + tm, POOL_HALO), :] = jnp.where(i < nt - 1, unext_ref[0].astype(F32), 0.0)

    t = i * tm + lax.broadcasted_iota(jnp.int32, (tm, 1), 0)
    for gi, w in enumerate(POOL_WINDOWS):
        cols = slice(gi * group, (gi + 1) * group)
        total = None
        for off in range(-(w // 2), w - w // 2):
            part = ubuf_ref[pl.ds(POOL_HALO + off, tm), cols]
            total = part if total is None else total + part
        lo = jnp.maximum(t - w // 2, 0)
        hi = jnp.minimum(t + (w - w // 2), seq_len)
        d = total / (hi - lo).astype(F32) - ubuf_ref[pl.ds(POOL_HALO, tm), cols]
        yg = jnp.dot(d.astype(BF16), pw_ref[gi], preferred_element_type=F32)
        y_ref[:, cols] = (yg * ps_ref[:, cols]).astype(BF16)

    y_a = jnp.dot(heads_ref[0], wa_ref[...], preferred_element_type=F32)
    y_b = jnp.dot(y_ref[...], wb_ref[...], preferred_element_type=F32)
    mixed = g_ref[0, :, :D].astype(F32) * y_a + g_ref[0, :, D:].astype(F32) * y_b
    mix = jnp.dot(mixed.astype(BF16), wo_ref[...], preferred_element_type=F32)
    o_ref[0] = x_ref[0] + gate_ref[0] * mix


def _token_mixer(heads, u, g, x, gate, w_a_up, w_b_up, pool_w, pool_scale, w_o, tm):
    B, L, D = x.shape
    AW, PW = heads.shape[2], u.shape[2]
    hb = tm // POOL_HALO
    n_halo = L // POOL_HALO
    const = lambda shape: pl.BlockSpec(shape, lambda b, i: (0,) * len(shape),
                                       pipeline_mode=pl.Buffered(1))
    weights = (_nbytes((AW, D), BF16) + _nbytes((PW, D), BF16) + _nbytes(pool_w.shape, BF16)
               + _nbytes((D, D), BF16))
    blocks = (weights + 2 * (_nbytes((tm, AW), BF16) + _nbytes((tm, PW), BF16)
                             + _nbytes((tm, 2 * D), BF16) + 2 * _nbytes((tm, D), F32))
              + _nbytes((tm + 2 * POOL_HALO, PW), F32) + _nbytes((tm, PW), BF16)
              + 4 * _nbytes((tm, D), F32))
    return pl.pallas_call(
        functools.partial(_mixer_kernel, tm=tm, seq_len=L),
        grid=(B, L // tm),
        in_specs=[
            pl.BlockSpec((1, tm, AW), lambda b, i: (b, i, 0)),
            pl.BlockSpec((1, tm, PW), lambda b, i: (b, i, 0)),
            pl.BlockSpec((1, POOL_HALO, PW), lambda b, i: (b, jnp.maximum(i * hb - 1, 0), 0)),
            pl.BlockSpec((1, POOL_HALO, PW), lambda b, i: (b, jnp.minimum((i + 1) * hb, n_halo - 1), 0)),
            pl.BlockSpec((1, tm, 2 * D), lambda b, i: (b, i, 0)),
            pl.BlockSpec((1, tm, D), lambda b, i: (b, i, 0)),
            pl.BlockSpec((1, 1, D), lambda b, i: (b, 0, 0)),
            const((AW, D)), const((PW, D)), const(pool_w.shape), const((1, PW)), const((D, D)),
        ],
        out_specs=pl.BlockSpec((1, tm, D), lambda b, i: (b, i, 0)),
        out_shape=jax.ShapeDtypeStruct((B, L, D), F32),
        scratch_shapes=[pltpu.VMEM((tm + 2 * POOL_HALO, PW), F32), pltpu.VMEM((tm, PW), BF16)],
        compiler_params=pltpu.CompilerParams(
            dimension_semantics=("parallel", "parallel"),
            vmem_limit_bytes=_vmem_limit(blocks)),
        name="token_mixer",
    )(heads, u, u, u, g, x, gate, w_a_up, w_b_up, pool_w, pool_scale, w_o)


def _mlp_kernel(x_ref, nw_ref, shift_ref, scale_ref, gate_ref, w1_ref, w2_ref, o_ref, h_ref):
    j = pl.program_id(2)

    @pl.when(j == 0)
    def _():
        h = _modulated_norm(x_ref[0], nw_ref[...], shift_ref[0], scale_ref[0])
        h_ref[...] = h.astype(BF16)

        o_ref[...] = jnp.zeros_like(o_ref)

    a = jnp.maximum(jnp.dot(h_ref[...], w1_ref[...], preferred_element_type=F32), 0.0)
    a2 = (a * a).astype(BF16)
    tf = w1_ref.shape[1]
    for n in range(0, o_ref.shape[2], tf):
        o_ref[0, :, n:n + tf] += jnp.dot(a2, w2_ref[:, n:n + tf], preferred_element_type=F32)

    @pl.when(j == pl.num_programs(2) - 1)
    def _():
        o_ref[0] = x_ref[0] + gate_ref[0] * o_ref[0]


def _mlp(x, norm_w, shift, scale, gate, w1, w2, tm, tf):
    B, L, D = x.shape
    F = w1.shape[1]
    vec = lambda: pl.BlockSpec((1, 1, D), lambda b, i, j: (b, 0, 0))
    blocks = (2 * (2 * _nbytes((tm, D), F32) + _nbytes((D, tf), BF16) + _nbytes((tf, D), BF16))
              + _nbytes((tm, D), BF16) + 2 * _nbytes((tm, tf), F32) + _nbytes((tm, D), F32))
    return pl.pallas_call(
        _mlp_kernel,
        grid=(B, L // tm, F // tf),
        in_specs=[
            pl.BlockSpec((1, tm, D), lambda b, i, j: (b, i, 0)),
            pl.BlockSpec((1, D), lambda b, i, j: (0, 0)),
            vec(), vec(), vec(),
            pl.BlockSpec((D, tf), lambda b, i, j: (0, j)),
            pl.BlockSpec((tf, D), lambda b, i, j: (j, 0)),
        ],
        out_specs=pl.BlockSpec((1, tm, D), lambda b, i, j: (b, i, 0)),
        out_shape=jax.ShapeDtypeStruct((B, L, D), F32),
        scratch_shapes=[pltpu.VMEM((tm, D), BF16)],
        compiler_params=pltpu.CompilerParams(
            dimension_semantics=("parallel", "parallel", "arbitrary"),
            vmem_limit_bytes=_vmem_limit(blocks)),
        name="sq_relu_mlp",
    )(x, norm_w, shift, scale, gate, w1, w2)


def _rope_tables(seq_len):
    t = jnp.arange(seq_len)
    row, col = t // GRID_W, t % GRID_W
    half = HEAD_DIM // 2
    quarter = half // 2
    inv_freq = ROPE_THETA ** (-jnp.arange(0, half, 2, dtype=F32) / half)
    lane = jnp.arange(V7X_LANES)
    e = lane % HEAD_DIM
    pos = jnp.where((e // half)[None, :] == 0, row[:, None], col[:, None]).astype(F32)
    ang = pos * inv_freq[e % quarter][None, :]
    cos, sin = jnp.cos(ang), jnp.sin(ang)
    second = ((e % half) >= quarter)[None, :]
    return cos, jnp.where(second, sin, 0.0), jnp.where(second, 0.0, -sin)


def _identity_tables(seq_len):
    ones = jnp.ones((seq_len, V7X_LANES), F32)
    return ones, jnp.zeros_like(ones), jnp.zeros_like(ones)


def _pick(n, pref):
    t = min(pref, n)
    while n % t:
        t //= 2
    return t


def kernel(x, c, ctx, c_ctx, w_mod, b_mod, norm_attn_w, w_in, q_norm_w, k_norm_w, lambda_q1, lambda_k1,
           lambda_q2, lambda_k2, subln_w, pool_w, pool_scale, w_a_up, w_b_up, w_o, norm_mlp_w, w_ff1, w_ff2):
    B, L, D = x.shape
    Lc = ctx.shape[1]
    depth = w_mod.shape[0]
    qk_w = 2 * ATTN_HEADS * HEAD_DIM
    tn = qk_w
    assert depth == 1 and w_in.shape[2] == 4 * tn + 2 * D and L % GRID_W == 0

    for l in range(depth):
        lam_init = 0.8 - 0.6 * math.exp(-0.3 * l)

        rows = 8
        cvec = jnp.zeros((rows, D), F32).at[:B].set(c).at[B].set(c_ctx)
        mod = _modulation(cvec, w_mod[l], b_mod[l][None, :])
        sa, ca, ga, sm, cm, gm = [mod[:B, None, k * D:(k + 1) * D] for k in range(6)]
        sa_c, ca_c = [jnp.broadcast_to(mod[B, k * D:(k + 1) * D], (B, 1, D)) for k in range(2)]

        w_in_bf = w_in[l].astype(BF16)
        qw128 = (jnp.tile(q_norm_w[l], 2) * (HEAD_DIM ** -0.5 * math.log2(math.e)))[None, :]
        kw128 = jnp.tile(k_norm_w[l], 2)[None, :]
        nw = norm_attn_w[l][None, :]

        ctx_p = _input_projection(ctx, nw, sa_c, ca_c, w_in_bf, 1, ("k", "v"), qw128, kw128,
                                  *_identity_tables(Lc), tm=_pick(Lc, 256), tn=tn)
        p = _input_projection(x, nw, sa, ca, w_in_bf, 0, ("q", "k", "v", "u", "g", "g", "g", "g"),
                              qw128, kw128, *_rope_tables(L), tm=_pick(L, 512), tn=tn)

        lams = [v[l][None, :] for v in (lambda_q1, lambda_k1, lambda_q2, lambda_k2)]
        swap = lambda a: jnp.swapaxes(a, 1, 2)
        heads = _diff_attention(swap(p["q"]), p["k"], swap(p["v"]), ctx_p["k"], swap(ctx_p["v"]), lams,
                                subln_w[l][:, None], lam_init, tq=_pick(L, 256), tk=_pick(L, 512))

        x = _token_mixer(heads, p["u"], p["g"], x, ga, w_a_up[l].astype(BF16), w_b_up[l].astype(BF16),
                         pool_w[l].astype(BF16), pool_scale[l][None, :], w_o[l].astype(BF16),
                         tm=_pick(L, 256))
        x = _mlp(x, norm_mlp_w[l][None, :], sm, cm, gm, w_ff1[l].astype(BF16), w_ff2[l].astype(BF16),
                 tm=_pick(L, 1024), tf=512)
    return x
```

```python
import functools
import math

import jax
import jax.numpy as jnp
from jax import lax
from jax.experimental import pallas as pl
from jax.experimental.pallas import tpu as pltpu

F32 = jnp.float32
BF16 = jnp.bfloat16

ATTN_HEADS = 8
HEAD_DIM = 64
VALUE_DIM = 2 * HEAD_DIM
SCORE_BOUND = 64.0
BF16_SLACK = 1.0 + 2.0 ** -6
GRID_W = 64
ROPE_THETA = 10000.0
POOL_WINDOWS = (2, 4, 8, 16)
POOL_HALO = 16
EPS = 1e-6

V7X_LANES = 128
V7X_VMEM_BYTES = 64 * 1024 * 1024
V7X_VMEM_CAP = V7X_VMEM_BYTES - 6 * 1024 * 1024


def _vmem_limit(block_bytes):
    return int(min(V7X_VMEM_CAP, block_bytes * 1.25 + 8 * 1024 * 1024))


def _nbytes(shape, dtype):
    return math.prod(shape) * jnp.dtype(dtype).itemsize


def _mod_kernel(c_ref, w_ref, b_ref, o_ref):
    cv = c_ref[...]
    act = cv * jax.nn.sigmoid(cv)
    o_ref[...] = jnp.dot(act, w_ref[...], preferred_element_type=F32) + b_ref[...]


def _modulation(cvec, w_mod, b_mod, tn=1024):
    rows, d = cvec.shape
    n = w_mod.shape[1]
    blocks = 2 * (_nbytes((d, tn), F32) + _nbytes((rows, tn), F32) * 2) + _nbytes((rows, d), F32)
    return pl.pallas_call(
        _mod_kernel,
        grid=(n // tn,),
        in_specs=[
            pl.BlockSpec((rows, d), lambda j: (0, 0)),
            pl.BlockSpec((d, tn), lambda j: (0, j)),
            pl.BlockSpec((1, tn), lambda j: (0, j)),
        ],
        out_specs=pl.BlockSpec((rows, tn), lambda j: (0, j)),
        out_shape=jax.ShapeDtypeStruct((rows, n), F32),
        compiler_params=pltpu.CompilerParams(
            dimension_semantics=("parallel",), vmem_limit_bytes=_vmem_limit(blocks)),
        name="modulation",
    )(cvec, w_mod, b_mod)


def _modulated_norm(x, norm_w, shift, scale):
    y = x * lax.rsqrt(jnp.mean(x * x, axis=-1, keepdims=True) + EPS)
    return (y * norm_w) * (1.0 + scale) + shift


def _head_norm_rope(acc, w128, cos, sin_prev, sin_next):
    lane = lax.broadcasted_iota(jnp.int32, acc.shape, 1)
    first = lane < HEAD_DIM
    sq = acc * acc
    tot = jnp.sum(sq, axis=-1, keepdims=True)
    lo = jnp.sum(jnp.where(first, sq, 0.0), axis=-1, keepdims=True)
    ms = jnp.where(first, lo, tot - lo) * (1.0 / HEAD_DIM)
    y = acc * lax.rsqrt(ms + EPS) * w128
    quarter = HEAD_DIM // 4
    return (y * cos + pltpu.roll(y, quarter, 1) * sin_prev
            + pltpu.roll(y, V7X_LANES - quarter, 1) * sin_next)


def _inproj_kernel(x_ref, nw_ref, shift_ref, scale_ref, w_ref, qw_ref, kw_ref,
                   cos_ref, sp_ref, sn_ref, *rest, kinds):
    out_names = tuple(dict.fromkeys(kinds))
    outs = dict(zip(out_names, rest[:len(out_names)]))
    h_ref = rest[len(out_names)]
    j = pl.program_id(2)

    @pl.when(j == 0)
    def _():
        h = _modulated_norm(x_ref[0], nw_ref[...], shift_ref[0], scale_ref[0])
        h_ref[...] = h.astype(BF16)

    def project():
        return jnp.dot(h_ref[...], w_ref[...], preferred_element_type=F32)

    def qk_epilogue(o_ref, w128_ref):
        acc = project()
        cos, sp, sn = cos_ref[...], sp_ref[...], sn_ref[...]
        w128 = w128_ref[...]
        for c in range(acc.shape[1] // V7X_LANES):
            sl = slice(c * V7X_LANES, (c + 1) * V7X_LANES)
            o_ref[0, :, sl] = _head_norm_rope(acc[:, sl], w128, cos, sp, sn).astype(BF16)

    for jj, kind in enumerate(kinds):
        if kind == "g" and kinds.index("g") != jj:
            continue

        if kind == "q":
            pl.when(j == jj)(functools.partial(qk_epilogue, outs["q"], qw_ref))
        elif kind == "k":
            pl.when(j == jj)(functools.partial(qk_epilogue, outs["k"], kw_ref))
        elif kind == "g":
            @pl.when(j >= jj)
            def _():
                outs["g"][0] = jax.nn.sigmoid(project()).astype(BF16)
        else:
            def plain(o_ref):
                o_ref[0] = project().astype(BF16)
            pl.when(j == jj)(functools.partial(plain, outs[kind]))


def _input_projection(x, norm_w, shift, scale, w_in_bf, col_block0, kinds, qw128, kw128,
                      cos, sin_prev, sin_next, tm, tn=1024):
    B, L, D = x.shape
    out_names = tuple(dict.fromkeys(kinds))
    first = {n: kinds.index(n) for n in out_names}
    count = {n: kinds.count(n) for n in out_names}

    def out_map(name):
        j0, nblk = first[name], count[name]
        return lambda b, i, j: (b, i, jnp.clip(j - j0, 0, nblk - 1))

    vec = lambda: pl.BlockSpec((1, 1, D), lambda b, i, j: (b, 0, 0))
    tab = lambda: pl.BlockSpec((tm, V7X_LANES), lambda b, i, j: (i, 0))
    in_specs = [
        pl.BlockSpec((1, tm, D), lambda b, i, j: (b, i, 0)),
        pl.BlockSpec((1, D), lambda b, i, j: (0, 0)),
        vec(), vec(),
        pl.BlockSpec((D, tn), lambda b, i, j: (0, col_block0 + j)),
        pl.BlockSpec((1, V7X_LANES), lambda b, i, j: (0, 0)),
        pl.BlockSpec((1, V7X_LANES), lambda b, i, j: (0, 0)),
        tab(), tab(), tab(),
    ]
    out_specs = [pl.BlockSpec((1, tm, tn), out_map(n)) for n in out_names]
    out_shape = [jax.ShapeDtypeStruct((B, L, tn * count[n]), BF16) for n in out_names]
    blocks = (2 * (_nbytes((tm, D), F32) + _nbytes((D, tn), BF16) + 3 * _nbytes((tm, V7X_LANES), F32)
                   + len(out_names) * _nbytes((tm, tn), BF16))
              + _nbytes((tm, D), BF16) + 2 * _nbytes((tm, tn), F32))
    outs = pl.pallas_call(
        functools.partial(_inproj_kernel, kinds=tuple(kinds)),
        grid=(B, L // tm, len(kinds)),
        in_specs=in_specs,
        out_specs=out_specs,
        out_shape=out_shape,
        scratch_shapes=[pltpu.VMEM((tm, D), BF16)],
        compiler_params=pltpu.CompilerParams(
            dimension_semantics=("parallel", "parallel", "arbitrary"),
            vmem_limit_bytes=_vmem_limit(blocks)),
        name="input_projection",
    )(x, norm_w, shift, scale, w_in_bf, qw128, kw128, cos, sin_prev, sin_next)
    return dict(zip(out_names, outs))


def _attn_kernel(fast_ref, lq1_ref, lk1_ref, lq2_ref, lk2_ref, qt_ref, k_ref, vt_ref, kc_ref, vct_ref,
                 sw_ref, o_ref, rhs_ref, acc_ref, l_ref, *, tq, tk, lam_init):
    qt = qt_ref[0]
    row = lax.broadcasted_iota(jnp.int32, qt.shape, 0)
    zero = jnp.zeros_like(qt)
    rhs_ref[:, :tq] = jnp.where(row < HEAD_DIM, qt, zero)
    rhs_ref[:, tq:] = jnp.where(row >= HEAD_DIM, qt, zero)

    def scores(kt):
        return jnp.dot(kt, rhs_ref[...], preferred_element_type=F32)

    n_tiles = k_ref.shape[1] // tk

    @pl.when(fast_ref[0] == 1)
    def _():
        tiles = [(k_ref.at[0, pl.ds(t * tk, tk), :], vt_ref.at[0, :, pl.ds(t * tk, tk)])
                 for t in range(n_tiles)] + [(kc_ref.at[0], vct_ref.at[0])]
        acc = lsum = None
        for kt_ref, vtt_ref in tiles:
            p = jnp.exp2(scores(kt_ref[...]))
            part = jnp.sum(p.reshape(p.shape[0] // 8, 8, p.shape[1]), axis=0)
            pv = jnp.dot(vtt_ref[...], p.astype(BF16), preferred_element_type=F32)
            acc = pv if acc is None else acc + pv
            lsum = part if lsum is None else lsum + part
        acc_ref[...] = acc
        l_ref[...] = jnp.sum(lsum, axis=0, keepdims=True)

    @pl.when(fast_ref[0] != 1)
    def _():
        def update(carry, kt, vtt):
            m_prev, l_prev = carry
            s = scores(kt)
            m_new = jnp.maximum(m_prev, jnp.max(s, axis=0, keepdims=True))
            alpha = jnp.exp2(m_prev - m_new)
            p = jnp.exp2(s - m_new)
            acc_ref[...] = alpha * acc_ref[...] + jnp.dot(vtt, p.astype(BF16), preferred_element_type=F32)
            return m_new, alpha * l_prev + jnp.sum(p, axis=0, keepdims=True)

        def body(t, carry):
            off = pl.multiple_of(t * tk, tk)
            return update(carry, k_ref[0, pl.ds(off, tk), :], vt_ref[0, :, pl.ds(off, tk)])

        acc_ref[...] = jnp.zeros_like(acc_ref)
        init = (jnp.full((1, 2 * tq), -jnp.inf, F32), jnp.zeros((1, 2 * tq), F32))
        carry = lax.fori_loop(0, n_tiles, body, init)
        _, l_ref[...] = update(carry, kc_ref[0], vct_ref[0])

    lam = (jnp.exp(jnp.sum(lq1_ref[...] * lk1_ref[...])) - jnp.exp(jnp.sum(lq2_ref[...] * lk2_ref[...]))
           + lam_init)
    o = acc_ref[...] / l_ref[...]
    od = o[:, :tq] - lam * o[:, tq:]
    y = od * lax.rsqrt(jnp.mean(od * od, axis=0, keepdims=True) + EPS) * sw_ref[...]
    o_ref[0] = (y * (1.0 - lam_init)).T.astype(BF16)


def _diff_attention(fast, qt, k, vt, k_c, vt_c, lams, subln_w, lam_init, tq, tk):
    B, L, W = k.shape
    Lc = k_c.shape[1]
    H = W // VALUE_DIM
    assert L % tk == 0
    lam_spec = pl.BlockSpec((1, HEAD_DIM), lambda b, h, i: (0, 0))
    blocks = (2 * (2 * _nbytes((tq, VALUE_DIM), BF16) + 2 * _nbytes((L, VALUE_DIM), BF16)
                   + 2 * _nbytes((Lc, VALUE_DIM), BF16))
              + 3 * _nbytes((2 * tq, VALUE_DIM), F32) + 4 * _nbytes((2 * tq, tk), F32))
    return pl.pallas_call(
        functools.partial(_attn_kernel, tq=tq, tk=tk, lam_init=lam_init),
        grid=(B, H, L // tq),
        in_specs=[
            pl.BlockSpec(memory_space=pltpu.SMEM),
            lam_spec, lam_spec, lam_spec, lam_spec,
            pl.BlockSpec((1, VALUE_DIM, tq), lambda b, h, i: (b, h, i)),
            pl.BlockSpec((1, L, VALUE_DIM), lambda b, h, i: (b, 0, h)),
            pl.BlockSpec((1, VALUE_DIM, L), lambda b, h, i: (b, h, 0)),
            pl.BlockSpec((1, Lc, VALUE_DIM), lambda b, h, i: (b, 0, h)),
            pl.BlockSpec((1, VALUE_DIM, Lc), lambda b, h, i: (b, h, 0)),
            pl.BlockSpec((VALUE_DIM, 1), lambda b, h, i: (0, 0)),
        ],
        out_specs=pl.BlockSpec((1, tq, VALUE_DIM), lambda b, h, i: (b, i, h)),
        out_shape=jax.ShapeDtypeStruct((B, L, W), BF16),
        scratch_shapes=[pltpu.VMEM((VALUE_DIM, 2 * tq), BF16),
                        pltpu.VMEM((VALUE_DIM, 2 * tq), F32), pltpu.VMEM((1, 2 * tq), F32)],
        compiler_params=pltpu.CompilerParams(
            dimension_semantics=("parallel", "parallel", "parallel"),
            vmem_limit_bytes=_vmem_limit(blocks)),
        name="diff_attention",
    )(fast, *lams, qt, k, vt, k_c, vt_c, subln_w)


def _mixer_kernel(heads_ref, u_ref, uprev_ref, unext_ref, g_ref, x_ref, gate_ref,
                  wa_ref, wb_ref, pw_ref, ps_ref, wo_ref, o_ref, ubuf_ref, y_ref, *, tm, seq_len):
    i = pl.program_id(1)
    nt = pl.num_programs(1)
    D = x_ref.shape[2]
    group = u_ref.shape[2] // len(POOL_WINDOWS)

    ubuf_ref[pl.ds(POOL_HALO, tm), :] = u_ref[0].astype(F32)
    ubuf_ref[pl.ds(0, POOL_HALO), :] = jnp.where(i > 0, uprev_ref[0].astype(F32), 0.0)
    ubuf_ref[pl.ds(POOL_HALO + tm, POOL_HALO), :] = jnp.where(i < nt - 1, unext_ref[0].astype(F32), 0.0)

    t = i * tm + lax.broadcasted_iota(jnp.int32, (tm, 1), 0)
    for gi, w in enumerate(POOL_WINDOWS):
        cols = slice(gi * group, (gi + 1) * group)
        total = None
        for off in range(-(w // 2), w - w // 2):
            part = ubuf_ref[pl.ds(POOL_HALO + off, tm), cols]
            total = part if total is None else total + part
        lo = jnp.maximum(t - w // 2, 0)
        hi = jnp.minimum(t + (w - w // 2), seq_len)
        d = total / (hi - lo).astype(F32) - ubuf_ref[pl.ds(POOL_HALO, tm), cols]
        yg = jnp.dot(d.astype(BF16), pw_ref[gi], preferred_element_type=F32)
        y_ref[:, cols] = (yg * ps_ref[:, cols]).astype(BF16)

    y_a = jnp.dot(heads_ref[0], wa_ref[...], preferred_element_type=F32)
    y_b = jnp.dot(y_ref[...], wb_ref[...], preferred_element_type=F32)
    mixed = g_ref[0, :, :D].astype(F32) * y_a + g_ref[0, :, D:].astype(F32) * y_b
    mix = jnp.dot(mixed.astype(BF16), wo_ref[...], preferred_element_type=F32)
    o_ref[0] = x_ref[0] + gate_ref[0] * mix


def _token_mixer(heads, u, g, x, gate, w_a_up, w_b_up, pool_w, pool_scale, w_o, tm):
    B, L, D = x.shape
    AW, PW = heads.shape[2], u.shape[2]
    hb = tm // POOL_HALO
    n_halo = L // POOL_HALO
    const = lambda shape: pl.BlockSpec(shape, lambda b, i: (0,) * len(shape),
                                       pipeline_mode=pl.Buffered(1))
    weights = (_nbytes((AW, D), BF16) + _nbytes((PW, D), BF16) + _nbytes(pool_w.shape, BF16)
               + _nbytes((D, D), BF16))
    blocks = (weights + 2 * (_nbytes((tm, AW), BF16) + _nbytes((tm, PW), BF16)
                             + _nbytes((tm, 2 * D), BF16) + 2 * _nbytes((tm, D), F32))
              + _nbytes((tm + 2 * POOL_HALO, PW), F32) + _nbytes((tm, PW), BF16)
              + 4 * _nbytes((tm, D), F32))
    return pl.pallas_call(
        functools.partial(_mixer_kernel, tm=tm, seq_len=L),
        grid=(B, L // tm),
        in_specs=[
            pl.BlockSpec((1, tm, AW), lambda b, i: (b, i, 0)),
            pl.BlockSpec((1, tm, PW), lambda b, i: (b, i, 0)),
            pl.BlockSpec((1, POOL_HALO, PW), lambda b, i: (b, jnp.maximum(i * hb - 1, 0), 0)),
            pl.BlockSpec((1, POOL_HALO, PW), lambda b, i: (b, jnp.minimum((i + 1) * hb, n_halo - 1), 0)),
            pl.BlockSpec((1, tm, 2 * D), lambda b, i: (b, i, 0)),
            pl.BlockSpec((1, tm, D), lambda b, i: (b, i, 0)),
            pl.BlockSpec((1, 1, D), lambda b, i: (b, 0, 0)),
            const((AW, D)), const((PW, D)), const(pool_w.shape), const((1, PW)), const((D, D)),
        ],
        out_specs=pl.BlockSpec((1, tm, D), lambda b, i: (b, i, 0)),
        out_shape=jax.ShapeDtypeStruct((B, L, D), F32),
        scratch_shapes=[pltpu.VMEM((tm + 2 * POOL_HALO, PW), F32), pltpu.VMEM((tm, PW), BF16)],
        compiler_params=pltpu.CompilerParams(
            dimension_semantics=("parallel", "parallel"),
            vmem_limit_bytes=_vmem_limit(blocks)),
        name="token_mixer",
    )(heads, u, u, u, g, x, gate, w_a_up, w_b_up, pool_w, pool_scale, w_o)


def _mlp_kernel(x_ref, nw_ref, shift_ref, scale_ref, gate_ref, w1_ref, w2_ref, o_ref, h_ref):
    j = pl.program_id(2)

    @pl.when(j == 0)
    def _():
        h = _modulated_norm(x_ref[0], nw_ref[...], shift_ref[0], scale_ref[0])
        h_ref[...] = h.astype(BF16)

        o_ref[...] = jnp.zeros_like(o_ref)

    a = jnp.maximum(jnp.dot(h_ref[...], w1_ref[...], preferred_element_type=F32), 0.0)
    a2 = (a * a).astype(BF16)
    tf = w1_ref.shape[1]
    for n in range(0, o_ref.shape[2], tf):
        o_ref[0, :, n:n + tf] += jnp.dot(a2, w2_ref[:, n:n + tf], preferred_element_type=F32)

    @pl.when(j == pl.num_programs(2) - 1)
    def _():
        o_ref[0] = x_ref[0] + gate_ref[0] * o_ref[0]


def _mlp(x, norm_w, shift, scale, gate, w1, w2, tm, tf):
    B, L, D = x.shape
    F = w1.shape[1]
    vec = lambda: pl.BlockSpec((1, 1, D), lambda b, i, j: (b, 0, 0))
    blocks = (2 * (2 * _nbytes((tm, D), F32) + _nbytes((D, tf), BF16) + _nbytes((tf, D), BF16))
              + _nbytes((tm, D), BF16) + 2 * _nbytes((tm, tf), F32) + _nbytes((tm, D), F32))
    return pl.pallas_call(
        _mlp_kernel,
        grid=(B, L // tm, F // tf),
        in_specs=[
            pl.BlockSpec((1, tm, D), lambda b, i, j: (b, i, 0)),
            pl.BlockSpec((1, D), lambda b, i, j: (0, 0)),
            vec(), vec(), vec(),
            pl.BlockSpec((D, tf), lambda b, i, j: (0, j)),
            pl.BlockSpec((tf, D), lambda b, i, j: (j, 0)),
        ],
        out_specs=pl.BlockSpec((1, tm, D), lambda b, i, j: (b, i, 0)),
        out_shape=jax.ShapeDtypeStruct((B, L, D), F32),
        scratch_shapes=[pltpu.VMEM((tm, D), BF16)],
        compiler_params=pltpu.CompilerParams(
            dimension_semantics=("parallel", "parallel", "arbitrary"),
            vmem_limit_bytes=_vmem_limit(blocks)),
        name="sq_relu_mlp",
    )(x, norm_w, shift, scale, gate, w1, w2)


def _rope_tables(seq_len):
    t = jnp.arange(seq_len)
    row, col = t // GRID_W, t % GRID_W
    half = HEAD_DIM // 2
    quarter = half // 2
    inv_freq = ROPE_THETA ** (-jnp.arange(0, half, 2, dtype=F32) / half)
    lane = jnp.arange(V7X_LANES)
    e = lane % HEAD_DIM
    pos = jnp.where((e // half)[None, :] == 0, row[:, None], col[:, None]).astype(F32)
    ang = pos * inv_freq[e % quarter][None, :]
    cos, sin = jnp.cos(ang), jnp.sin(ang)
    second = ((e % half) >= quarter)[None, :]
    return cos, jnp.where(second, sin, 0.0), jnp.where(second, 0.0, -sin)


def _identity_tables(seq_len):
    ones = jnp.ones((seq_len, V7X_LANES), F32)
    return ones, jnp.zeros_like(ones), jnp.zeros_like(ones)


def _pick(n, pref):
    t = min(pref, n)
    while n % t:
        t //= 2
    return t


def kernel(x, c, ctx, c_ctx, w_mod, b_mod, norm_attn_w, w_in, q_norm_w, k_norm_w, lambda_q1, lambda_k1,
           lambda_q2, lambda_k2, subln_w, pool_w, pool_scale, w_a_up, w_b_up, w_o, norm_mlp_w, w_ff1, w_ff2):
    B, L, D = x.shape
    Lc = ctx.shape[1]
    depth = w_mod.shape[0]
    qk_w = 2 * ATTN_HEADS * HEAD_DIM
    tn = qk_w
    assert depth == 1 and w_in.shape[2] == 4 * tn + 2 * D and L % GRID_W == 0

    for l in range(depth):
        lam_init = 0.8 - 0.6 * math.exp(-0.3 * l)

        rows = 8
        cvec = jnp.zeros((rows, D), F32).at[:B].set(c).at[B].set(c_ctx)
        mod = _modulation(cvec, w_mod[l], b_mod[l][None, :])
        sa, ca, ga, sm, cm, gm = [mod[:B, None, k * D:(k + 1) * D] for k in range(6)]
        sa_c, ca_c = [jnp.broadcast_to(mod[B, k * D:(k + 1) * D], (B, 1, D)) for k in range(2)]

        w_in_bf = w_in[l].astype(BF16)
        qw128 = (jnp.tile(q_norm_w[l], 2) * (HEAD_DIM ** -0.5 * math.log2(math.e)))[None, :]
        kw128 = jnp.tile(k_norm_w[l], 2)[None, :]
        nw = norm_attn_w[l][None, :]

        ctx_p = _input_projection(ctx, nw, sa_c, ca_c, w_in_bf, 1, ("k", "v"), qw128, kw128,
                                  *_identity_tables(Lc), tm=_pick(Lc, 256), tn=tn)
        p = _input_projection(x, nw, sa, ca, w_in_bf, 0, ("q", "k", "v", "u", "g", "g", "g", "g"),
                              qw128, kw128, *_rope_tables(L), tm=_pick(L, 512), tn=tn)

        lams = [v[l][None, :] for v in (lambda_q1, lambda_k1, lambda_q2, lambda_k2)]
        swap = lambda a: jnp.swapaxes(a, 1, 2)
        score_bound = (HEAD_DIM * jnp.max(jnp.abs(qw128)) * jnp.max(jnp.abs(kw128))) * BF16_SLACK
        fast = (score_bound <= SCORE_BOUND).astype(jnp.int32).reshape(1)
        heads = _diff_attention(fast, swap(p["q"]), p["k"], swap(p["v"]), ctx_p["k"], swap(ctx_p["v"]), lams,
                                subln_w[l][:, None], lam_init, tq=_pick(L, 256), tk=_pick(L, 1024))

        x = _token_mixer(heads, p["u"], p["g"], x, ga, w_a_up[l].astype(BF16), w_b_up[l].astype(BF16),
                         pool_w[l].astype(BF16), pool_scale[l][None, :], w_o[l].astype(BF16),
                         tm=_pick(L, 256))
        x = _mlp(x, norm_mlp_w[l][None, :], sm, cm, gm, w_ff1[l].astype(BF16), w_ff2[l].astype(BF16),
                 tm=_pick(L, 1024), tf=512)
    return x
```

```python
import functools
import math

import jax
import jax.numpy as jnp
from jax import lax
from jax.experimental import pallas as pl
from jax.experimental.pallas import tpu as pltpu

F32 = jnp.float32
BF16 = jnp.bfloat16

ATTN_HEADS = 8
HEAD_DIM = 64
VALUE_DIM = 2 * HEAD_DIM
SCORE_BOUND = 64.0
BF16_SLACK = 1.0 + 2.0 ** -6
GRID_W = 64
ROPE_THETA = 10000.0
POOL_WINDOWS = (2, 4, 8, 16)
POOL_HALO = 16
EPS = 1e-6

V7X_LANES = 128
V7X_VMEM_BYTES = 64 * 1024 * 1024
V7X_VMEM_CAP = V7X_VMEM_BYTES - 6 * 1024 * 1024


def _vmem_limit(block_bytes):
    return int(min(V7X_VMEM_CAP, block_bytes * 1.25 + 8 * 1024 * 1024))


def _nbytes(shape, dtype):
    return math.prod(shape) * jnp.dtype(dtype).itemsize


def _mod_kernel(c_ref, w_ref, b_ref, o_ref):
    cv = c_ref[...]
    act = cv * jax.nn.sigmoid(cv)
    o_ref[...] = jnp.dot(act, w_ref[...], preferred_element_type=F32) + b_ref[...]


def _modulation(cvec, w_mod, b_mod, tn=1024):
    rows, d = cvec.shape
    n = w_mod.shape[1]
    blocks = 2 * (_nbytes((d, tn), F32) + _nbytes((rows, tn), F32) * 2) + _nbytes((rows, d), F32)
    return pl.pallas_call(
        _mod_kernel,
        grid=(n // tn,),
        in_specs=[
            pl.BlockSpec((rows, d), lambda j: (0, 0)),
            pl.BlockSpec((d, tn), lambda j: (0, j)),
            pl.BlockSpec((1, tn), lambda j: (0, j)),
        ],
        out_specs=pl.BlockSpec((rows, tn), lambda j: (0, j)),
        out_shape=jax.ShapeDtypeStruct((rows, n), F32),
        compiler_params=pltpu.CompilerParams(
            dimension_semantics=("parallel",), vmem_limit_bytes=_vmem_limit(blocks)),
        name="modulation",
    )(cvec, w_mod, b_mod)


def _modulated_norm(x, norm_w, shift, scale):
    y = x * lax.rsqrt(jnp.mean(x * x, axis=-1, keepdims=True) + EPS)
    return (y * norm_w) * (1.0 + scale) + shift


def _inproj_kernel(x_ref, nw_ref, shift_ref, scale_ref, w_ref, seg_ref, tab_ref, *rest, kinds):
    out_names = tuple(dict.fromkeys(kinds))
    outs = dict(zip(out_names, rest[:len(out_names)]))
    h_ref = rest[len(out_names)]
    j = pl.program_id(2)
    tn = w_ref.shape[1]

    @pl.when(j == 0)
    def _():
        h = _modulated_norm(x_ref[0], nw_ref[...], shift_ref[0], scale_ref[0])
        h_ref[...] = h.astype(BF16)

    def project(cols=slice(None)):
        return jnp.dot(h_ref[...], w_ref[:, cols], preferred_element_type=F32)

    def qk_epilogue(o_ref, tab0):
        quarter = HEAD_DIM // 4
        half = h_ref.shape[0] // 2
        for r0 in (0, half):
            rows = pl.ds(r0, half)
            cosw, sinw_prev, sinw_next = (tab_ref[tab0 + i, rows, :] for i in range(3))
            acc = jnp.dot(h_ref[rows, :], w_ref[...], preferred_element_type=F32)
            ms = jnp.dot((acc * acc).astype(BF16), seg_ref[...], preferred_element_type=F32)
            y = acc * lax.rsqrt(ms + EPS)
            for c in range(0, tn, V7X_LANES):
                yc = y[:, c:c + V7X_LANES]
                out = (yc * cosw + pltpu.roll(yc, quarter, 1) * sinw_prev
                       + pltpu.roll(yc, V7X_LANES - quarter, 1) * sinw_next)
                o_ref[0, rows, c:c + V7X_LANES] = out.astype(BF16)

    def in_range(name):
        j0, n = kinds.index(name), kinds.count(name)
        return (j >= j0) & (j < j0 + n)

    for name in out_names:
        if name == "q":
            pl.when(in_range(name))(functools.partial(qk_epilogue, outs[name], 0))
        elif name == "k":
            pl.when(in_range(name))(functools.partial(qk_epilogue, outs[name], 3))
        elif name == "g":
            @pl.when(in_range(name))
            def _():
                outs["g"][0] = jax.nn.sigmoid(project()).astype(BF16)
        else:
            def plain(o_ref):
                o_ref[0] = project().astype(BF16)
            pl.when(in_range(name))(functools.partial(plain, outs[name]))


def _input_projection(x, norm_w, shift, scale, w_in_bf, col_block0, kinds, seg_mean, tables, tm, tn):
    B, L, D = x.shape
    out_names = tuple(dict.fromkeys(kinds))
    first = {n: kinds.index(n) for n in out_names}
    count = {n: kinds.count(n) for n in out_names}

    def out_map(name):
        j0, nblk = first[name], count[name]
        return lambda i, b, j: (b, i, jnp.clip(j - j0, 0, nblk - 1))

    vec = lambda: pl.BlockSpec((1, 1, D), lambda i, b, j: (b, 0, 0))
    in_specs = [
        pl.BlockSpec((1, tm, D), lambda i, b, j: (b, i, 0)),
        pl.BlockSpec((1, D), lambda i, b, j: (0, 0)),
        vec(), vec(),
        pl.BlockSpec((D, tn), lambda i, b, j: (0, col_block0 + j)),
        pl.BlockSpec(seg_mean.shape, lambda i, b, j: (0, 0)),
        pl.BlockSpec((tables.shape[0], tm, V7X_LANES), lambda i, b, j: (0, i, 0)),
    ]
    out_specs = [pl.BlockSpec((1, tm, tn), out_map(n)) for n in out_names]
    out_shape = [jax.ShapeDtypeStruct((B, L, tn * count[n]), BF16) for n in out_names]
    blocks = (2 * (_nbytes((tm, D), F32) + _nbytes((D, tn), BF16)
                   + _nbytes((tables.shape[0], tm, V7X_LANES), F32)
                   + len(out_names) * _nbytes((tm, tn), BF16))
              + _nbytes((tm, D), BF16) + 3 * _nbytes((tm, tn), F32))
    outs = pl.pallas_call(
        functools.partial(_inproj_kernel, kinds=tuple(kinds)),
        grid=(L // tm, B, len(kinds)),
        in_specs=in_specs,
        out_specs=out_specs,
        out_shape=out_shape,
        scratch_shapes=[pltpu.VMEM((tm, D), BF16)],
        compiler_params=pltpu.CompilerParams(
            dimension_semantics=("parallel", "parallel", "arbitrary"),
            vmem_limit_bytes=_vmem_limit(blocks)),
        name="input_projection",
    )(x, norm_w, shift, scale, w_in_bf, seg_mean, tables)
    return dict(zip(out_names, outs))


def _attn_kernel(fast_ref, lq1_ref, lk1_ref, lq2_ref, lk2_ref, qt_ref, k_ref, vt_ref, kc_ref, vct_ref,
                 sw_ref, o_ref, rhs_ref, acc_ref, l_ref, *, tq, tk, lam_init):
    qt = qt_ref[0]
    row = lax.broadcasted_iota(jnp.int32, qt.shape, 0)
    zero = jnp.zeros_like(qt)
    rhs_ref[:, :tq] = jnp.where(row < HEAD_DIM, qt, zero)
    rhs_ref[:, tq:] = jnp.where(row >= HEAD_DIM, qt, zero)

    def scores(kt):
        return jnp.dot(kt, rhs_ref[...], preferred_element_type=F32)

    n_tiles = k_ref.shape[1] // tk

    @pl.when(fast_ref[0] == 1)
    def _():
        tiles = [(k_ref.at[0, pl.ds(t * tk, tk), :], vt_ref.at[0, :, pl.ds(t * tk, tk)])
                 for t in range(n_tiles)] + [(kc_ref.at[0], vct_ref.at[0])]
        acc = lsum = None
        for kt_ref, vtt_ref in tiles:
            p = jnp.exp2(scores(kt_ref[...]))
            part = jnp.sum(p.reshape(p.shape[0] // 8, 8, p.shape[1]), axis=0)
            pv = jnp.dot(vtt_ref[...], p.astype(BF16), preferred_element_type=F32)
            acc = pv if acc is None else acc + pv
            lsum = part if lsum is None else lsum + part
        acc_ref[...] = acc
        l_ref[...] = jnp.sum(lsum, axis=0, keepdims=True)

    @pl.when(fast_ref[0] != 1)
    def _():
        def update(carry, kt, vtt):
            m_prev, l_prev = carry
            s = scores(kt)
            m_new = jnp.maximum(m_prev, jnp.max(s, axis=0, keepdims=True))
            alpha = jnp.exp2(m_prev - m_new)
            p = jnp.exp2(s - m_new)
            acc_ref[...] = alpha * acc_ref[...] + jnp.dot(vtt, p.astype(BF16), preferred_element_type=F32)
            return m_new, alpha * l_prev + jnp.sum(p, axis=0, keepdims=True)

        def body(t, carry):
            off = pl.multiple_of(t * tk, tk)
            return update(carry, k_ref[0, pl.ds(off, tk), :], vt_ref[0, :, pl.ds(off, tk)])

        acc_ref[...] = jnp.zeros_like(acc_ref)
        init = (jnp.full((1, 2 * tq), -jnp.inf, F32), jnp.zeros((1, 2 * tq), F32))
        carry = lax.fori_loop(0, n_tiles, body, init)
        _, l_ref[...] = update(carry, kc_ref[0], vct_ref[0])

    lam = (jnp.exp(jnp.sum(lq1_ref[...] * lk1_ref[...])) - jnp.exp(jnp.sum(lq2_ref[...] * lk2_ref[...]))
           + lam_init)
    o = acc_ref[...] / l_ref[...]
    od = o[:, :tq] - lam * o[:, tq:]
    y = od * lax.rsqrt(jnp.mean(od * od, axis=0, keepdims=True) + EPS) * sw_ref[...]
    o_ref[0] = (y * (1.0 - lam_init)).T.astype(BF16)


def _diff_attention(fast, qt, k, vt, k_c, vt_c, lams, subln_w, lam_init, tq, tk):
    B, L, W = k.shape
    Lc = k_c.shape[1]
    H = W // VALUE_DIM
    assert L % tk == 0
    lam_spec = pl.BlockSpec((1, HEAD_DIM), lambda b, h, i: (0, 0))
    blocks = (2 * (2 * _nbytes((tq, VALUE_DIM), BF16) + 2 * _nbytes((L, VALUE_DIM), BF16)
                   + 2 * _nbytes((Lc, VALUE_DIM), BF16))
              + 3 * _nbytes((2 * tq, VALUE_DIM), F32) + 4 * _nbytes((2 * tq, tk), F32))
    return pl.pallas_call(
        functools.partial(_attn_kernel, tq=tq, tk=tk, lam_init=lam_init),
        grid=(B, H, L // tq),
        in_specs=[
            pl.BlockSpec(memory_space=pltpu.SMEM),
            lam_spec, lam_spec, lam_spec, lam_spec,
            pl.BlockSpec((1, VALUE_DIM, tq), lambda b, h, i: (b, h, i)),
            pl.BlockSpec((1, L, VALUE_DIM), lambda b, h, i: (b, 0, h)),
            pl.BlockSpec((1, VALUE_DIM, L), lambda b, h, i: (b, h, 0)),
            pl.BlockSpec((1, Lc, VALUE_DIM), lambda b, h, i: (b, 0, h)),
            pl.BlockSpec((1, VALUE_DIM, Lc), lambda b, h, i: (b, h, 0)),
            pl.BlockSpec((VALUE_DIM, 1), lambda b, h, i: (0, 0)),
        ],
        out_specs=pl.BlockSpec((1, tq, VALUE_DIM), lambda b, h, i: (b, i, h)),
        out_shape=jax.ShapeDtypeStruct((B, L, W), BF16),
        scratch_shapes=[pltpu.VMEM((VALUE_DIM, 2 * tq), BF16),
                        pltpu.VMEM((VALUE_DIM, 2 * tq), F32), pltpu.VMEM((1, 2 * tq), F32)],
        compiler_params=pltpu.CompilerParams(
            dimension_semantics=("parallel", "parallel", "parallel"),
            vmem_limit_bytes=_vmem_limit(blocks)),
        name="diff_attention",
    )(fast, *lams, qt, k, vt, k_c, vt_c, subln_w)


def _mixer_kernel(heads_ref, u_ref, uprev_ref, unext_ref, g_ref, x_ref, gate_ref,
                  wa_ref, wb_ref, pw_ref, ps_ref, wo_ref, o_ref, ubuf_ref, y_ref, *, tm, seq_len):
    i = pl.program_id(1)
    nt = pl.num_programs(1)
    D = x_ref.shape[2]
    group = u_ref.shape[2] // len(POOL_WINDOWS)

    ubuf_ref[pl.ds(POOL_HALO, tm), :] = u_ref[0].astype(F32)
    ubuf_ref[pl.ds(0, POOL_HALO), :] = jnp.where(i > 0, uprev_ref[0].astype(F32), 0.0)
    ubuf_ref[pl.ds(POOL_HALO + tm, POOL_HALO), :] = jnp.where(i < nt - 1, unext_ref[0].astype(F32), 0.0)

    t = i * tm + lax.broadcasted_iota(jnp.int32, (tm, 1), 0)
    for gi, w in enumerate(POOL_WINDOWS):
        cols = slice(gi * group, (gi + 1) * group)
        total = None
        for off in range(-(w // 2), w - w // 2):
            part = ubuf_ref[pl.ds(POOL_HALO + off, tm), cols]
            total = part if total is None else total + part
        lo = jnp.maximum(t - w // 2, 0)
        hi = jnp.minimum(t + (w - w // 2), seq_len)
        d = total / (hi - lo).astype(F32) - ubuf_ref[pl.ds(POOL_HALO, tm), cols]
        yg = jnp.dot(d.astype(BF16), pw_ref[gi], preferred_element_type=F32)
        y_ref[:, cols] = (yg * ps_ref[:, cols]).astype(BF16)

    y_a = jnp.dot(heads_ref[0], wa_ref[...], preferred_element_type=F32)
    y_b = jnp.dot(y_ref[...], wb_ref[...], preferred_element_type=F32)
    mixed = g_ref[0, :, :D].astype(F32) * y_a + g_ref[0, :, D:].astype(F32) * y_b
    mix = jnp.dot(mixed.astype(BF16), wo_ref[...], preferred_element_type=F32)
    o_ref[0] = x_ref[0] + gate_ref[0] * mix


def _token_mixer(heads, u, g, x, gate, w_a_up, w_b_up, pool_w, pool_scale, w_o, tm):
    B, L, D = x.shape
    AW, PW = heads.shape[2], u.shape[2]
    hb = tm // POOL_HALO
    n_halo = L // POOL_HALO
    const = lambda shape: pl.BlockSpec(shape, lambda b, i: (0,) * len(shape),
                                       pipeline_mode=pl.Buffered(1))
    weights = (_nbytes((AW, D), BF16) + _nbytes((PW, D), BF16) + _nbytes(pool_w.shape, BF16)
               + _nbytes((D, D), BF16))
    blocks = (weights + 2 * (_nbytes((tm, AW), BF16) + _nbytes((tm, PW), BF16)
                             + _nbytes((tm, 2 * D), BF16) + 2 * _nbytes((tm, D), F32))
              + _nbytes((tm + 2 * POOL_HALO, PW), F32) + _nbytes((tm, PW), BF16)
              + 4 * _nbytes((tm, D), F32))
    return pl.pallas_call(
        functools.partial(_mixer_kernel, tm=tm, seq_len=L),
        grid=(B, L // tm),
        in_specs=[
            pl.BlockSpec((1, tm, AW), lambda b, i: (b, i, 0)),
            pl.BlockSpec((1, tm, PW), lambda b, i: (b, i, 0)),
            pl.BlockSpec((1, POOL_HALO, PW), lambda b, i: (b, jnp.maximum(i * hb - 1, 0), 0)),
            pl.BlockSpec((1, POOL_HALO, PW), lambda b, i: (b, jnp.minimum((i + 1) * hb, n_halo - 1), 0)),
            pl.BlockSpec((1, tm, 2 * D), lambda b, i: (b, i, 0)),
            pl.BlockSpec((1, tm, D), lambda b, i: (b, i, 0)),
            pl.BlockSpec((1, 1, D), lambda b, i: (b, 0, 0)),
            const((AW, D)), const((PW, D)), const(pool_w.shape), const((1, PW)), const((D, D)),
        ],
        out_specs=pl.BlockSpec((1, tm, D), lambda b, i: (b, i, 0)),
        out_shape=jax.ShapeDtypeStruct((B, L, D), F32),
        scratch_shapes=[pltpu.VMEM((tm + 2 * POOL_HALO, PW), F32), pltpu.VMEM((tm, PW), BF16)],
        compiler_params=pltpu.CompilerParams(
            dimension_semantics=("parallel", "parallel"),
            vmem_limit_bytes=_vmem_limit(blocks)),
        name="token_mixer",
    )(heads, u, u, u, g, x, gate, w_a_up, w_b_up, pool_w, pool_scale, w_o)


def _mlp_kernel(x_ref, nw_ref, shift_ref, scale_ref, gate_ref, w1_ref, w2_ref, o_ref, h_ref):
    j = pl.program_id(2)

    @pl.when(j == 0)
    def _():
        h = _modulated_norm(x_ref[0], nw_ref[...], shift_ref[0], scale_ref[0])
        h_ref[...] = h.astype(BF16)
        o_ref[...] = jnp.zeros_like(o_ref)

    a = jnp.maximum(jnp.dot(h_ref[...], w1_ref[...], preferred_element_type=F32), 0.0)
    a2 = (a * a).astype(BF16)
    tf = w1_ref.shape[1]
    for n in range(0, o_ref.shape[2], tf):
        o_ref[0, :, n:n + tf] += jnp.dot(a2, w2_ref[:, n:n + tf], preferred_element_type=F32)

    @pl.when(j == pl.num_programs(2) - 1)
    def _():
        o_ref[0] = x_ref[0] + gate_ref[0] * o_ref[0]


def _mlp(x, norm_w, shift, scale, gate, w1, w2, tm, tf):
    B, L, D = x.shape
    F = w1.shape[1]
    vec = lambda: pl.BlockSpec((1, 1, D), lambda b, i, j: (b, 0, 0))
    blocks = (2 * (2 * _nbytes((tm, D), F32) + _nbytes((D, tf), BF16) + _nbytes((tf, D), BF16))
              + _nbytes((tm, D), BF16) + 2 * _nbytes((tm, tf), F32) + _nbytes((tm, D), F32))
    return pl.pallas_call(
        _mlp_kernel,
        grid=(B, L // tm, F // tf),
        in_specs=[
            pl.BlockSpec((1, tm, D), lambda b, i, j: (b, i, 0)),
            pl.BlockSpec((1, D), lambda b, i, j: (0, 0)),
            vec(), vec(), vec(),
            pl.BlockSpec((D, tf), lambda b, i, j: (0, j)),
            pl.BlockSpec((tf, D), lambda b, i, j: (j, 0)),
        ],
        out_specs=pl.BlockSpec((1, tm, D), lambda b, i, j: (b, i, 0)),
        out_shape=jax.ShapeDtypeStruct((B, L, D), F32),
        scratch_shapes=[pltpu.VMEM((tm, D), BF16)],
        compiler_params=pltpu.CompilerParams(
            dimension_semantics=("parallel", "parallel", "arbitrary"),
            vmem_limit_bytes=_vmem_limit(blocks)),
        name="sq_relu_mlp",
    )(x, norm_w, shift, scale, gate, w1, w2)


def _rope_tables(seq_len, qw128, kw128, rope):
    quarter = HEAD_DIM // 4
    if rope:
        t = jnp.arange(seq_len)
        row, col = t // GRID_W, t % GRID_W
        half = HEAD_DIM // 2
        inv_freq = ROPE_THETA ** (-jnp.arange(0, half, 2, dtype=F32) / half)
        lane = jnp.arange(V7X_LANES)
        e = lane % HEAD_DIM
        pos = jnp.where((e // half)[None, :] == 0, row[:, None], col[:, None]).astype(F32)
        ang = pos * inv_freq[e % quarter][None, :]
        cos, sin = jnp.cos(ang), jnp.sin(ang)
        second = ((e % half) >= quarter)[None, :]
        sin_prev, sin_next = jnp.where(second, sin, 0.0), jnp.where(second, 0.0, -sin)
    else:
        cos = jnp.ones((seq_len, V7X_LANES), F32)
        sin_prev = sin_next = jnp.zeros_like(cos)
    tabs = []
    for w in (qw128, kw128):
        tabs += [cos * w, sin_prev * jnp.roll(w, quarter, axis=1), sin_next * jnp.roll(w, -quarter, axis=1)]
    return jnp.stack(tabs)


def _pick(n, pref):
    t = min(pref, n)
    while n % t:
        t //= 2
    return t


def kernel(x, c, ctx, c_ctx, w_mod, b_mod, norm_attn_w, w_in, q_norm_w, k_norm_w, lambda_q1, lambda_k1,
           lambda_q2, lambda_k2, subln_w, pool_w, pool_scale, w_a_up, w_b_up, w_o, norm_mlp_w, w_ff1, w_ff2):
    B, L, D = x.shape
    Lc = ctx.shape[1]
    depth = w_mod.shape[0]
    qk_w = 2 * ATTN_HEADS * HEAD_DIM
    tn = qk_w // 2
    assert depth == 1 and w_in.shape[2] == 8 * tn + 2 * D and L % GRID_W == 0

    for l in range(depth):
        lam_init = 0.8 - 0.6 * math.exp(-0.3 * l)

        rows = 8
        cvec = jnp.zeros((rows, D), F32).at[:B].set(c).at[B].set(c_ctx)
        mod = _modulation(cvec, w_mod[l], b_mod[l][None, :])
        sa, ca, ga, sm, cm, gm = [mod[:B, None, k * D:(k + 1) * D] for k in range(6)]
        sa_c, ca_c = [jnp.broadcast_to(mod[B, k * D:(k + 1) * D], (B, 1, D)) for k in range(2)]

        w_in_bf = w_in[l].astype(BF16)
        qw128 = (jnp.tile(q_norm_w[l], 2) * (HEAD_DIM ** -0.5 * math.log2(math.e)))[None, :]
        kw128 = jnp.tile(k_norm_w[l], 2)[None, :]
        nw = norm_attn_w[l][None, :]

        seg = jnp.kron(jnp.eye(tn // HEAD_DIM, dtype=F32),
                       jnp.full((HEAD_DIM, HEAD_DIM), 1.0 / HEAD_DIM)).astype(BF16)
        half_blocks = lambda names: tuple(n for n in names for _ in range(2))
        ctx_p = _input_projection(ctx, nw, sa_c, ca_c, w_in_bf, 2, half_blocks(("k", "v")), seg,
                                  _rope_tables(Lc, qw128, kw128, rope=False), tm=_pick(Lc, 256), tn=tn)
        p = _input_projection(x, nw, sa, ca, w_in_bf, 0,
                              half_blocks(("q", "k", "v", "u", "g", "g", "g", "g")), seg,
                              _rope_tables(L, qw128, kw128, rope=True), tm=_pick(L, 1024), tn=tn)

        lams = [v[l][None, :] for v in (lambda_q1, lambda_k1, lambda_q2, lambda_k2)]
        swap = lambda a: jnp.swapaxes(a, 1, 2)
        score_bound = (HEAD_DIM * jnp.max(jnp.abs(qw128)) * jnp.max(jnp.abs(kw128))) * BF16_SLACK
        fast = (score_bound <= SCORE_BOUND).astype(jnp.int32).reshape(1)
        heads = _diff_attention(fast, swap(p["q"]), p["k"], swap(p["v"]), ctx_p["k"], swap(ctx_p["v"]), lams,
                                subln_w[l][:, None], lam_init, tq=_pick(L, 512), tk=_pick(L, 1024))

        x = _token_mixer(heads, p["u"], p["g"], x, ga, w_a_up[l].astype(BF16), w_b_up[l].astype(BF16),
                         pool_w[l].astype(BF16), pool_scale[l][None, :], w_o[l].astype(BF16),
                         tm=_pick(L, 256))
        x = _mlp(x, norm_mlp_w[l][None, :], sm, cm, gm, w_ff1[l].astype(BF16), w_ff2[l].astype(BF16),
                 tm=_pick(L, 1024), tf=512)
    return x
```

```python
import functools
import math

import jax
import jax.numpy as jnp
from jax import lax
from jax.experimental import pallas as pl
from jax.experimental.pallas import tpu as pltpu

F32 = jnp.float32
BF16 = jnp.bfloat16

ATTN_HEADS = 8
HEAD_DIM = 64
VALUE_DIM = 2 * HEAD_DIM
SCORE_BOUND = 64.0
BF16_SLACK = 1.0 + 2.0 ** -6
GRID_W = 64
ROPE_THETA = 10000.0
POOL_WINDOWS = (2, 4, 8, 16)
POOL_HALO = 16
EPS = 1e-6

V7X_LANES = 128
V7X_VMEM_BYTES = 64 * 1024 * 1024
V7X_VMEM_CAP = V7X_VMEM_BYTES - 6 * 1024 * 1024


def _vmem_limit(block_bytes):
    return int(min(V7X_VMEM_CAP, block_bytes * 1.25 + 8 * 1024 * 1024))


def _nbytes(shape, dtype):
    return math.prod(shape) * jnp.dtype(dtype).itemsize


def _mod_kernel(c_ref, w_ref, b_ref, o_ref):
    cv = c_ref[...]
    act = cv * jax.nn.sigmoid(cv)
    o_ref[...] = jnp.dot(act, w_ref[...], preferred_element_type=F32) + b_ref[...]


def _modulation(cvec, w_mod, b_mod, tn=1024):
    rows, d = cvec.shape
    n = w_mod.shape[1]
    blocks = 2 * (_nbytes((d, tn), F32) + _nbytes((rows, tn), F32) * 2) + _nbytes((rows, d), F32)
    return pl.pallas_call(
        _mod_kernel,
        grid=(n // tn,),
        in_specs=[
            pl.BlockSpec((rows, d), lambda j: (0, 0)),
            pl.BlockSpec((d, tn), lambda j: (0, j)),
            pl.BlockSpec((1, tn), lambda j: (0, j)),
        ],
        out_specs=pl.BlockSpec((rows, tn), lambda j: (0, j)),
        out_shape=jax.ShapeDtypeStruct((rows, n), F32),
        compiler_params=pltpu.CompilerParams(
            dimension_semantics=("parallel",), vmem_limit_bytes=_vmem_limit(blocks)),
        name="modulation",
    )(cvec, w_mod, b_mod)


def _modulated_norm(x, norm_w, shift, scale):
    y = x * lax.rsqrt(jnp.mean(x * x, axis=-1, keepdims=True) + EPS)
    return (y * norm_w) * (1.0 + scale) + shift


def _inproj_kernel(x_ref, nw_ref, shift_ref, scale_ref, w_ref, seg_ref, qkw_ref, tab_ref, o_ref, h_ref,
                   *, kinds):
    j = pl.program_id(2)
    tm, tn = o_ref.shape[1], o_ref.shape[2]
    seg_w = seg_ref.shape[0]

    @pl.when(j == 0)
    def _():
        h = _modulated_norm(x_ref[0], nw_ref[...], shift_ref[0], scale_ref[0])
        h_ref[...] = h.astype(BF16)

    def qk_epilogue(which):
        quarter = HEAD_DIM // 4
        half = tm // 2
        w128 = qkw_ref[which:which + 1, :]
        for r0 in (0, half):
            rows = pl.ds(r0, half)
            cos, sin_prev, sin_next = (tab_ref[i, rows, :] for i in range(3))
            acc = jnp.dot(h_ref[rows, :], w_ref[...], preferred_element_type=F32)
            for s0 in range(0, tn, seg_w):
                a = acc[:, s0:s0 + seg_w]
                ms = jnp.dot((a * a).astype(BF16), seg_ref[...], preferred_element_type=F32)
                y = a * lax.rsqrt(ms + EPS)
                for c in range(0, seg_w, V7X_LANES):
                    yc = y[:, c:c + V7X_LANES] * w128
                    out = (yc * cos + pltpu.roll(yc, quarter, 1) * sin_prev
                           + pltpu.roll(yc, V7X_LANES - quarter, 1) * sin_next)
                    o_ref[0, rows, s0 + c:s0 + c + V7X_LANES] = out.astype(BF16)

    def project():
        return jnp.dot(h_ref[...], w_ref[...], preferred_element_type=F32)

    def in_range(name):
        j0, n = kinds.index(name), kinds.count(name)
        return (j >= j0) & (j < j0 + n)

    for name in dict.fromkeys(kinds):
        if name == "q":
            pl.when(in_range(name))(functools.partial(qk_epilogue, 0))
        elif name == "k":
            pl.when(in_range(name))(functools.partial(qk_epilogue, 1))
        elif name == "g":
            @pl.when(in_range(name))
            def _():
                o_ref[0] = jax.nn.sigmoid(project()).astype(BF16)
        else:
            @pl.when(in_range(name))
            def _():
                o_ref[0] = project().astype(BF16)


def _input_projection(x, norm_w, shift, scale, w_in_bf, col_block0, kinds, seg_mean, qk_w, tables, tm, tn):
    B, L, D = x.shape
    vec = lambda: pl.BlockSpec((1, 1, D), lambda i, b, j: (b, 0, 0))
    blocks = (2 * (_nbytes((tm, D), F32) + _nbytes((D, tn), BF16) + _nbytes((3, tm, V7X_LANES), F32)
                   + _nbytes((tm, tn), BF16))
              + _nbytes((tm, D), BF16) + 2 * _nbytes((tm, tn), F32))
    return pl.pallas_call(
        functools.partial(_inproj_kernel, kinds=tuple(kinds)),
        grid=(L // tm, B, len(kinds)),
        in_specs=[
            pl.BlockSpec((1, tm, D), lambda i, b, j: (b, i, 0)),
            pl.BlockSpec((1, D), lambda i, b, j: (0, 0)),
            vec(), vec(),
            pl.BlockSpec((D, tn), lambda i, b, j: (0, col_block0 + j)),
            pl.BlockSpec(seg_mean.shape, lambda i, b, j: (0, 0)),
            pl.BlockSpec(qk_w.shape, lambda i, b, j: (0, 0)),
            pl.BlockSpec((3, tm, V7X_LANES), lambda i, b, j: (0, i, 0)),
        ],
        out_specs=pl.BlockSpec((1, tm, tn), lambda i, b, j: (b, i, j)),
        out_shape=jax.ShapeDtypeStruct((B, L, tn * len(kinds)), BF16),
        scratch_shapes=[pltpu.VMEM((tm, D), BF16)],
        compiler_params=pltpu.CompilerParams(
            dimension_semantics=("parallel", "parallel", "arbitrary"),
            vmem_limit_bytes=_vmem_limit(blocks)),
        name="input_projection",
    )(x, norm_w, shift, scale, w_in_bf, seg_mean, qk_w, tables)


def _attn_kernel(fast_ref, lq1_ref, lk1_ref, lq2_ref, lk2_ref, qt_ref, k_ref, vt_ref, kc_ref, vct_ref,
                 sw_ref, o_ref, rhs_ref, acc_ref, l_ref, *, tq, tk, lam_init):
    qt = qt_ref[0]
    row = lax.broadcasted_iota(jnp.int32, qt.shape, 0)
    zero = jnp.zeros_like(qt)
    rhs_ref[:, :tq] = jnp.where(row < HEAD_DIM, qt, zero)
    rhs_ref[:, tq:] = jnp.where(row >= HEAD_DIM, qt, zero)

    def scores(kt):
        return jnp.dot(kt, rhs_ref[...], preferred_element_type=F32)

    n_tiles = k_ref.shape[1] // tk

    @pl.when(fast_ref[0] == 1)
    def _():
        tiles = [(k_ref.at[0, pl.ds(t * tk, tk), :], vt_ref.at[0, :, pl.ds(t * tk, tk)])
                 for t in range(n_tiles)] + [(kc_ref.at[0], vct_ref.at[0])]
        acc = lsum = None
        for kt_ref, vtt_ref in tiles:
            p = jnp.exp2(scores(kt_ref[...]))
            part = jnp.sum(p.reshape(p.shape[0] // 8, 8, p.shape[1]), axis=0)
            pv = jnp.dot(vtt_ref[...], p.astype(BF16), preferred_element_type=F32)
            acc = pv if acc is None else acc + pv
            lsum = part if lsum is None else lsum + part
        acc_ref[...] = acc
        l_ref[...] = jnp.sum(lsum, axis=0, keepdims=True)

    @pl.when(fast_ref[0] != 1)
    def _():
        def update(carry, kt, vtt):
            m_prev, l_prev = carry
            s = scores(kt)
            m_new = jnp.maximum(m_prev, jnp.max(s, axis=0, keepdims=True))
            alpha = jnp.exp2(m_prev - m_new)
            p = jnp.exp2(s - m_new)
            acc_ref[...] = alpha * acc_ref[...] + jnp.dot(vtt, p.astype(BF16), preferred_element_type=F32)
            return m_new, alpha * l_prev + jnp.sum(p, axis=0, keepdims=True)

        def body(t, carry):
            off = pl.multiple_of(t * tk, tk)
            return update(carry, k_ref[0, pl.ds(off, tk), :], vt_ref[0, :, pl.ds(off, tk)])

        acc_ref[...] = jnp.zeros_like(acc_ref)
        init = (jnp.full((1, 2 * tq), -jnp.inf, F32), jnp.zeros((1, 2 * tq), F32))
        carry = lax.fori_loop(0, n_tiles, body, init)
        _, l_ref[...] = update(carry, kc_ref[0], vct_ref[0])

    lam = (jnp.exp(jnp.sum(lq1_ref[...] * lk1_ref[...])) - jnp.exp(jnp.sum(lq2_ref[...] * lk2_ref[...]))
           + lam_init)
    o = acc_ref[...] / l_ref[...]
    od = o[:, :tq] - lam * o[:, tq:]
    y = od * lax.rsqrt(jnp.mean(od * od, axis=0, keepdims=True) + EPS) * sw_ref[...]
    o_ref[0] = (y * (1.0 - lam_init)).T.astype(BF16)


def _diff_attention(fast, qt, k, k_block0, vt, k_c, vt_c, lams, subln_w, lam_init, tq, tk):
    B, W, L = qt.shape
    Lc = k_c.shape[1]
    H = W // VALUE_DIM
    assert L % tk == 0
    lam_spec = pl.BlockSpec((1, HEAD_DIM), lambda b, h, i: (0, 0))
    blocks = (2 * (2 * _nbytes((tq, VALUE_DIM), BF16) + 2 * _nbytes((L, VALUE_DIM), BF16)
                   + 2 * _nbytes((Lc, VALUE_DIM), BF16))
              + 3 * _nbytes((2 * tq, VALUE_DIM), F32) + 4 * _nbytes((2 * tq, tk), F32))
    return pl.pallas_call(
        functools.partial(_attn_kernel, tq=tq, tk=tk, lam_init=lam_init),
        grid=(B, H, L // tq),
        in_specs=[
            pl.BlockSpec(memory_space=pltpu.SMEM),
            lam_spec, lam_spec, lam_spec, lam_spec,
            pl.BlockSpec((1, VALUE_DIM, tq), lambda b, h, i: (b, h, i)),
            pl.BlockSpec((1, L, VALUE_DIM), lambda b, h, i: (b, 0, k_block0 + h)),
            pl.BlockSpec((1, VALUE_DIM, L), lambda b, h, i: (b, h, 0)),
            pl.BlockSpec((1, Lc, VALUE_DIM), lambda b, h, i: (b, 0, h)),
            pl.BlockSpec((1, VALUE_DIM, Lc), lambda b, h, i: (b, h, 0)),
            pl.BlockSpec((VALUE_DIM, 1), lambda b, h, i: (0, 0)),
        ],
        out_specs=pl.BlockSpec((1, tq, VALUE_DIM), lambda b, h, i: (b, i, h)),
        out_shape=jax.ShapeDtypeStruct((B, L, W), BF16),
        scratch_shapes=[pltpu.VMEM((VALUE_DIM, 2 * tq), BF16),
                        pltpu.VMEM((VALUE_DIM, 2 * tq), F32), pltpu.VMEM((1, 2 * tq), F32)],
        compiler_params=pltpu.CompilerParams(
            dimension_semantics=("parallel", "parallel", "parallel"),
            vmem_limit_bytes=_vmem_limit(blocks)),
        name="diff_attention",
    )(fast, *lams, qt, k, vt, k_c, vt_c, subln_w)


def _mixer_kernel(heads_ref, u_ref, uprev_ref, unext_ref, g_ref, x_ref, gate_ref,
                  wa_ref, wb_ref, pw_ref, ps_ref, wo_ref, o_ref, ubuf_ref, y_ref, *, tm, seq_len):
    i = pl.program_id(1)
    nt = pl.num_programs(1)
    D = x_ref.shape[2]
    group = u_ref.shape[2] // len(POOL_WINDOWS)

    ubuf_ref[pl.ds(POOL_HALO, tm), :] = u_ref[0].astype(F32)
    ubuf_ref[pl.ds(0, POOL_HALO), :] = jnp.where(i > 0, uprev_ref[0].astype(F32), 0.0)
    ubuf_ref[pl.ds(POOL_HALO + tm, POOL_HALO), :] = jnp.where(i < nt - 1, unext_ref[0].astype(F32), 0.0)

    t = i * tm + lax.broadcasted_iota(jnp.int32, (tm, 1), 0)
    for gi, w in enumerate(POOL_WINDOWS):
        cols = slice(gi * group, (gi + 1) * group)
        total = None
        for off in range(-(w // 2), w - w // 2):
            part = ubuf_ref[pl.ds(POOL_HALO + off, tm), cols]
            total = part if total is None else total + part
        lo = jnp.maximum(t - w // 2, 0)
        hi = jnp.minimum(t + (w - w // 2), seq_len)
        d = total / (hi - lo).astype(F32) - ubuf_ref[pl.ds(POOL_HALO, tm), cols]
        yg = jnp.dot(d.astype(BF16), pw_ref[gi], preferred_element_type=F32)
        y_ref[:, cols] = (yg * ps_ref[:, cols]).astype(BF16)

    y_a = jnp.dot(heads_ref[0], wa_ref[...], preferred_element_type=F32)
    y_b = jnp.dot(y_ref[...], wb_ref[...], preferred_element_type=F32)
    mixed = g_ref[0, :, :D].astype(F32) * y_a + g_ref[0, :, D:].astype(F32) * y_b
    mix = jnp.dot(mixed.astype(BF16), wo_ref[...], preferred_element_type=F32)
    o_ref[0] = x_ref[0] + gate_ref[0] * mix


def _token_mixer(heads, p_all, u_block, g_block, x, gate, w_a_up, w_b_up, pool_w, pool_scale, w_o, tm):
    B, L, D = x.shape
    AW, PW = heads.shape[2], w_b_up.shape[0]
    hb = tm // POOL_HALO
    n_halo = L // POOL_HALO
    const = lambda shape: pl.BlockSpec(shape, lambda b, i: (0,) * len(shape),
                                       pipeline_mode=pl.Buffered(1))
    weights = (_nbytes((AW, D), BF16) + _nbytes((PW, D), BF16) + _nbytes(pool_w.shape, BF16)
               + _nbytes((D, D), BF16))
    blocks = (weights + 2 * (_nbytes((tm, AW), BF16) + _nbytes((tm, PW), BF16)
                             + _nbytes((tm, 2 * D), BF16) + 2 * _nbytes((tm, D), F32))
              + _nbytes((tm + 2 * POOL_HALO, PW), F32) + _nbytes((tm, PW), BF16)
              + 4 * _nbytes((tm, D), F32))
    return pl.pallas_call(
        functools.partial(_mixer_kernel, tm=tm, seq_len=L),
        grid=(B, L // tm),
        in_specs=[
            pl.BlockSpec((1, tm, AW), lambda b, i: (b, i, 0)),
            pl.BlockSpec((1, tm, PW), lambda b, i: (b, i, u_block)),
            pl.BlockSpec((1, POOL_HALO, PW), lambda b, i: (b, jnp.maximum(i * hb - 1, 0), u_block)),
            pl.BlockSpec((1, POOL_HALO, PW), lambda b, i: (b, jnp.minimum((i + 1) * hb, n_halo - 1), u_block)),
            pl.BlockSpec((1, tm, 2 * D), lambda b, i: (b, i, g_block)),
            pl.BlockSpec((1, tm, D), lambda b, i: (b, i, 0)),
            pl.BlockSpec((1, 1, D), lambda b, i: (b, 0, 0)),
            const((AW, D)), const((PW, D)), const(pool_w.shape), const((1, PW)), const((D, D)),
        ],
        out_specs=pl.BlockSpec((1, tm, D), lambda b, i: (b, i, 0)),
        out_shape=jax.ShapeDtypeStruct((B, L, D), F32),
        scratch_shapes=[pltpu.VMEM((tm + 2 * POOL_HALO, PW), F32), pltpu.VMEM((tm, PW), BF16)],
        compiler_params=pltpu.CompilerParams(
            dimension_semantics=("parallel", "parallel"),
            vmem_limit_bytes=_vmem_limit(blocks)),
        name="token_mixer",
    )(heads, p_all, p_all, p_all, p_all, x, gate, w_a_up, w_b_up, pool_w, pool_scale, w_o)


def _mlp_kernel(x_ref, nw_ref, shift_ref, scale_ref, gate_ref, w1_ref, w2_ref, o_ref, h_ref):
    j = pl.program_id(2)

    @pl.when(j == 0)
    def _():
        h = _modulated_norm(x_ref[0], nw_ref[...], shift_ref[0], scale_ref[0])
        h_ref[...] = h.astype(BF16)
        o_ref[...] = jnp.zeros_like(o_ref)

    a = jnp.maximum(jnp.dot(h_ref[...], w1_ref[...], preferred_element_type=F32), 0.0)
    a2 = (a * a).astype(BF16)
    tf = w1_ref.shape[1]
    for n in range(0, o_ref.shape[2], tf):
        o_ref[0, :, n:n + tf] += jnp.dot(a2, w2_ref[:, n:n + tf], preferred_element_type=F32)

    @pl.when(j == pl.num_programs(2) - 1)
    def _():
        o_ref[0] = x_ref[0] + gate_ref[0] * o_ref[0]


def _mlp(x, norm_w, shift, scale, gate, w1, w2, tm, tf):
    B, L, D = x.shape
    F = w1.shape[1]
    vec = lambda: pl.BlockSpec((1, 1, D), lambda b, i, j: (b, 0, 0))
    blocks = (2 * (2 * _nbytes((tm, D), F32) + _nbytes((D, tf), BF16) + _nbytes((tf, D), BF16))
              + _nbytes((tm, D), BF16) + 2 * _nbytes((tm, tf), F32) + _nbytes((tm, D), F32))
    return pl.pallas_call(
        _mlp_kernel,
        grid=(B, L // tm, F // tf),
        in_specs=[
            pl.BlockSpec((1, tm, D), lambda b, i, j: (b, i, 0)),
            pl.BlockSpec((1, D), lambda b, i, j: (0, 0)),
            vec(), vec(), vec(),
            pl.BlockSpec((D, tf), lambda b, i, j: (0, j)),
            pl.BlockSpec((tf, D), lambda b, i, j: (j, 0)),
        ],
        out_specs=pl.BlockSpec((1, tm, D), lambda b, i, j: (b, i, 0)),
        out_shape=jax.ShapeDtypeStruct((B, L, D), F32),
        scratch_shapes=[pltpu.VMEM((tm, D), BF16)],
        compiler_params=pltpu.CompilerParams(
            dimension_semantics=("parallel", "parallel", "arbitrary"),
            vmem_limit_bytes=_vmem_limit(blocks)),
        name="sq_relu_mlp",
    )(x, norm_w, shift, scale, gate, w1, w2)


def _rope_tables(seq_len, rope):
    if not rope:
        ones = jnp.ones((seq_len, V7X_LANES), F32)
        return jnp.stack([ones, jnp.zeros_like(ones), jnp.zeros_like(ones)])
    t = jnp.arange(seq_len)
    row, col = t // GRID_W, t % GRID_W
    half = HEAD_DIM // 2
    quarter = half // 2
    inv_freq = ROPE_THETA ** (-jnp.arange(0, half, 2, dtype=F32) / half)
    lane = jnp.arange(V7X_LANES)
    e = lane % HEAD_DIM
    pos = jnp.where((e // half)[None, :] == 0, row[:, None], col[:, None]).astype(F32)
    ang = pos * inv_freq[e % quarter][None, :]
    cos, sin = jnp.cos(ang), jnp.sin(ang)
    second = ((e % half) >= quarter)[None, :]
    return jnp.stack([cos, jnp.where(second, sin, 0.0), jnp.where(second, 0.0, -sin)])


def _pick(n, pref):
    t = min(pref, n)
    while n % t:
        t //= 2
    return t


def kernel(x, c, ctx, c_ctx, w_mod, b_mod, norm_attn_w, w_in, q_norm_w, k_norm_w, lambda_q1, lambda_k1,
           lambda_q2, lambda_k2, subln_w, pool_w, pool_scale, w_a_up, w_b_up, w_o, norm_mlp_w, w_ff1, w_ff2):
    B, L, D = x.shape
    Lc = ctx.shape[1]
    depth = w_mod.shape[0]
    qk_w = 2 * ATTN_HEADS * HEAD_DIM
    tn = qk_w
    assert depth == 1 and w_in.shape[2] == 4 * tn + 2 * D and L % GRID_W == 0
    assert 2 * D == 2 * tn * 2 and pool_w.shape[1] * pool_w.shape[2] == tn

    for l in range(depth):
        lam_init = 0.8 - 0.6 * math.exp(-0.3 * l)

        rows = 8
        cvec = jnp.zeros((rows, D), F32).at[:B].set(c).at[B].set(c_ctx)
        mod = _modulation(cvec, w_mod[l], b_mod[l][None, :])
        sa, ca, ga, sm, cm, gm = [mod[:B, None, k * D:(k + 1) * D] for k in range(6)]
        sa_c, ca_c = [jnp.broadcast_to(mod[B, k * D:(k + 1) * D], (B, 1, D)) for k in range(2)]

        w_in_bf = w_in[l].astype(BF16)
        qw128 = jnp.tile(q_norm_w[l], 2) * (HEAD_DIM ** -0.5 * math.log2(math.e))
        kw128 = jnp.tile(k_norm_w[l], 2)
        qk_w128 = jnp.stack([qw128, kw128])
        nw = norm_attn_w[l][None, :]

        seg_w = tn // 2
        seg = jnp.kron(jnp.eye(seg_w // HEAD_DIM, dtype=F32),
                       jnp.full((HEAD_DIM, HEAD_DIM), 1.0 / HEAD_DIM)).astype(BF16)
        pc = _input_projection(ctx, nw, sa_c, ca_c, w_in_bf, 1, ("k", "v"), seg, qk_w128,
                               _rope_tables(Lc, rope=False), tm=_pick(Lc, 256), tn=tn)
        p = _input_projection(x, nw, sa, ca, w_in_bf, 0, ("q", "k", "v", "u", "g", "g", "g", "g"), seg,
                              qk_w128, _rope_tables(L, rope=True), tm=_pick(L, 1024), tn=tn)

        lams = [v[l][None, :] for v in (lambda_q1, lambda_k1, lambda_q2, lambda_k2)]
        swap = lambda a: jnp.swapaxes(a, 1, 2)
        score_bound = (HEAD_DIM * jnp.max(jnp.abs(qw128)) * jnp.max(jnp.abs(kw128))) * BF16_SLACK
        fast = (score_bound <= SCORE_BOUND).astype(jnp.int32).reshape(1)
        heads = _diff_attention(fast, swap(p[:, :, :tn]), p, tn // VALUE_DIM, swap(p[:, :, 2 * tn:3 * tn]),
                                pc, swap(pc[:, :, tn:]), lams, subln_w[l][:, None], lam_init,
                                tq=_pick(L, 512), tk=_pick(L, 1024))

        x = _token_mixer(heads, p, 3, 1, x, ga, w_a_up[l].astype(BF16), w_b_up[l].astype(BF16),
                         pool_w[l].astype(BF16), pool_scale[l][None, :], w_o[l].astype(BF16),
                         tm=_pick(L, 256))
        x = _mlp(x, norm_mlp_w[l][None, :], sm, cm, gm, w_ff1[l].astype(BF16), w_ff2[l].astype(BF16),
                 tm=_pick(L, 1024), tf=512)
    return x
```

```python
import functools
import math

import jax
import jax.numpy as jnp
from jax import lax
from jax.experimental import pallas as pl
from jax.experimental.pallas import tpu as pltpu

F32 = jnp.float32
BF16 = jnp.bfloat16

ATTN_HEADS = 8
HEAD_DIM = 64
VALUE_DIM = 2 * HEAD_DIM
SCORE_BOUND = 64.0
BF16_SLACK = 1.0 + 2.0 ** -6
GRID_W = 64
ROPE_THETA = 10000.0
POOL_WINDOWS = (2, 4, 8, 16)
POOL_HALO = 16
EPS = 1e-6

V7X_LANES = 128
V7X_VMEM_BYTES = 64 * 1024 * 1024
V7X_VMEM_CAP = V7X_VMEM_BYTES - 6 * 1024 * 1024


def _vmem_limit(block_bytes):
    return int(min(V7X_VMEM_CAP, block_bytes * 1.25 + 8 * 1024 * 1024))


def _nbytes(shape, dtype):
    return math.prod(shape) * jnp.dtype(dtype).itemsize


def _mod_kernel(c_ref, w_ref, b_ref, o_ref):
    cv = c_ref[...]
    act = cv * jax.nn.sigmoid(cv)
    o_ref[...] = jnp.dot(act, w_ref[...], preferred_element_type=F32) + b_ref[...]


def _modulation(cvec, w_mod, b_mod, tn=1024):
    rows, d = cvec.shape
    n = w_mod.shape[1]
    blocks = 2 * (_nbytes((d, tn), F32) + _nbytes((rows, tn), F32) * 2) + _nbytes((rows, d), F32)
    return pl.pallas_call(
        _mod_kernel,
        grid=(n // tn,),
        in_specs=[
            pl.BlockSpec((rows, d), lambda j: (0, 0)),
            pl.BlockSpec((d, tn), lambda j: (0, j)),
            pl.BlockSpec((1, tn), lambda j: (0, j)),
        ],
        out_specs=pl.BlockSpec((rows, tn), lambda j: (0, j)),
        out_shape=jax.ShapeDtypeStruct((rows, n), F32),
        compiler_params=pltpu.CompilerParams(
            dimension_semantics=("parallel",), vmem_limit_bytes=_vmem_limit(blocks)),
        name="modulation",
    )(cvec, w_mod, b_mod)


def _modulated_norm(x, norm_w, shift, scale):
    y = x * lax.rsqrt(jnp.mean(x * x, axis=-1, keepdims=True) + EPS)
    return (y * norm_w) * (1.0 + scale) + shift


def _inproj_kernel(x_ref, nw_ref, shift_ref, scale_ref, w_ref, seg_ref, qkw_ref, tab_ref, o_ref, h_ref,
                   *, kinds):
    j = pl.program_id(2)
    tm, tn = o_ref.shape[1], o_ref.shape[2]
    seg_w = seg_ref.shape[0]

    def qk_epilogue(which, first):
        quarter = HEAD_DIM // 4
        half = tm // 2
        w128 = qkw_ref[which:which + 1, :]
        for r0 in (0, half):
            rows = pl.ds(r0, half)
            if first:
                h = _modulated_norm(x_ref[0, rows, :], nw_ref[...], shift_ref[0], scale_ref[0])
                h_ref[rows, :] = h.astype(BF16)
            cos, sin_prev, sin_next = (tab_ref[i, rows, :] for i in range(3))
            acc = jnp.dot(h_ref[rows, :], w_ref[...], preferred_element_type=F32)
            for s0 in range(0, tn, seg_w):
                a = acc[:, s0:s0 + seg_w]
                ms = jnp.dot((a * a).astype(BF16), seg_ref[...], preferred_element_type=F32)
                y = a * lax.rsqrt(ms + EPS)
                for c in range(0, seg_w, V7X_LANES):
                    yc = y[:, c:c + V7X_LANES] * w128
                    out = (yc * cos + pltpu.roll(yc, quarter, 1) * sin_prev
                           + pltpu.roll(yc, V7X_LANES - quarter, 1) * sin_next)
                    o_ref[0, rows, s0 + c:s0 + c + V7X_LANES] = out.astype(BF16)

    def project():
        return jnp.dot(h_ref[...], w_ref[...], preferred_element_type=F32)

    def in_range(name):
        j0, n = kinds.index(name), kinds.count(name)
        return (j >= j0) & (j < j0 + n)

    assert kinds[0] in ("q", "k") and kinds.count(kinds[0]) == 1
    for name in dict.fromkeys(kinds):
        if name in ("q", "k"):
            pl.when(in_range(name))(functools.partial(qk_epilogue, ("q", "k").index(name), name == kinds[0]))
        elif name == "g":
            @pl.when(in_range(name))
            def _():
                o_ref[0] = jax.nn.sigmoid(project()).astype(BF16)
        else:
            @pl.when(in_range(name))
            def _():
                o_ref[0] = project().astype(BF16)


def _input_projection(x, norm_w, shift, scale, w_in_bf, col_block0, kinds, seg_mean, qk_w, tables, tm, tn):
    B, L, D = x.shape
    vec = lambda: pl.BlockSpec((1, 1, D), lambda i, b, j: (b, 0, 0))
    blocks = (2 * (_nbytes((tm, D), F32) + _nbytes((D, tn), BF16) + _nbytes((3, tm, V7X_LANES), F32)
                   + _nbytes((tm, tn), BF16))
              + _nbytes((tm, D), BF16) + 2 * _nbytes((tm, tn), F32))
    return pl.pallas_call(
        functools.partial(_inproj_kernel, kinds=tuple(kinds)),
        grid=(L // tm, B, len(kinds)),
        in_specs=[
            pl.BlockSpec((1, tm, D), lambda i, b, j: (b, i, 0)),
            pl.BlockSpec((1, D), lambda i, b, j: (0, 0)),
            vec(), vec(),
            pl.BlockSpec((D, tn), lambda i, b, j: (0, col_block0 + j)),
            pl.BlockSpec(seg_mean.shape, lambda i, b, j: (0, 0)),
            pl.BlockSpec(qk_w.shape, lambda i, b, j: (0, 0)),
            pl.BlockSpec((3, tm, V7X_LANES), lambda i, b, j: (0, i, 0)),
        ],
        out_specs=pl.BlockSpec((1, tm, tn), lambda i, b, j: (b, i, j)),
        out_shape=jax.ShapeDtypeStruct((B, L, tn * len(kinds)), BF16),
        scratch_shapes=[pltpu.VMEM((tm, D), BF16)],
        compiler_params=pltpu.CompilerParams(
            dimension_semantics=("parallel", "parallel", "arbitrary"),
            vmem_limit_bytes=_vmem_limit(blocks)),
        name="input_projection",
    )(x, norm_w, shift, scale, w_in_bf, seg_mean, qk_w, tables)


def _attn_kernel(fast_ref, lq1_ref, lk1_ref, lq2_ref, lk2_ref, qt_ref, k_ref, vt_ref, kc_ref, vct_ref,
                 sw_ref, o_ref, rhs_ref, acc_ref, l_ref, *, tq, tk, lam_init):
    qt = qt_ref[0]
    row = lax.broadcasted_iota(jnp.int32, qt.shape, 0)
    zero = jnp.zeros_like(qt)
    rhs_ref[:, :tq] = jnp.where(row < HEAD_DIM, qt, zero)
    rhs_ref[:, tq:] = jnp.where(row >= HEAD_DIM, qt, zero)

    def scores(kt):
        return jnp.dot(kt, rhs_ref[...], preferred_element_type=F32)

    n_tiles = k_ref.shape[1] // tk

    @pl.when(fast_ref[0] == 1)
    def _():
        tiles = [(k_ref.at[0, pl.ds(t * tk, tk), :], vt_ref.at[0, :, pl.ds(t * tk, tk)])
                 for t in range(n_tiles)] + [(kc_ref.at[0], vct_ref.at[0])]
        acc = lsum = None
        for kt_ref, vtt_ref in tiles:
            p = jnp.exp2(scores(kt_ref[...]))
            part = jnp.sum(p.reshape(p.shape[0] // 8, 8, p.shape[1]), axis=0)
            pv = jnp.dot(vtt_ref[...], p.astype(BF16), preferred_element_type=F32)
            acc = pv if acc is None else acc + pv
            lsum = part if lsum is None else lsum + part
        acc_ref[...] = acc
        l_ref[...] = jnp.sum(lsum, axis=0, keepdims=True)

    @pl.when(fast_ref[0] != 1)
    def _():
        def update(carry, kt, vtt):
            m_prev, l_prev = carry
            s = scores(kt)
            m_new = jnp.maximum(m_prev, jnp.max(s, axis=0, keepdims=True))
            alpha = jnp.exp2(m_prev - m_new)
            p = jnp.exp2(s - m_new)
            acc_ref[...] = alpha * acc_ref[...] + jnp.dot(vtt, p.astype(BF16), preferred_element_type=F32)
            return m_new, alpha * l_prev + jnp.sum(p, axis=0, keepdims=True)

        def body(t, carry):
            off = pl.multiple_of(t * tk, tk)
            return update(carry, k_ref[0, pl.ds(off, tk), :], vt_ref[0, :, pl.ds(off, tk)])

        acc_ref[...] = jnp.zeros_like(acc_ref)
        init = (jnp.full((1, 2 * tq), -jnp.inf, F32), jnp.zeros((1, 2 * tq), F32))
        carry = lax.fori_loop(0, n_tiles, body, init)
        _, l_ref[...] = update(carry, kc_ref[0], vct_ref[0])

    lam = (jnp.exp(jnp.sum(lq1_ref[...] * lk1_ref[...])) - jnp.exp(jnp.sum(lq2_ref[...] * lk2_ref[...]))
           + lam_init)
    o = acc_ref[...] / l_ref[...]
    od = o[:, :tq] - lam * o[:, tq:]
    y = od * lax.rsqrt(jnp.mean(od * od, axis=0, keepdims=True) + EPS) * sw_ref[...]
    o_ref[0] = (y * (1.0 - lam_init)).T.astype(BF16)


def _diff_attention(fast, qt, k, k_block0, vt, k_c, vt_c, lams, subln_w, lam_init, tq, tk):
    B, W, L = qt.shape
    Lc = k_c.shape[1]
    H = W // VALUE_DIM
    assert L % tk == 0
    lam_spec = pl.BlockSpec((1, HEAD_DIM), lambda b, h, i: (0, 0))
    blocks = (2 * (2 * _nbytes((tq, VALUE_DIM), BF16) + 2 * _nbytes((L, VALUE_DIM), BF16)
                   + 2 * _nbytes((Lc, VALUE_DIM), BF16))
              + 3 * _nbytes((2 * tq, VALUE_DIM), F32) + 4 * _nbytes((2 * tq, tk), F32))
    return pl.pallas_call(
        functools.partial(_attn_kernel, tq=tq, tk=tk, lam_init=lam_init),
        grid=(B, H, L // tq),
        in_specs=[
            pl.BlockSpec(memory_space=pltpu.SMEM),
            lam_spec, lam_spec, lam_spec, lam_spec,
            pl.BlockSpec((1, VALUE_DIM, tq), lambda b, h, i: (b, h, i)),
            pl.BlockSpec((1, L, VALUE_DIM), lambda b, h, i: (b, 0, k_block0 + h)),
            pl.BlockSpec((1, VALUE_DIM, L), lambda b, h, i: (b, h, 0)),
            pl.BlockSpec((1, Lc, VALUE_DIM), lambda b, h, i: (b, 0, h)),
            pl.BlockSpec((1, VALUE_DIM, Lc), lambda b, h, i: (b, h, 0)),
            pl.BlockSpec((VALUE_DIM, 1), lambda b, h, i: (0, 0)),
        ],
        out_specs=pl.BlockSpec((1, tq, VALUE_DIM), lambda b, h, i: (b, i, h)),
        out_shape=jax.ShapeDtypeStruct((B, L, W), BF16),
        scratch_shapes=[pltpu.VMEM((VALUE_DIM, 2 * tq), BF16),
                        pltpu.VMEM((VALUE_DIM, 2 * tq), F32), pltpu.VMEM((1, 2 * tq), F32)],
        compiler_params=pltpu.CompilerParams(
            dimension_semantics=("parallel", "parallel", "parallel"),
            vmem_limit_bytes=_vmem_limit(blocks)),
        name="diff_attention",
    )(fast, *lams, qt, k, vt, k_c, vt_c, subln_w)


def _mixer_kernel(heads_ref, u_ref, uprev_ref, unext_ref, g_ref, x_ref, gate_ref,
                  wa_ref, wb_ref, pw_ref, ps_ref, wo_ref, o_ref, ubuf_ref, y_ref, ya_ref, *, tm, seq_len):
    i = pl.program_id(1)
    nt = pl.num_programs(1)
    D = x_ref.shape[2]
    group = u_ref.shape[2] // len(POOL_WINDOWS)

    ubuf_ref[pl.ds(POOL_HALO, tm), :] = u_ref[0].astype(F32)
    ubuf_ref[pl.ds(0, POOL_HALO), :] = jnp.where(i > 0, uprev_ref[0].astype(F32), 0.0)
    ubuf_ref[pl.ds(POOL_HALO + tm, POOL_HALO), :] = jnp.where(i < nt - 1, unext_ref[0].astype(F32), 0.0)

    t = i * tm + lax.broadcasted_iota(jnp.int32, (tm, 1), 0)
    ya_cols = D // len(POOL_WINDOWS)
    for gi, w in enumerate(POOL_WINDOWS):
        a_cols = slice(gi * ya_cols, (gi + 1) * ya_cols)
        ya_ref[:, a_cols] = jnp.dot(heads_ref[0], wa_ref[:, a_cols], preferred_element_type=F32)

        cols = slice(gi * group, (gi + 1) * group)
        total = None
        for off in range(-(w // 2), w - w // 2):
            part = ubuf_ref[pl.ds(POOL_HALO + off, tm), cols]
            total = part if total is None else total + part
        lo = jnp.maximum(t - w // 2, 0)
        hi = jnp.minimum(t + (w - w // 2), seq_len)
        d = total * (1.0 / (hi - lo).astype(F32)) - ubuf_ref[pl.ds(POOL_HALO, tm), cols]
        yg = jnp.dot(d.astype(BF16), pw_ref[gi], preferred_element_type=F32)
        y_ref[:, cols] = (yg * ps_ref[:, cols]).astype(BF16)

    y_b = jnp.dot(y_ref[...], wb_ref[...], preferred_element_type=F32)
    mixed = g_ref[0, :, :D].astype(F32) * ya_ref[...] + g_ref[0, :, D:].astype(F32) * y_b
    mix = jnp.dot(mixed.astype(BF16), wo_ref[...], preferred_element_type=F32)
    o_ref[0] = x_ref[0] + gate_ref[0] * mix


def _token_mixer(heads, p_all, u_block, g_block, x, gate, w_a_up, w_b_up, pool_w, pool_scale, w_o, tm):
    B, L, D = x.shape
    AW, PW = heads.shape[2], w_b_up.shape[0]
    hb = tm // POOL_HALO
    n_halo = L // POOL_HALO
    const = lambda shape: pl.BlockSpec(shape, lambda b, i: (0,) * len(shape),
                                       pipeline_mode=pl.Buffered(1))
    weights = (_nbytes((AW, D), BF16) + _nbytes((PW, D), BF16) + _nbytes(pool_w.shape, BF16)
               + _nbytes((D, D), BF16))
    blocks = (weights + 2 * (_nbytes((tm, AW), BF16) + _nbytes((tm, PW), BF16)
                             + _nbytes((tm, 2 * D), BF16) + 2 * _nbytes((tm, D), F32))
              + _nbytes((tm + 2 * POOL_HALO, PW), F32) + _nbytes((tm, PW), BF16)
              + 4 * _nbytes((tm, D), F32))
    return pl.pallas_call(
        functools.partial(_mixer_kernel, tm=tm, seq_len=L),
        grid=(B, L // tm),
        in_specs=[
            pl.BlockSpec((1, tm, AW), lambda b, i: (b, i, 0)),
            pl.BlockSpec((1, tm, PW), lambda b, i: (b, i, u_block)),
            pl.BlockSpec((1, POOL_HALO, PW), lambda b, i: (b, jnp.maximum(i * hb - 1, 0), u_block)),
            pl.BlockSpec((1, POOL_HALO, PW), lambda b, i: (b, jnp.minimum((i + 1) * hb, n_halo - 1), u_block)),
            pl.BlockSpec((1, tm, 2 * D), lambda b, i: (b, i, g_block)),
            pl.BlockSpec((1, tm, D), lambda b, i: (b, i, 0)),
            pl.BlockSpec((1, 1, D), lambda b, i: (b, 0, 0)),
            const((AW, D)), const((PW, D)), const(pool_w.shape), const((1, PW)), const((D, D)),
        ],
        out_specs=pl.BlockSpec((1, tm, D), lambda b, i: (b, i, 0)),
        out_shape=jax.ShapeDtypeStruct((B, L, D), F32),
        scratch_shapes=[pltpu.VMEM((tm + 2 * POOL_HALO, PW), F32), pltpu.VMEM((tm, PW), BF16),
                        pltpu.VMEM((tm, D), F32)],
        compiler_params=pltpu.CompilerParams(
            dimension_semantics=("parallel", "parallel"),
            vmem_limit_bytes=_vmem_limit(blocks)),
        name="token_mixer",
    )(heads, p_all, p_all, p_all, p_all, x, gate, w_a_up, w_b_up, pool_w, pool_scale, w_o)


def _mlp_kernel(x_ref, nw_ref, shift_ref, scale_ref, gate_ref, w1_ref, w2_ref, o_ref, h_ref):
    j = pl.program_id(2)
    tm, tf = h_ref.shape[0], w1_ref.shape[1]

    def hidden(h):
        a = jnp.maximum(jnp.dot(h, w1_ref[...], preferred_element_type=F32), 0.0)
        return (a * a).astype(BF16)

    @pl.when(j == 0)
    def _():
        chunk = tm // 4
        for r0 in range(0, tm, chunk):
            rows = pl.ds(r0, chunk)
            h = _modulated_norm(x_ref[0, rows, :], nw_ref[...], shift_ref[0], scale_ref[0]).astype(BF16)
            h_ref[rows, :] = h
            a2 = hidden(h)
            for n in range(0, o_ref.shape[2], tf):
                o_ref[0, rows, n:n + tf] = jnp.dot(a2, w2_ref[:, n:n + tf], preferred_element_type=F32)

    @pl.when(j > 0)
    def _():
        a2 = hidden(h_ref[...])
        for n in range(0, o_ref.shape[2], tf):
            o_ref[0, :, n:n + tf] += jnp.dot(a2, w2_ref[:, n:n + tf], preferred_element_type=F32)

    @pl.when(j == pl.num_programs(2) - 1)
    def _():
        o_ref[0] = x_ref[0] + gate_ref[0] * o_ref[0]


def _mlp(x, norm_w, shift, scale, gate, w1, w2, tm, tf):
    B, L, D = x.shape
    F = w1.shape[1]
    vec = lambda: pl.BlockSpec((1, 1, D), lambda b, i, j: (b, 0, 0))
    blocks = (2 * (2 * _nbytes((tm, D), F32) + _nbytes((D, tf), BF16) + _nbytes((tf, D), BF16))
              + _nbytes((tm, D), BF16) + 2 * _nbytes((tm, tf), F32) + _nbytes((tm, D), F32))
    return pl.pallas_call(
        _mlp_kernel,
        grid=(B, L // tm, F // tf),
        in_specs=[
            pl.BlockSpec((1, tm, D), lambda b, i, j: (b, i, 0)),
            pl.BlockSpec((1, D), lambda b, i, j: (0, 0)),
            vec(), vec(), vec(),
            pl.BlockSpec((D, tf), lambda b, i, j: (0, j)),
            pl.BlockSpec((tf, D), lambda b, i, j: (j, 0)),
        ],
        out_specs=pl.BlockSpec((1, tm, D), lambda b, i, j: (b, i, 0)),
        out_shape=jax.ShapeDtypeStruct((B, L, D), F32),
        scratch_shapes=[pltpu.VMEM((tm, D), BF16)],
        compiler_params=pltpu.CompilerParams(
            dimension_semantics=("parallel", "parallel", "arbitrary"),
            vmem_limit_bytes=_vmem_limit(blocks)),
        name="sq_relu_mlp",
    )(x, norm_w, shift, scale, gate, w1, w2)


def _rope_tables(seq_len, rope):
    if not rope:
        ones = jnp.ones((seq_len, V7X_LANES), F32)
        return jnp.stack([ones, jnp.zeros_like(ones), jnp.zeros_like(ones)])
    t = jnp.arange(seq_len)
    row, col = t // GRID_W, t % GRID_W
    half = HEAD_DIM // 2
    quarter = half // 2
    inv_freq = ROPE_THETA ** (-jnp.arange(0, half, 2, dtype=F32) / half)
    lane = jnp.arange(V7X_LANES)
    e = lane % HEAD_DIM
    pos = jnp.where((e // half)[None, :] == 0, row[:, None], col[:, None]).astype(F32)
    ang = pos * inv_freq[e % quarter][None, :]
    cos, sin = jnp.cos(ang), jnp.sin(ang)
    second = ((e % half) >= quarter)[None, :]
    return jnp.stack([cos, jnp.where(second, sin, 0.0), jnp.where(second, 0.0, -sin)])


def _pick(n, pref):
    t = min(pref, n)
    while n % t:
        t //= 2
    return t


def kernel(x, c, ctx, c_ctx, w_mod, b_mod, norm_attn_w, w_in, q_norm_w, k_norm_w, lambda_q1, lambda_k1,
           lambda_q2, lambda_k2, subln_w, pool_w, pool_scale, w_a_up, w_b_up, w_o, norm_mlp_w, w_ff1, w_ff2):
    B, L, D = x.shape
    Lc = ctx.shape[1]
    depth = w_mod.shape[0]
    qk_w = 2 * ATTN_HEADS * HEAD_DIM
    tn = qk_w
    assert depth == 1 and w_in.shape[2] == 4 * tn + 2 * D and L % GRID_W == 0
    assert 2 * D == 2 * tn * 2 and pool_w.shape[1] * pool_w.shape[2] == tn

    for l in range(depth):
        lam_init = 0.8 - 0.6 * math.exp(-0.3 * l)

        rows = 8
        cvec = jnp.zeros((rows, D), F32).at[:B].set(c).at[B].set(c_ctx)
        mod = _modulation(cvec, w_mod[l], b_mod[l][None, :])
        sa, ca, ga, sm, cm, gm = [mod[:B, None, k * D:(k + 1) * D] for k in range(6)]
        sa_c, ca_c = [jnp.broadcast_to(mod[B, k * D:(k + 1) * D], (B, 1, D)) for k in range(2)]

        w_in_bf = w_in[l].astype(BF16)
        qw128 = jnp.tile(q_norm_w[l], 2) * (HEAD_DIM ** -0.5 * math.log2(math.e))
        kw128 = jnp.tile(k_norm_w[l], 2)
        qk_w128 = jnp.stack([qw128, kw128])
        nw = norm_attn_w[l][None, :]

        seg_w = tn // 2
        seg = jnp.kron(jnp.eye(seg_w // HEAD_DIM, dtype=F32),
                       jnp.full((HEAD_DIM, HEAD_DIM), 1.0 / HEAD_DIM)).astype(BF16)
        pc = _input_projection(ctx, nw, sa_c, ca_c, w_in_bf, 1, ("k", "v"), seg, qk_w128,
                               _rope_tables(Lc, rope=False), tm=_pick(Lc, 256), tn=tn)
        p = _input_projection(x, nw, sa, ca, w_in_bf, 0, ("q", "k", "v", "u", "g", "g", "g", "g"), seg,
                              qk_w128, _rope_tables(L, rope=True), tm=_pick(L, 1024), tn=tn)

        lams = [v[l][None, :] for v in (lambda_q1, lambda_k1, lambda_q2, lambda_k2)]
        swap = lambda a: jnp.swapaxes(a, 1, 2)
        score_bound = (HEAD_DIM * jnp.max(jnp.abs(qw128)) * jnp.max(jnp.abs(kw128))) * BF16_SLACK
        fast = (score_bound <= SCORE_BOUND).astype(jnp.int32).reshape(1)
        heads = _diff_attention(fast, swap(p[:, :, :tn]), p, tn // VALUE_DIM, swap(p[:, :, 2 * tn:3 * tn]),
                                pc, swap(pc[:, :, tn:]), lams, subln_w[l][:, None], lam_init,
                                tq=_pick(L, 512), tk=_pick(L, 1024))

        x = _token_mixer(heads, p, 3, 1, x, ga, w_a_up[l].astype(BF16), w_b_up[l].astype(BF16),
                         pool_w[l].astype(BF16), pool_scale[l][None, :], w_o[l].astype(BF16),
                         tm=_pick(L, 256))
        x = _mlp(x, norm_mlp_w[l][None, :], sm, cm, gm, w_ff1[l].astype(BF16), w_ff2[l].astype(BF16),
                 tm=_pick(L, 1024), tf=512)
    return x
```

```python
import functools
import math

import jax
import jax.numpy as jnp
from jax import lax
from jax.experimental import pallas as pl
from jax.experimental.pallas import tpu as pltpu

F32 = jnp.float32
BF16 = jnp.bfloat16

ATTN_HEADS = 8
HEAD_DIM = 64
VALUE_DIM = 2 * HEAD_DIM
SCORE_BOUND = 64.0
BF16_SLACK = 1.0 + 2.0 ** -6
GRID_W = 64
ROPE_THETA = 10000.0
POOL_WINDOWS = (2, 4, 8, 16)
POOL_HALO = 16
EPS = 1e-6

V7X_LANES = 128
V7X_VMEM_BYTES = 64 * 1024 * 1024
V7X_VMEM_CAP = V7X_VMEM_BYTES - 6 * 1024 * 1024


def _vmem_limit(block_bytes):
    return int(min(V7X_VMEM_CAP, block_bytes * 1.25 + 8 * 1024 * 1024))


def _nbytes(shape, dtype):
    return math.prod(shape) * jnp.dtype(dtype).itemsize


def _mod_kernel(c_ref, w_ref, b_ref, o_ref):
    cv = c_ref[...]
    act = cv * jax.nn.sigmoid(cv)
    o_ref[...] = jnp.dot(act, w_ref[...], preferred_element_type=F32) + b_ref[...]


def _modulation(cvec, w_mod, b_mod, tn=1024):
    rows, d = cvec.shape
    n = w_mod.shape[1]
    blocks = 2 * (_nbytes((d, tn), F32) + _nbytes((rows, tn), F32) * 2) + _nbytes((rows, d), F32)
    return pl.pallas_call(
        _mod_kernel,
        grid=(n // tn,),
        in_specs=[
            pl.BlockSpec((rows, d), lambda j: (0, 0)),
            pl.BlockSpec((d, tn), lambda j: (0, j)),
            pl.BlockSpec((1, tn), lambda j: (0, j)),
        ],
        out_specs=pl.BlockSpec((rows, tn), lambda j: (0, j)),
        out_shape=jax.ShapeDtypeStruct((rows, n), F32),
        compiler_params=pltpu.CompilerParams(
            dimension_semantics=("parallel",), vmem_limit_bytes=_vmem_limit(blocks)),
        name="modulation",
    )(cvec, w_mod, b_mod)


def _modulated_norm(x, norm_w, shift, scale):
    y = x * lax.rsqrt(jnp.mean(x * x, axis=-1, keepdims=True) + EPS)
    return (y * norm_w) * (1.0 + scale) + shift


def _inproj_kernel(x_ref, nw_ref, shift_ref, scale_ref, w_ref, seg_ref, qkw_ref, tab_ref, o_ref, h_ref,
                   *, kinds):
    j = pl.program_id(2)
    tm, tn = o_ref.shape[1], o_ref.shape[2]
    seg_w = seg_ref.shape[0]

    def qk_epilogue(which, first):
        quarter = HEAD_DIM // 4
        half = tm // 2
        w128 = qkw_ref[which:which + 1, :]
        for r0 in (0, half):
            rows = pl.ds(r0, half)
            if first:
                h = _modulated_norm(x_ref[0, rows, :], nw_ref[...], shift_ref[0], scale_ref[0])
                h_ref[rows, :] = h.astype(BF16)
            cos, sin_prev, sin_next = (tab_ref[i, rows, :] for i in range(3))
            acc = jnp.dot(h_ref[rows, :], w_ref[...], preferred_element_type=F32)
            for s0 in range(0, tn, seg_w):
                a = acc[:, s0:s0 + seg_w]
                ms = jnp.dot((a * a).astype(BF16), seg_ref[...], preferred_element_type=F32)
                y = a * lax.rsqrt(ms + EPS)
                for c in range(0, seg_w, V7X_LANES):
                    yc = y[:, c:c + V7X_LANES] * w128
                    out = (yc * cos + pltpu.roll(yc, quarter, 1) * sin_prev
                           + pltpu.roll(yc, V7X_LANES - quarter, 1) * sin_next)
                    o_ref[0, rows, s0 + c:s0 + c + V7X_LANES] = out.astype(BF16)

    def project():
        return jnp.dot(h_ref[...], w_ref[...], preferred_element_type=F32)

    def in_range(name):
        j0, n = kinds.index(name), kinds.count(name)
        return (j >= j0) & (j < j0 + n)

    assert kinds[0] in ("q", "k") and kinds.count(kinds[0]) == 1
    for name in dict.fromkeys(kinds):
        if name in ("q", "k"):
            pl.when(in_range(name))(functools.partial(qk_epilogue, ("q", "k").index(name), name == kinds[0]))
        elif name == "g":
            @pl.when(in_range(name))
            def _():
                o_ref[0] = jax.nn.sigmoid(project()).astype(BF16)
        else:
            @pl.when(in_range(name))
            def _():
                o_ref[0] = project().astype(BF16)


def _input_projection(x, norm_w, shift, scale, w_in_bf, col_block0, kinds, seg_mean, qk_w, tables, tm, tn):
    B, L, D = x.shape
    vec = lambda: pl.BlockSpec((1, 1, D), lambda i, b, j: (b, 0, 0))
    blocks = (2 * (_nbytes((tm, D), F32) + _nbytes((D, tn), BF16) + _nbytes((3, tm, V7X_LANES), F32)
                   + _nbytes((tm, tn), BF16))
              + _nbytes((tm, D), BF16) + 2 * _nbytes((tm, tn), F32))
    return pl.pallas_call(
        functools.partial(_inproj_kernel, kinds=tuple(kinds)),
        grid=(L // tm, B, len(kinds)),
        in_specs=[
            pl.BlockSpec((1, tm, D), lambda i, b, j: (b, i, 0)),
            pl.BlockSpec((1, D), lambda i, b, j: (0, 0)),
            vec(), vec(),
            pl.BlockSpec((D, tn), lambda i, b, j: (0, col_block0 + j)),
            pl.BlockSpec(seg_mean.shape, lambda i, b, j: (0, 0)),
            pl.BlockSpec(qk_w.shape, lambda i, b, j: (0, 0)),
            pl.BlockSpec((3, tm, V7X_LANES), lambda i, b, j: (0, i, 0)),
        ],
        out_specs=pl.BlockSpec((1, tm, tn), lambda i, b, j: (b, i, j)),
        out_shape=jax.ShapeDtypeStruct((B, L, tn * len(kinds)), BF16),
        scratch_shapes=[pltpu.VMEM((tm, D), BF16)],
        compiler_params=pltpu.CompilerParams(
            dimension_semantics=("parallel", "parallel", "arbitrary"),
            vmem_limit_bytes=_vmem_limit(blocks)),
        name="input_projection",
    )(x, norm_w, shift, scale, w_in_bf, seg_mean, qk_w, tables)


def _attn_kernel(fast_ref, lq1_ref, lk1_ref, lq2_ref, lk2_ref, q_ref, k_ref, v_ref, kc_ref, vc_ref,
                 sw_ref, o_ref, rhs_ref, acc_ref, *, tq, tk, lam_init):
    subs = q_ref.shape[1] // tq
    lane = lax.broadcasted_iota(jnp.int32, (tq, VALUE_DIM), 1)
    for s in range(subs):
        q = q_ref[0, pl.ds(s * tq, tq), :]
        zero = jnp.zeros_like(q)
        rhs_ref[s, :tq, :] = jnp.where(lane < HEAD_DIM, q, zero)
        rhs_ref[s, tq:, :] = jnp.where(lane >= HEAD_DIM, q, zero)

    def scores(s, kt):
        return lax.dot_general(kt, rhs_ref[s], (((1,), (1,)), ((), ())), preferred_element_type=F32)

    def values_t_dot(v, p):
        return lax.dot_general(v, p, (((0,), (0,)), ((), ())), preferred_element_type=F32)

    lam = (jnp.exp(jnp.sum(lq1_ref[...] * lk1_ref[...])) - jnp.exp(jnp.sum(lq2_ref[...] * lk2_ref[...]))
           + lam_init)

    def finish(s, acc, l):
        o = acc * (1.0 / l)
        od = o[:, :tq] - lam * o[:, tq:]
        y = od * lax.rsqrt(jnp.mean(od * od, axis=0, keepdims=True) + EPS) * sw_ref[...]
        o_ref[0, pl.ds(s * tq, tq), :] = (y * (1.0 - lam_init)).T.astype(BF16)

    n_tiles = k_ref.shape[1] // tk

    @pl.when(fast_ref[0] == 1)
    def _():
        tiles = [(k_ref.at[0, pl.ds(t * tk, tk), :], v_ref.at[0, pl.ds(t * tk, tk), :])
                 for t in range(n_tiles)] + [(kc_ref.at[0], vc_ref.at[0])]

        def run(s, some_tiles, state):
            acc, lsum = state
            for kt_ref, vtt_ref in some_tiles:
                p = jnp.exp2(scores(s, kt_ref[...]))
                part = jnp.sum(p.reshape(p.shape[0] // 8, 8, p.shape[1]), axis=0)
                pv = values_t_dot(vtt_ref[...], p.astype(BF16))
                acc = pv if acc is None else acc + pv
                lsum = part if lsum is None else lsum + part
            return acc, lsum

        state = run(0, tiles, (None, None))
        for s in range(1, subs):
            started = run(s, tiles[:1], (None, None))
            finish(s - 1, state[0], jnp.sum(state[1], axis=0, keepdims=True))
            state = run(s, tiles[1:], started)
        finish(subs - 1, state[0], jnp.sum(state[1], axis=0, keepdims=True))

    @pl.when(fast_ref[0] != 1)
    def _():
        for s in range(subs):
            def update(carry, kt, vtt):
                m_prev, l_prev = carry
                sc = scores(s, kt)
                m_new = jnp.maximum(m_prev, jnp.max(sc, axis=0, keepdims=True))
                alpha = jnp.exp2(m_prev - m_new)
                p = jnp.exp2(sc - m_new)
                acc_ref[...] = alpha * acc_ref[...] + values_t_dot(vtt, p.astype(BF16))
                return m_new, alpha * l_prev + jnp.sum(p, axis=0, keepdims=True)

            def body(t, carry):
                off = pl.multiple_of(t * tk, tk)
                return update(carry, k_ref[0, pl.ds(off, tk), :], v_ref[0, pl.ds(off, tk), :])

            acc_ref[...] = jnp.zeros_like(acc_ref)
            init = (jnp.full((1, 2 * tq), -jnp.inf, F32), jnp.zeros((1, 2 * tq), F32))
            carry = lax.fori_loop(0, n_tiles, body, init)
            _, l = update(carry, kc_ref[0], vc_ref[0])
            finish(s, acc_ref[...], l)


def _diff_attention(fast, p, p_c, col0, lams, subln_w, lam_init, tq, tk, subs):
    B, L, _ = p.shape
    Lc = p_c.shape[1]
    H = ATTN_HEADS
    assert L % tk == 0
    lam_spec = pl.BlockSpec((1, HEAD_DIM), lambda b, h, i: (0, 0))
    keys = lambda rows, c0: pl.BlockSpec((1, rows, VALUE_DIM), lambda b, h, i: (b, 0, c0 + h))
    blocks = (2 * (2 * _nbytes((tq, VALUE_DIM), BF16) + 2 * _nbytes((L, VALUE_DIM), BF16)
                   + 2 * _nbytes((Lc, VALUE_DIM), BF16))
              + 3 * _nbytes((2 * tq, VALUE_DIM), F32) + 4 * _nbytes((2 * tq, tk), F32))
    return pl.pallas_call(
        functools.partial(_attn_kernel, tq=tq, tk=tk, lam_init=lam_init),
        grid=(B, H, L // (subs * tq)),
        in_specs=[
            pl.BlockSpec(memory_space=pltpu.SMEM),
            lam_spec, lam_spec, lam_spec, lam_spec,
            pl.BlockSpec((1, subs * tq, VALUE_DIM), lambda b, h, i: (b, i, col0["q"] + h)),
            keys(L, col0["k"]), keys(L, col0["v"]), keys(Lc, col0["kc"]), keys(Lc, col0["vc"]),
            pl.BlockSpec((VALUE_DIM, 1), lambda b, h, i: (0, 0)),
        ],
        out_specs=pl.BlockSpec((1, subs * tq, VALUE_DIM), lambda b, h, i: (b, i, h)),
        out_shape=jax.ShapeDtypeStruct((B, L, H * VALUE_DIM), BF16),
        scratch_shapes=[pltpu.VMEM((subs, 2 * tq, VALUE_DIM), BF16),
                        pltpu.VMEM((VALUE_DIM, 2 * tq), F32)],
        compiler_params=pltpu.CompilerParams(
            dimension_semantics=("parallel", "parallel", "parallel"),
            vmem_limit_bytes=_vmem_limit(blocks)),
        name="diff_attention",
    )(fast, *lams, p, p, p, p_c, p_c, subln_w)


def _mixer_kernel(heads_ref, u_ref, uprev_ref, unext_ref, g_ref, x_ref, gate_ref,
                  wa_ref, wb_ref, pw_ref, ps_ref, wo_ref, o_ref, ubuf_ref, y_ref, ya_ref, *, tm, seq_len):
    i = pl.program_id(1)
    nt = pl.num_programs(1)
    D = x_ref.shape[2]
    group = u_ref.shape[2] // len(POOL_WINDOWS)

    ubuf_ref[pl.ds(POOL_HALO, tm), :] = u_ref[0].astype(F32)
    ubuf_ref[pl.ds(0, POOL_HALO), :] = jnp.where(i > 0, uprev_ref[0].astype(F32), 0.0)
    ubuf_ref[pl.ds(POOL_HALO + tm, POOL_HALO), :] = jnp.where(i < nt - 1, unext_ref[0].astype(F32), 0.0)

    t = i * tm + lax.broadcasted_iota(jnp.int32, (tm, 1), 0)
    ya_cols = D // len(POOL_WINDOWS)
    for gi, w in enumerate(POOL_WINDOWS):
        a_cols = slice(gi * ya_cols, (gi + 1) * ya_cols)
        ya_ref[:, a_cols] = jnp.dot(heads_ref[0], wa_ref[:, a_cols], preferred_element_type=F32)

        cols = slice(gi * group, (gi + 1) * group)
        total = None
        for off in range(-(w // 2), w - w // 2):
            part = ubuf_ref[pl.ds(POOL_HALO + off, tm), cols]
            total = part if total is None else total + part
        lo = jnp.maximum(t - w // 2, 0)
        hi = jnp.minimum(t + (w - w // 2), seq_len)
        d = total * (1.0 / (hi - lo).astype(F32)) - ubuf_ref[pl.ds(POOL_HALO, tm), cols]
        yg = jnp.dot(d.astype(BF16), pw_ref[gi], preferred_element_type=F32)
        y_ref[:, cols] = (yg * ps_ref[:, cols]).astype(BF16)

    y_b = jnp.dot(y_ref[...], wb_ref[...], preferred_element_type=F32)
    mixed = g_ref[0, :, :D].astype(F32) * ya_ref[...] + g_ref[0, :, D:].astype(F32) * y_b
    mix = jnp.dot(mixed.astype(BF16), wo_ref[...], preferred_element_type=F32)
    o_ref[0] = x_ref[0] + gate_ref[0] * mix


def _token_mixer(heads, p_all, u_block, g_block, x, gate, w_a_up, w_b_up, pool_w, pool_scale, w_o, tm):
    B, L, D = x.shape
    AW, PW = heads.shape[2], w_b_up.shape[0]
    hb = tm // POOL_HALO
    n_halo = L // POOL_HALO
    const = lambda shape: pl.BlockSpec(shape, lambda b, i: (0,) * len(shape),
                                       pipeline_mode=pl.Buffered(1))
    weights = (_nbytes((AW, D), BF16) + _nbytes((PW, D), BF16) + _nbytes(pool_w.shape, BF16)
               + _nbytes((D, D), BF16))
    blocks = (weights + 2 * (_nbytes((tm, AW), BF16) + _nbytes((tm, PW), BF16)
                             + _nbytes((tm, 2 * D), BF16) + 2 * _nbytes((tm, D), F32))
              + _nbytes((tm + 2 * POOL_HALO, PW), F32) + _nbytes((tm, PW), BF16)
              + 4 * _nbytes((tm, D), F32))
    return pl.pallas_call(
        functools.partial(_mixer_kernel, tm=tm, seq_len=L),
        grid=(B, L // tm),
        in_specs=[
            pl.BlockSpec((1, tm, AW), lambda b, i: (b, i, 0)),
            pl.BlockSpec((1, tm, PW), lambda b, i: (b, i, u_block)),
            pl.BlockSpec((1, POOL_HALO, PW), lambda b, i: (b, jnp.maximum(i * hb - 1, 0), u_block)),
            pl.BlockSpec((1, POOL_HALO, PW), lambda b, i: (b, jnp.minimum((i + 1) * hb, n_halo - 1), u_block)),
            pl.BlockSpec((1, tm, 2 * D), lambda b, i: (b, i, g_block)),
            pl.BlockSpec((1, tm, D), lambda b, i: (b, i, 0)),
            pl.BlockSpec((1, 1, D), lambda b, i: (b, 0, 0)),
            const((AW, D)), const((PW, D)), const(pool_w.shape), const((1, PW)), const((D, D)),
        ],
        out_specs=pl.BlockSpec((1, tm, D), lambda b, i: (b, i, 0)),
        out_shape=jax.ShapeDtypeStruct((B, L, D), F32),
        scratch_shapes=[pltpu.VMEM((tm + 2 * POOL_HALO, PW), F32), pltpu.VMEM((tm, PW), BF16),
                        pltpu.VMEM((tm, D), F32)],
        compiler_params=pltpu.CompilerParams(
            dimension_semantics=("parallel", "parallel"),
            vmem_limit_bytes=_vmem_limit(blocks)),
        name="token_mixer",
    )(heads, p_all, p_all, p_all, p_all, x, gate, w_a_up, w_b_up, pool_w, pool_scale, w_o)


def _mlp_kernel(x_ref, nw_ref, shift_ref, scale_ref, gate_ref, w1_ref, w2_ref, o_ref, h_ref):
    j = pl.program_id(2)
    tm, tf = h_ref.shape[0], w1_ref.shape[1]

    def hidden(h):
        a = jnp.maximum(jnp.dot(h, w1_ref[...], preferred_element_type=F32), 0.0)
        return (a * a).astype(BF16)

    @pl.when(j == 0)
    def _():
        chunk = tm // 4
        for r0 in range(0, tm, chunk):
            rows = pl.ds(r0, chunk)
            h = _modulated_norm(x_ref[0, rows, :], nw_ref[...], shift_ref[0], scale_ref[0]).astype(BF16)
            h_ref[rows, :] = h
            a2 = hidden(h)
            for n in range(0, o_ref.shape[2], tf):
                o_ref[0, rows, n:n + tf] = jnp.dot(a2, w2_ref[:, n:n + tf], preferred_element_type=F32)

    @pl.when(j > 0)
    def _():
        a2 = hidden(h_ref[...])
        for n in range(0, o_ref.shape[2], tf):
            o_ref[0, :, n:n + tf] += jnp.dot(a2, w2_ref[:, n:n + tf], preferred_element_type=F32)

    @pl.when(j == pl.num_programs(2) - 1)
    def _():
        o_ref[0] = x_ref[0] + gate_ref[0] * o_ref[0]


def _mlp(x, norm_w, shift, scale, gate, w1, w2, tm, tf):
    B, L, D = x.shape
    F = w1.shape[1]
    vec = lambda: pl.BlockSpec((1, 1, D), lambda b, i, j: (b, 0, 0))
    blocks = (2 * (2 * _nbytes((tm, D), F32) + _nbytes((D, tf), BF16) + _nbytes((tf, D), BF16))
              + _nbytes((tm, D), BF16) + 2 * _nbytes((tm, tf), F32) + _nbytes((tm, D), F32))
    return pl.pallas_call(
        _mlp_kernel,
        grid=(B, L // tm, F // tf),
        in_specs=[
            pl.BlockSpec((1, tm, D), lambda b, i, j: (b, i, 0)),
            pl.BlockSpec((1, D), lambda b, i, j: (0, 0)),
            vec(), vec(), vec(),
            pl.BlockSpec((D, tf), lambda b, i, j: (0, j)),
            pl.BlockSpec((tf, D), lambda b, i, j: (j, 0)),
        ],
        out_specs=pl.BlockSpec((1, tm, D), lambda b, i, j: (b, i, 0)),
        out_shape=jax.ShapeDtypeStruct((B, L, D), F32),
        scratch_shapes=[pltpu.VMEM((tm, D), BF16)],
        compiler_params=pltpu.CompilerParams(
            dimension_semantics=("parallel", "parallel", "arbitrary"),
            vmem_limit_bytes=_vmem_limit(blocks)),
        name="sq_relu_mlp",
    )(x, norm_w, shift, scale, gate, w1, w2)


def _rope_tables(seq_len, rope):
    if not rope:
        ones = jnp.ones((seq_len, V7X_LANES), F32)
        return jnp.stack([ones, jnp.zeros_like(ones), jnp.zeros_like(ones)])
    t = jnp.arange(seq_len)
    row, col = t // GRID_W, t % GRID_W
    half = HEAD_DIM // 2
    quarter = half // 2
    inv_freq = ROPE_THETA ** (-jnp.arange(0, half, 2, dtype=F32) / half)
    lane = jnp.arange(V7X_LANES)
    e = lane % HEAD_DIM
    pos = jnp.where((e // half)[None, :] == 0, row[:, None], col[:, None]).astype(F32)
    ang = pos * inv_freq[e % quarter][None, :]
    cos, sin = jnp.cos(ang), jnp.sin(ang)
    second = ((e % half) >= quarter)[None, :]
    return jnp.stack([cos, jnp.where(second, sin, 0.0), jnp.where(second, 0.0, -sin)])


def _pick(n, pref):
    t = min(pref, n)
    while n % t:
        t //= 2
    return t


def kernel(x, c, ctx, c_ctx, w_mod, b_mod, norm_attn_w, w_in, q_norm_w, k_norm_w, lambda_q1, lambda_k1,
           lambda_q2, lambda_k2, subln_w, pool_w, pool_scale, w_a_up, w_b_up, w_o, norm_mlp_w, w_ff1, w_ff2):
    B, L, D = x.shape
    Lc = ctx.shape[1]
    depth = w_mod.shape[0]
    qk_w = 2 * ATTN_HEADS * HEAD_DIM
    tn = qk_w
    assert depth == 1 and w_in.shape[2] == 4 * tn + 2 * D and L % GRID_W == 0
    assert 2 * D == 2 * tn * 2 and pool_w.shape[1] * pool_w.shape[2] == tn

    for l in range(depth):
        lam_init = 0.8 - 0.6 * math.exp(-0.3 * l)

        rows = 8
        cvec = jnp.zeros((rows, D), F32).at[:B].set(c).at[B].set(c_ctx)
        mod = _modulation(cvec, w_mod[l], b_mod[l][None, :])
        sa, ca, ga, sm, cm, gm = [mod[:B, None, k * D:(k + 1) * D] for k in range(6)]
        sa_c, ca_c = [jnp.broadcast_to(mod[B, k * D:(k + 1) * D], (B, 1, D)) for k in range(2)]

        w_in_bf = w_in[l].astype(BF16)
        qw128 = jnp.tile(q_norm_w[l], 2) * (HEAD_DIM ** -0.5 * math.log2(math.e))
        kw128 = jnp.tile(k_norm_w[l], 2)
        qk_w128 = jnp.stack([qw128, kw128])
        nw = norm_attn_w[l][None, :]

        seg_w = tn // 2
        seg = jnp.kron(jnp.eye(seg_w // HEAD_DIM, dtype=F32),
                       jnp.full((HEAD_DIM, HEAD_DIM), 1.0 / HEAD_DIM)).astype(BF16)
        pc = _input_projection(ctx, nw, sa_c, ca_c, w_in_bf, 1, ("k", "v"), seg, qk_w128,
                               _rope_tables(Lc, rope=False), tm=_pick(Lc, 256), tn=tn)
        p = _input_projection(x, nw, sa, ca, w_in_bf, 0, ("q", "k", "v", "u", "g", "g", "g", "g"), seg,
                              qk_w128, _rope_tables(L, rope=True), tm=_pick(L, 1024), tn=tn)

        lams = [v[l][None, :] for v in (lambda_q1, lambda_k1, lambda_q2, lambda_k2)]
        score_bound = (HEAD_DIM * jnp.max(jnp.abs(qw128)) * jnp.max(jnp.abs(kw128))) * BF16_SLACK
        fast = (score_bound <= SCORE_BOUND).astype(jnp.int32).reshape(1)
        per_block = tn // VALUE_DIM
        col0 = {"q": 0, "k": per_block, "v": 2 * per_block, "kc": 0, "vc": per_block}
        heads = _diff_attention(fast, p, pc, col0, lams, subln_w[l][:, None], lam_init,
                                tq=_pick(L, 512), tk=_pick(L, 1024), subs=2 if L % 1024 == 0 else 1)

        x = _token_mixer(heads, p, 3, 1, x, ga, w_a_up[l].astype(BF16), w_b_up[l].astype(BF16),
                         pool_w[l].astype(BF16), pool_scale[l][None, :], w_o[l].astype(BF16),
                         tm=_pick(L, 256))
        x = _mlp(x, norm_mlp_w[l][None, :], sm, cm, gm, w_ff1[l].astype(BF16), w_ff2[l].astype(BF16),
                 tm=_pick(L, 1024), tf=512)
    return x
```

```python
import functools
import math

import jax
import jax.numpy as jnp
from jax import lax
from jax.experimental import pallas as pl
from jax.experimental.pallas import tpu as pltpu

F32 = jnp.float32
BF16 = jnp.bfloat16

ATTN_HEADS = 8
HEAD_DIM = 64
VALUE_DIM = 2 * HEAD_DIM
SCORE_BOUND = 64.0
BF16_SLACK = 1.0 + 2.0 ** -6
GRID_W = 64
ROPE_THETA = 10000.0
POOL_WINDOWS = (2, 4, 8, 16)
POOL_HALO = 16
EPS = 1e-6

V7X_LANES = 128
V7X_VMEM_BYTES = 64 * 1024 * 1024
V7X_VMEM_CAP = V7X_VMEM_BYTES - 6 * 1024 * 1024


def _vmem_limit(block_bytes):
    return int(min(V7X_VMEM_CAP, block_bytes * 1.25 + 8 * 1024 * 1024))


def _nbytes(shape, dtype):
    return math.prod(shape) * jnp.dtype(dtype).itemsize


def _mod_kernel(c_ref, w_ref, b_ref, o_ref):
    cv = c_ref[...]
    act = cv * jax.nn.sigmoid(cv)
    o_ref[...] = jnp.dot(act, w_ref[...], preferred_element_type=F32) + b_ref[...]


def _modulation(cvec, w_mod, b_mod, tn=1024):
    rows, d = cvec.shape
    n = w_mod.shape[1]
    blocks = 2 * (_nbytes((d, tn), F32) + _nbytes((rows, tn), F32) * 2) + _nbytes((rows, d), F32)
    return pl.pallas_call(
        _mod_kernel,
        grid=(n // tn,),
        in_specs=[
            pl.BlockSpec((rows, d), lambda j: (0, 0)),
            pl.BlockSpec((d, tn), lambda j: (0, j)),
            pl.BlockSpec((1, tn), lambda j: (0, j)),
        ],
        out_specs=pl.BlockSpec((rows, tn), lambda j: (0, j)),
        out_shape=jax.ShapeDtypeStruct((rows, n), F32),
        compiler_params=pltpu.CompilerParams(
            dimension_semantics=("parallel",), vmem_limit_bytes=_vmem_limit(blocks)),
        name="modulation",
    )(cvec, w_mod, b_mod)


def _modulated_norm(x, norm_w, shift, scale):
    y = x * lax.rsqrt(jnp.mean(x * x, axis=-1, keepdims=True) + EPS)
    return (y * norm_w) * (1.0 + scale) + shift


def _inproj_kernel(x_ref, nw_ref, shift_ref, scale_ref, w_ref, seg_ref, qkw_ref, tab_ref, o_ref, h_ref,
                   *, kinds):
    j = pl.program_id(2)
    tm, tn = o_ref.shape[1], o_ref.shape[2]
    seg_w = seg_ref.shape[0]

    def qk_epilogue(which, first):
        quarter = HEAD_DIM // 4
        half = tm // 2
        w128 = qkw_ref[which:which + 1, :]
        for r0 in (0, half):
            rows = pl.ds(r0, half)
            if first:
                h = _modulated_norm(x_ref[0, rows, :], nw_ref[...], shift_ref[0], scale_ref[0])
                h_ref[rows, :] = h.astype(BF16)
            cos, sin_prev, sin_next = (tab_ref[i, rows, :] for i in range(3))
            acc = jnp.dot(h_ref[rows, :], w_ref[...], preferred_element_type=F32)
            for s0 in range(0, tn, seg_w):
                a = acc[:, s0:s0 + seg_w]
                ms = jnp.dot((a * a).astype(BF16), seg_ref[...], preferred_element_type=F32)
                y = a * lax.rsqrt(ms + EPS)
                for c in range(0, seg_w, V7X_LANES):
                    yc = y[:, c:c + V7X_LANES] * w128
                    out = (yc * cos + pltpu.roll(yc, quarter, 1) * sin_prev
                           + pltpu.roll(yc, V7X_LANES - quarter, 1) * sin_next)
                    o_ref[0, rows, s0 + c:s0 + c + V7X_LANES] = out.astype(BF16)

    def project():
        return jnp.dot(h_ref[...], w_ref[...], preferred_element_type=F32)

    def in_range(name):
        j0, n = kinds.index(name), kinds.count(name)
        return (j >= j0) & (j < j0 + n)

    assert kinds[0] in ("q", "k") and kinds.count(kinds[0]) == 1
    for name in dict.fromkeys(kinds):
        if name in ("q", "k"):
            pl.when(in_range(name))(functools.partial(qk_epilogue, ("q", "k").index(name), name == kinds[0]))
        elif name == "g":
            @pl.when(in_range(name))
            def _():
                o_ref[0] = jax.nn.sigmoid(project()).astype(BF16)
        else:
            @pl.when(in_range(name))
            def _():
                o_ref[0] = project().astype(BF16)


def _input_projection(x, norm_w, shift, scale, w_in_bf, col_block0, kinds, seg_mean, qk_w, tables, tm, tn):
    B, L, D = x.shape
    vec = lambda: pl.BlockSpec((1, 1, D), lambda i, b, j: (b, 0, 0))
    blocks = (2 * (_nbytes((tm, D), F32) + _nbytes((D, tn), BF16) + _nbytes((3, tm, V7X_LANES), F32)
                   + _nbytes((tm, tn), BF16))
              + _nbytes((tm, D), BF16) + 2 * _nbytes((tm, tn), F32))
    return pl.pallas_call(
        functools.partial(_inproj_kernel, kinds=tuple(kinds)),
        grid=(L // tm, B, len(kinds)),
        in_specs=[
            pl.BlockSpec((1, tm, D), lambda i, b, j: (b, i, 0)),
            pl.BlockSpec((1, D), lambda i, b, j: (0, 0)),
            vec(), vec(),
            pl.BlockSpec((D, tn), lambda i, b, j: (0, col_block0 + j)),
            pl.BlockSpec(seg_mean.shape, lambda i, b, j: (0, 0)),
            pl.BlockSpec(qk_w.shape, lambda i, b, j: (0, 0)),
            pl.BlockSpec((3, tm, V7X_LANES), lambda i, b, j: (0, i, 0)),
        ],
        out_specs=pl.BlockSpec((1, tm, tn), lambda i, b, j: (b, i, j)),
        out_shape=jax.ShapeDtypeStruct((B, L, tn * len(kinds)), BF16),
        scratch_shapes=[pltpu.VMEM((tm, D), BF16)],
        compiler_params=pltpu.CompilerParams(
            dimension_semantics=("parallel", "parallel", "arbitrary"),
            vmem_limit_bytes=_vmem_limit(blocks)),
        name="input_projection",
    )(x, norm_w, shift, scale, w_in_bf, seg_mean, qk_w, tables)


def _attn_kernel(fast_ref, lq1_ref, lk1_ref, lq2_ref, lk2_ref, q_ref, k_ref, v_ref, kc_ref, vc_ref,
                 sw_ref, o_ref, rhs_ref, acc_ref, *, tq, tk, lam_init):
    subs = q_ref.shape[1] // tq
    lane = lax.broadcasted_iota(jnp.int32, (tq, VALUE_DIM), 1)
    for s in range(subs):
        q = q_ref[0, pl.ds(s * tq, tq), :]
        zero = jnp.zeros_like(q)
        rhs_ref[s, :tq, :] = jnp.where(lane < HEAD_DIM, q, zero)
        rhs_ref[s, tq:, :] = jnp.where(lane >= HEAD_DIM, q, zero)

    def scores(s, kt):
        return lax.dot_general(kt, rhs_ref[s], (((1,), (1,)), ((), ())), preferred_element_type=F32)

    def values_t_dot(v, p):
        return lax.dot_general(v, p, (((0,), (0,)), ((), ())), preferred_element_type=F32)

    lam = (jnp.exp(jnp.sum(lq1_ref[...] * lk1_ref[...])) - jnp.exp(jnp.sum(lq2_ref[...] * lk2_ref[...]))
           + lam_init)

    def finish(s, acc, l):
        o = acc * (1.0 / l)
        od = o[:, :tq] - lam * o[:, tq:]
        y = od * lax.rsqrt(jnp.mean(od * od, axis=0, keepdims=True) + EPS) * sw_ref[...]
        o_ref[0, pl.ds(s * tq, tq), :] = (y * (1.0 - lam_init)).T.astype(BF16)

    n_tiles = k_ref.shape[1] // tk

    @pl.when(fast_ref[0] == 1)
    def _():
        tiles = [(k_ref.at[0, pl.ds(t * tk, tk), :], v_ref.at[0, pl.ds(t * tk, tk), :])
                 for t in range(n_tiles)] + [(kc_ref.at[0], vc_ref.at[0])]

        def run(s, some_tiles, state):
            acc, lsum = state
            for kt_ref, vtt_ref in some_tiles:
                p = jnp.exp2(scores(s, kt_ref[...]))
                part = jnp.sum(p.reshape(p.shape[0] // 8, 8, p.shape[1]), axis=0)
                pv = values_t_dot(vtt_ref[...], p.astype(BF16))
                acc = pv if acc is None else acc + pv
                lsum = part if lsum is None else lsum + part
            return acc, lsum

        state = run(0, tiles, (None, None))
        for s in range(1, subs):
            started = run(s, tiles[:1], (None, None))
            finish(s - 1, state[0], jnp.sum(state[1], axis=0, keepdims=True))
            state = run(s, tiles[1:], started)
        finish(subs - 1, state[0], jnp.sum(state[1], axis=0, keepdims=True))

    @pl.when(fast_ref[0] != 1)
    def _():
        for s in range(subs):
            def update(carry, kt, vtt):
                m_prev, l_prev = carry
                sc = scores(s, kt)
                m_new = jnp.maximum(m_prev, jnp.max(sc, axis=0, keepdims=True))
                alpha = jnp.exp2(m_prev - m_new)
                p = jnp.exp2(sc - m_new)
                acc_ref[...] = alpha * acc_ref[...] + values_t_dot(vtt, p.astype(BF16))
                return m_new, alpha * l_prev + jnp.sum(p, axis=0, keepdims=True)

            def body(t, carry):
                off = pl.multiple_of(t * tk, tk)
                return update(carry, k_ref[0, pl.ds(off, tk), :], v_ref[0, pl.ds(off, tk), :])

            acc_ref[...] = jnp.zeros_like(acc_ref)
            init = (jnp.full((1, 2 * tq), -jnp.inf, F32), jnp.zeros((1, 2 * tq), F32))
            carry = lax.fori_loop(0, n_tiles, body, init)
            _, l = update(carry, kc_ref[0], vc_ref[0])
            finish(s, acc_ref[...], l)


def _diff_attention(fast, p, p_c, col0, lams, subln_w, lam_init, tq, tk, subs):
    B, L, _ = p.shape
    Lc = p_c.shape[1]
    H = ATTN_HEADS
    assert L % tk == 0
    lam_spec = pl.BlockSpec((1, HEAD_DIM), lambda b, h, i: (0, 0))
    keys = lambda rows, c0: pl.BlockSpec((1, rows, VALUE_DIM), lambda b, h, i: (b, 0, c0 + h))
    blocks = (2 * (2 * _nbytes((tq, VALUE_DIM), BF16) + 2 * _nbytes((L, VALUE_DIM), BF16)
                   + 2 * _nbytes((Lc, VALUE_DIM), BF16))
              + 3 * _nbytes((2 * tq, VALUE_DIM), F32) + 4 * _nbytes((2 * tq, tk), F32))
    return pl.pallas_call(
        functools.partial(_attn_kernel, tq=tq, tk=tk, lam_init=lam_init),
        grid=(B, H, L // (subs * tq)),
        in_specs=[
            pl.BlockSpec(memory_space=pltpu.SMEM),
            lam_spec, lam_spec, lam_spec, lam_spec,
            pl.BlockSpec((1, subs * tq, VALUE_DIM), lambda b, h, i: (b, i, col0["q"] + h)),
            keys(L, col0["k"]), keys(L, col0["v"]), keys(Lc, col0["kc"]), keys(Lc, col0["vc"]),
            pl.BlockSpec((VALUE_DIM, 1), lambda b, h, i: (0, 0)),
        ],
        out_specs=pl.BlockSpec((1, subs * tq, VALUE_DIM), lambda b, h, i: (b, i, h)),
        out_shape=jax.ShapeDtypeStruct((B, L, H * VALUE_DIM), BF16),
        scratch_shapes=[pltpu.VMEM((subs, 2 * tq, VALUE_DIM), BF16),
                        pltpu.VMEM((VALUE_DIM, 2 * tq), F32)],
        compiler_params=pltpu.CompilerParams(
            dimension_semantics=("parallel", "parallel", "parallel"),
            vmem_limit_bytes=_vmem_limit(blocks)),
        name="diff_attention",
    )(fast, *lams, p, p, p, p_c, p_c, subln_w)


def _mixer_kernel(heads_ref, u_ref, uprev_ref, unext_ref, g_ref, x_ref, gate_ref,
                  wa_ref, wb_ref, pw_ref, ps_ref, wo_ref, o_ref, ubuf_ref, y_ref, ya_ref, *, tm, seq_len):
    i = pl.program_id(1)
    nt = pl.num_programs(1)
    D = x_ref.shape[2]
    group = u_ref.shape[2] // len(POOL_WINDOWS)

    ubuf_ref[pl.ds(POOL_HALO, tm), :] = u_ref[0].astype(F32)
    ubuf_ref[pl.ds(0, POOL_HALO), :] = jnp.where(i > 0, uprev_ref[0].astype(F32), 0.0)
    ubuf_ref[pl.ds(POOL_HALO + tm, POOL_HALO), :] = jnp.where(i < nt - 1, unext_ref[0].astype(F32), 0.0)

    t = i * tm + lax.broadcasted_iota(jnp.int32, (tm, 1), 0)
    ya_cols = D // len(POOL_WINDOWS)
    for gi, w in enumerate(POOL_WINDOWS):
        a_cols = slice(gi * ya_cols, (gi + 1) * ya_cols)
        ya_ref[:, a_cols] = jnp.dot(heads_ref[0], wa_ref[:, a_cols], preferred_element_type=F32)

        cols = slice(gi * group, (gi + 1) * group)
        total = None
        for off in range(-(w // 2), w - w // 2):
            part = ubuf_ref[pl.ds(POOL_HALO + off, tm), cols]
            total = part if total is None else total + part
        lo = jnp.maximum(t - w // 2, 0)
        hi = jnp.minimum(t + (w - w // 2), seq_len)
        d = total * (1.0 / (hi - lo).astype(F32)) - ubuf_ref[pl.ds(POOL_HALO, tm), cols]
        yg = jnp.dot(d.astype(BF16), pw_ref[gi], preferred_element_type=F32)
        y_ref[:, cols] = (yg * ps_ref[:, cols]).astype(BF16)

    y_b = jnp.dot(y_ref[...], wb_ref[...], preferred_element_type=F32)
    mixed = g_ref[0, :, :D].astype(F32) * ya_ref[...] + g_ref[0, :, D:].astype(F32) * y_b
    mix = jnp.dot(mixed.astype(BF16), wo_ref[...], preferred_element_type=F32)
    o_ref[0] = x_ref[0] + gate_ref[0] * mix


def _token_mixer(heads, p_all, u_block, g_block, x, gate, w_a_up, w_b_up, pool_w, pool_scale, w_o, tm):
    B, L, D = x.shape
    AW, PW = heads.shape[2], w_b_up.shape[0]
    hb = tm // POOL_HALO
    n_halo = L // POOL_HALO
    const = lambda shape: pl.BlockSpec(shape, lambda b, i: (0,) * len(shape),
                                       pipeline_mode=pl.Buffered(1))
    weights = (_nbytes((AW, D), BF16) + _nbytes((PW, D), BF16) + _nbytes(pool_w.shape, BF16)
               + _nbytes((D, D), BF16))
    blocks = (weights + 2 * (_nbytes((tm, AW), BF16) + _nbytes((tm, PW), BF16)
                             + _nbytes((tm, 2 * D), BF16) + 2 * _nbytes((tm, D), F32))
              + _nbytes((tm + 2 * POOL_HALO, PW), F32) + _nbytes((tm, PW), BF16)
              + 4 * _nbytes((tm, D), F32))
    return pl.pallas_call(
        functools.partial(_mixer_kernel, tm=tm, seq_len=L),
        grid=(B, L // tm),
        in_specs=[
            pl.BlockSpec((1, tm, AW), lambda b, i: (b, i, 0)),
            pl.BlockSpec((1, tm, PW), lambda b, i: (b, i, u_block)),
            pl.BlockSpec((1, POOL_HALO, PW), lambda b, i: (b, jnp.maximum(i * hb - 1, 0), u_block)),
            pl.BlockSpec((1, POOL_HALO, PW), lambda b, i: (b, jnp.minimum((i + 1) * hb, n_halo - 1), u_block)),
            pl.BlockSpec((1, tm, 2 * D), lambda b, i: (b, i, g_block)),
            pl.BlockSpec((1, tm, D), lambda b, i: (b, i, 0)),
            pl.BlockSpec((1, 1, D), lambda b, i: (b, 0, 0)),
            const((AW, D)), const((PW, D)), const(pool_w.shape), const((1, PW)), const((D, D)),
        ],
        out_specs=pl.BlockSpec((1, tm, D), lambda b, i: (b, i, 0)),
        out_shape=jax.ShapeDtypeStruct((B, L, D), F32),
        scratch_shapes=[pltpu.VMEM((tm + 2 * POOL_HALO, PW), F32), pltpu.VMEM((tm, PW), BF16),
                        pltpu.VMEM((tm, D), F32)],
        compiler_params=pltpu.CompilerParams(
            dimension_semantics=("parallel", "parallel"),
            vmem_limit_bytes=_vmem_limit(blocks)),
        name="token_mixer",
    )(heads, p_all, p_all, p_all, p_all, x, gate, w_a_up, w_b_up, pool_w, pool_scale, w_o)


def _mlp_kernel(x_ref, nw_ref, shift_ref, scale_ref, gate_ref, w1_ref, w2_ref, o_ref, h_ref):
    j = pl.program_id(2)
    tm, tf = h_ref.shape[0], w1_ref.shape[1]

    def hidden(h):
        a = jnp.maximum(jnp.dot(h, w1_ref[...], preferred_element_type=F32), 0.0)
        return (a * a).astype(BF16)

    @pl.when(j == 0)
    def _():
        chunk = tm // 4
        for r0 in range(0, tm, chunk):
            rows = pl.ds(r0, chunk)
            h = _modulated_norm(x_ref[0, rows, :], nw_ref[...], shift_ref[0], scale_ref[0]).astype(BF16)
            h_ref[rows, :] = h
            a2 = hidden(h)
            for n in range(0, o_ref.shape[2], tf):
                o_ref[0, rows, n:n + tf] = jnp.dot(a2, w2_ref[:, n:n + tf], preferred_element_type=F32)

    @pl.when(j > 0)
    def _():
        a2 = hidden(h_ref[...])
        for n in range(0, o_ref.shape[2], tf):
            o_ref[0, :, n:n + tf] += jnp.dot(a2, w2_ref[:, n:n + tf], preferred_element_type=F32)

    @pl.when(j == pl.num_programs(2) - 1)
    def _():
        o_ref[0] = x_ref[0] + gate_ref[0] * o_ref[0]


def _mlp(x, norm_w, shift, scale, gate, w1, w2, tm, tf):
    B, L, D = x.shape
    F = w1.shape[1]
    vec = lambda: pl.BlockSpec((1, 1, D), lambda b, i, j: (b, 0, 0))
    blocks = (2 * (2 * _nbytes((tm, D), F32) + _nbytes((D, tf), BF16) + _nbytes((tf, D), BF16))
              + _nbytes((tm, D), BF16) + 2 * _nbytes((tm, tf), F32) + _nbytes((tm, D), F32))
    return pl.pallas_call(
        _mlp_kernel,
        grid=(B, L // tm, F // tf),
        in_specs=[
            pl.BlockSpec((1, tm, D), lambda b, i, j: (b, i, 0)),
            pl.BlockSpec((1, D), lambda b, i, j: (0, 0)),
            vec(), vec(), vec(),
            pl.BlockSpec((D, tf), lambda b, i, j: (0, j)),
            pl.BlockSpec((tf, D), lambda b, i, j: (j, 0)),
        ],
        out_specs=pl.BlockSpec((1, tm, D), lambda b, i, j: (b, i, 0)),
        out_shape=jax.ShapeDtypeStruct((B, L, D), F32),
        scratch_shapes=[pltpu.VMEM((tm, D), BF16)],
        compiler_params=pltpu.CompilerParams(
            dimension_semantics=("parallel", "parallel", "arbitrary"),
            vmem_limit_bytes=_vmem_limit(blocks)),
        name="sq_relu_mlp",
    )(x, norm_w, shift, scale, gate, w1, w2)


def _rope_tables(seq_len, rope):
    if not rope:
        ones = jnp.ones((seq_len, V7X_LANES), F32)
        return jnp.stack([ones, jnp.zeros_like(ones), jnp.zeros_like(ones)])
    t = jnp.arange(seq_len)
    row, col = t // GRID_W, t % GRID_W
    half = HEAD_DIM // 2
    quarter = half // 2
    inv_freq = ROPE_THETA ** (-jnp.arange(0, half, 2, dtype=F32) / half)
    lane = jnp.arange(V7X_LANES)
    e = lane % HEAD_DIM
    pos = jnp.where((e // half)[None, :] == 0, row[:, None], col[:, None]).astype(F32)
    ang = pos * inv_freq[e % quarter][None, :]
    cos, sin = jnp.cos(ang), jnp.sin(ang)
    second = ((e % half) >= quarter)[None, :]
    return jnp.stack([cos, jnp.where(second, sin, 0.0), jnp.where(second, 0.0, -sin)])


def _pick(n, pref):
    t = min(pref, n)
    while n % t:
        t //= 2
    return t


def kernel(x, c, ctx, c_ctx, w_mod, b_mod, norm_attn_w, w_in, q_norm_w, k_norm_w, lambda_q1, lambda_k1,
           lambda_q2, lambda_k2, subln_w, pool_w, pool_scale, w_a_up, w_b_up, w_o, norm_mlp_w, w_ff1, w_ff2):
    B, L, D = x.shape
    Lc = ctx.shape[1]
    depth = w_mod.shape[0]
    qk_w = 2 * ATTN_HEADS * HEAD_DIM
    tn = qk_w
    assert depth == 1 and w_in.shape[2] == 4 * tn + 2 * D and L % GRID_W == 0
    assert 2 * D == 2 * tn * 2 and pool_w.shape[1] * pool_w.shape[2] == tn

    for l in range(depth):
        lam_init = 0.8 - 0.6 * math.exp(-0.3 * l)

        rows = 8
        cvec = jnp.zeros((rows, D), F32).at[:B].set(c).at[B].set(c_ctx)
        mod = _modulation(cvec, w_mod[l], b_mod[l][None, :])
        sa, ca, ga, sm, cm, gm = [mod[:B, None, k * D:(k + 1) * D] for k in range(6)]
        sa_c, ca_c = [jnp.broadcast_to(mod[B, k * D:(k + 1) * D], (B, 1, D)) for k in range(2)]

        w_in_bf = w_in[l].astype(BF16)
        qw128 = jnp.tile(q_norm_w[l], 2) * (HEAD_DIM ** -0.5 * math.log2(math.e))
        kw128 = jnp.tile(k_norm_w[l], 2)
        qk_w128 = jnp.stack([qw128, kw128])
        nw = norm_attn_w[l][None, :]

        seg_w = tn // 2
        seg = jnp.kron(jnp.eye(seg_w // HEAD_DIM, dtype=F32),
                       jnp.full((HEAD_DIM, HEAD_DIM), 1.0 / HEAD_DIM)).astype(BF16)
        pc = _input_projection(ctx, nw, sa_c, ca_c, w_in_bf, 1, ("k", "v"), seg, qk_w128,
                               _rope_tables(Lc, rope=False), tm=_pick(Lc, 256), tn=tn)
        p = _input_projection(x, nw, sa, ca, w_in_bf, 0, ("q", "k", "v", "u", "g", "g", "g", "g"), seg,
                              qk_w128, _rope_tables(L, rope=True), tm=_pick(L, 1024), tn=tn)

        lams = [v[l][None, :] for v in (lambda_q1, lambda_k1, lambda_q2, lambda_k2)]
        score_bound = (HEAD_DIM * jnp.max(jnp.abs(qw128)) * jnp.max(jnp.abs(kw128))) * BF16_SLACK
        fast = (score_bound <= SCORE_BOUND).astype(jnp.int32).reshape(1)
        per_block = tn // VALUE_DIM
        col0 = {"q": 0, "k": per_block, "v": 2 * per_block, "kc": 0, "vc": per_block}
        heads = _diff_attention(fast, p, pc, col0, lams, subln_w[l][:, None], lam_init,
                                tq=_pick(L, 512), tk=_pick(L, 1024), subs=_pick(L // _pick(L, 512), 4))

        x = _token_mixer(heads, p, 3, 1, x, ga, w_a_up[l].astype(BF16), w_b_up[l].astype(BF16),
                         pool_w[l].astype(BF16), pool_scale[l][None, :], w_o[l].astype(BF16),
                         tm=_pick(L, 256))
        x = _mlp(x, norm_mlp_w[l][None, :], sm, cm, gm, w_ff1[l].astype(BF16), w_ff2[l].astype(BF16),
                 tm=_pick(L, 1024), tf=512)
    return x
```

```python
import functools
import math

import jax
import jax.numpy as jnp
from jax import lax
from jax.experimental import pallas as pl
from jax.experimental.pallas import tpu as pltpu

F32 = jnp.float32
BF16 = jnp.bfloat16

ATTN_HEADS = 8
HEAD_DIM = 64
VALUE_DIM = 2 * HEAD_DIM
SCORE_BOUND = 64.0
BF16_SLACK = 1.0 + 2.0 ** -6
GRID_W = 64
ROPE_THETA = 10000.0
POOL_WINDOWS = (2, 4, 8, 16)
POOL_HALO = 16
MLP_SUB = 512
EPS = 1e-6

V7X_LANES = 128
V7X_VMEM_BYTES = 64 * 1024 * 1024
V7X_VMEM_CAP = V7X_VMEM_BYTES - 6 * 1024 * 1024


def _vmem_limit(block_bytes):
    return int(min(V7X_VMEM_CAP, block_bytes * 1.25 + 8 * 1024 * 1024))


def _nbytes(shape, dtype):
    return math.prod(shape) * jnp.dtype(dtype).itemsize


def _mod_kernel(c_ref, w_ref, b_ref, o_ref):
    cv = c_ref[...]
    act = cv * jax.nn.sigmoid(cv)
    o_ref[...] = jnp.dot(act, w_ref[...], preferred_element_type=F32) + b_ref[...]


def _modulation(cvec, w_mod, b_mod, tn=1024):
    rows, d = cvec.shape
    n = w_mod.shape[1]
    blocks = 2 * (_nbytes((d, tn), F32) + _nbytes((rows, tn), F32) * 2) + _nbytes((rows, d), F32)
    return pl.pallas_call(
        _mod_kernel,
        grid=(n // tn,),
        in_specs=[
            pl.BlockSpec((rows, d), lambda j: (0, 0)),
            pl.BlockSpec((d, tn), lambda j: (0, j)),
            pl.BlockSpec((1, tn), lambda j: (0, j)),
        ],
        out_specs=pl.BlockSpec((rows, tn), lambda j: (0, j)),
        out_shape=jax.ShapeDtypeStruct((rows, n), F32),
        compiler_params=pltpu.CompilerParams(
            dimension_semantics=("parallel",), vmem_limit_bytes=_vmem_limit(blocks)),
        name="modulation",
    )(cvec, w_mod, b_mod)


def _modulated_norm(x, norm_w, shift, scale):
    y = x * lax.rsqrt(jnp.mean(x * x, axis=-1, keepdims=True) + EPS)
    return (y * norm_w) * (1.0 + scale) + shift


def _inproj_kernel(x_ref, nw_ref, shift_ref, scale_ref, w_ref, seg_ref, qkw_ref, tab_ref, o_ref, h_ref,
                   *, kinds):
    j = pl.program_id(2)
    tm, tn = o_ref.shape[1], o_ref.shape[2]
    seg_w = seg_ref.shape[0]

    def qk_epilogue(which, first):
        quarter = HEAD_DIM // 4
        half = tm // 2
        w128 = qkw_ref[which:which + 1, :]
        for r0 in (0, half):
            rows = pl.ds(r0, half)
            if first:
                h = _modulated_norm(x_ref[0, rows, :], nw_ref[...], shift_ref[0], scale_ref[0])
                h_ref[rows, :] = h.astype(BF16)
            cos, sin_prev, sin_next = (tab_ref[i, rows, :] for i in range(3))
            acc = jnp.dot(h_ref[rows, :], w_ref[...], preferred_element_type=F32)
            for s0 in range(0, tn, seg_w):
                a = acc[:, s0:s0 + seg_w]
                ms = jnp.dot((a * a).astype(BF16), seg_ref[...], preferred_element_type=F32)
                y = a * lax.rsqrt(ms + EPS)
                for c in range(0, seg_w, V7X_LANES):
                    yc = y[:, c:c + V7X_LANES] * w128
                    out = (yc * cos + pltpu.roll(yc, quarter, 1) * sin_prev
                           + pltpu.roll(yc, V7X_LANES - quarter, 1) * sin_next)
                    o_ref[0, rows, s0 + c:s0 + c + V7X_LANES] = out.astype(BF16)

    def project():
        return jnp.dot(h_ref[...], w_ref[...], preferred_element_type=F32)

    def in_range(name):
        j0, n = kinds.index(name), kinds.count(name)
        return (j >= j0) & (j < j0 + n)

    assert kinds[0] in ("q", "k") and kinds.count(kinds[0]) == 1
    for name in dict.fromkeys(kinds):
        if name in ("q", "k"):
            pl.when(in_range(name))(functools.partial(qk_epilogue, ("q", "k").index(name), name == kinds[0]))
        elif name == "g":
            @pl.when(in_range(name))
            def _():
                o_ref[0] = (0.5 * jnp.tanh(0.5 * project()) + 0.5).astype(BF16)
        else:
            @pl.when(in_range(name))
            def _():
                o_ref[0] = project().astype(BF16)


def _input_projection(x, norm_w, shift, scale, w_in_bf, col_block0, kinds, seg_mean, qk_w, tables, tm, tn):
    B, L, D = x.shape
    vec = lambda: pl.BlockSpec((1, 1, D), lambda i, b, j: (b, 0, 0))
    blocks = (2 * (_nbytes((tm, D), F32) + _nbytes((D, tn), BF16) + _nbytes((3, tm, V7X_LANES), F32)
                   + _nbytes((tm, tn), BF16))
              + _nbytes((tm, D), BF16) + 2 * _nbytes((tm, tn), F32))
    return pl.pallas_call(
        functools.partial(_inproj_kernel, kinds=tuple(kinds)),
        grid=(L // tm, B, len(kinds)),
        in_specs=[
            pl.BlockSpec((1, tm, D), lambda i, b, j: (b, i, 0)),
            pl.BlockSpec((1, D), lambda i, b, j: (0, 0)),
            vec(), vec(),
            pl.BlockSpec((D, tn), lambda i, b, j: (0, col_block0 + j)),
            pl.BlockSpec(seg_mean.shape, lambda i, b, j: (0, 0)),
            pl.BlockSpec(qk_w.shape, lambda i, b, j: (0, 0)),
            pl.BlockSpec((3, tm, V7X_LANES), lambda i, b, j: (0, i, 0)),
        ],
        out_specs=pl.BlockSpec((1, tm, tn), lambda i, b, j: (b, i, j)),
        out_shape=jax.ShapeDtypeStruct((B, L, tn * len(kinds)), BF16),
        scratch_shapes=[pltpu.VMEM((tm, D), BF16)],
        compiler_params=pltpu.CompilerParams(
            dimension_semantics=("parallel", "parallel", "arbitrary"),
            vmem_limit_bytes=_vmem_limit(blocks)),
        name="input_projection",
    )(x, norm_w, shift, scale, w_in_bf, seg_mean, qk_w, tables)


def _attn_kernel(fast_ref, lq1_ref, lk1_ref, lq2_ref, lk2_ref, q_ref, k_ref, v_ref, kc_ref, vc_ref,
                 sw_ref, o_ref, rhs_ref, acc_ref, *, tq, tk, lam_init):
    subs = q_ref.shape[1] // tq
    lane = lax.broadcasted_iota(jnp.int32, (tq, VALUE_DIM), 1)
    for s in range(subs):
        q = q_ref[0, pl.ds(s * tq, tq), :]
        zero = jnp.zeros_like(q)
        rhs_ref[s, :tq, :] = jnp.where(lane < HEAD_DIM, q, zero)
        rhs_ref[s, tq:, :] = jnp.where(lane >= HEAD_DIM, q, zero)

    def scores(s, kt):
        return lax.dot_general(kt, rhs_ref[s], (((1,), (1,)), ((), ())), preferred_element_type=F32)

    def values_t_dot(v, p):
        return lax.dot_general(v, p, (((0,), (0,)), ((), ())), preferred_element_type=F32)

    lam = (jnp.exp(jnp.sum(lq1_ref[...] * lk1_ref[...])) - jnp.exp(jnp.sum(lq2_ref[...] * lk2_ref[...]))
           + lam_init)

    def finish(s, acc, l):
        o = acc * (1.0 / l)
        od = o[:, :tq] - lam * o[:, tq:]
        y = od * lax.rsqrt(jnp.mean(od * od, axis=0, keepdims=True) + EPS) * sw_ref[...]
        o_ref[0, pl.ds(s * tq, tq), :] = (y * (1.0 - lam_init)).T.astype(BF16)

    n_tiles = k_ref.shape[1] // tk

    @pl.when(fast_ref[0] == 1)
    def _():
        tiles = [(k_ref.at[0, pl.ds(t * tk, tk), :], v_ref.at[0, pl.ds(t * tk, tk), :])
                 for t in range(n_tiles)] + [(kc_ref.at[0], vc_ref.at[0])]

        def run(s, some_tiles, state):
            acc, lsum = state
            for kt_ref, vtt_ref in some_tiles:
                p = jnp.exp2(scores(s, kt_ref[...]))
                part = jnp.sum(p.reshape(p.shape[0] // 8, 8, p.shape[1]), axis=0)
                pv = values_t_dot(vtt_ref[...], p.astype(BF16))
                acc = pv if acc is None else acc + pv
                lsum = part if lsum is None else lsum + part
            return acc, lsum

        state = run(0, tiles, (None, None))
        for s in range(1, subs):
            started = run(s, tiles[:1], (None, None))
            finish(s - 1, state[0], jnp.sum(state[1], axis=0, keepdims=True))
            state = run(s, tiles[1:], started)
        finish(subs - 1, state[0], jnp.sum(state[1], axis=0, keepdims=True))

    @pl.when(fast_ref[0] != 1)
    def _():
        for s in range(subs):
            def update(carry, kt, vtt):
                m_prev, l_prev = carry
                sc = scores(s, kt)
                m_new = jnp.maximum(m_prev, jnp.max(sc, axis=0, keepdims=True))
                alpha = jnp.exp2(m_prev - m_new)
                p = jnp.exp2(sc - m_new)
                acc_ref[...] = alpha * acc_ref[...] + values_t_dot(vtt, p.astype(BF16))
                return m_new, alpha * l_prev + jnp.sum(p, axis=0, keepdims=True)

            def body(t, carry):
                off = pl.multiple_of(t * tk, tk)
                return update(carry, k_ref[0, pl.ds(off, tk), :], v_ref[0, pl.ds(off, tk), :])

            acc_ref[...] = jnp.zeros_like(acc_ref)
            init = (jnp.full((1, 2 * tq), -jnp.inf, F32), jnp.zeros((1, 2 * tq), F32))
            carry = lax.fori_loop(0, n_tiles, body, init)
            _, l = update(carry, kc_ref[0], vc_ref[0])
            finish(s, acc_ref[...], l)


def _diff_attention(fast, p, p_c, col0, lams, subln_w, lam_init, tq, tk, subs):
    B, L, _ = p.shape
    Lc = p_c.shape[1]
    H = ATTN_HEADS
    assert L % tk == 0
    lam_spec = pl.BlockSpec((1, HEAD_DIM), lambda b, h, i: (0, 0))
    keys = lambda rows, c0: pl.BlockSpec((1, rows, VALUE_DIM), lambda b, h, i: (b, 0, c0 + h))
    blocks = (2 * (2 * _nbytes((tq, VALUE_DIM), BF16) + 2 * _nbytes((L, VALUE_DIM), BF16)
                   + 2 * _nbytes((Lc, VALUE_DIM), BF16))
              + 3 * _nbytes((2 * tq, VALUE_DIM), F32) + 4 * _nbytes((2 * tq, tk), F32))
    return pl.pallas_call(
        functools.partial(_attn_kernel, tq=tq, tk=tk, lam_init=lam_init),
        grid=(B, H, L // (subs * tq)),
        in_specs=[
            pl.BlockSpec(memory_space=pltpu.SMEM),
            lam_spec, lam_spec, lam_spec, lam_spec,
            pl.BlockSpec((1, subs * tq, VALUE_DIM), lambda b, h, i: (b, i, col0["q"] + h)),
            keys(L, col0["k"]), keys(L, col0["v"]), keys(Lc, col0["kc"]), keys(Lc, col0["vc"]),
            pl.BlockSpec((VALUE_DIM, 1), lambda b, h, i: (0, 0)),
        ],
        out_specs=pl.BlockSpec((1, subs * tq, VALUE_DIM), lambda b, h, i: (b, i, h)),
        out_shape=jax.ShapeDtypeStruct((B, L, H * VALUE_DIM), BF16),
        scratch_shapes=[pltpu.VMEM((subs, 2 * tq, VALUE_DIM), BF16),
                        pltpu.VMEM((VALUE_DIM, 2 * tq), F32)],
        compiler_params=pltpu.CompilerParams(
            dimension_semantics=("parallel", "parallel", "parallel"),
            vmem_limit_bytes=_vmem_limit(blocks)),
        name="diff_attention",
    )(fast, *lams, p, p, p, p_c, p_c, subln_w)


def _mixer_kernel(heads_ref, u_ref, uprev_ref, unext_ref, g_ref, x_ref, gate_ref,
                  wa_ref, wb_ref, pw_ref, ps_ref, wo_ref, o_ref, ubuf_ref, y_ref, ya_ref, *, tm, seq_len):
    i = pl.program_id(1)
    nt = pl.num_programs(1)
    D = x_ref.shape[2]
    group = u_ref.shape[2] // len(POOL_WINDOWS)

    ubuf_ref[pl.ds(POOL_HALO, tm), :] = u_ref[0].astype(F32)
    ubuf_ref[pl.ds(0, POOL_HALO), :] = jnp.where(i > 0, uprev_ref[0].astype(F32), 0.0)
    ubuf_ref[pl.ds(POOL_HALO + tm, POOL_HALO), :] = jnp.where(i < nt - 1, unext_ref[0].astype(F32), 0.0)

    t = i * tm + lax.broadcasted_iota(jnp.int32, (tm, 1), 0)
    ya_cols = D // len(POOL_WINDOWS)
    for gi, w in enumerate(POOL_WINDOWS):
        a_cols = slice(gi * ya_cols, (gi + 1) * ya_cols)
        ya_ref[:, a_cols] = jnp.dot(heads_ref[0], wa_ref[:, a_cols], preferred_element_type=F32)

        cols = slice(gi * group, (gi + 1) * group)
        total = None
        for off in range(-(w // 2), w - w // 2):
            part = ubuf_ref[pl.ds(POOL_HALO + off, tm), cols]
            total = part if total is None else total + part
        lo = jnp.maximum(t - w // 2, 0)
        hi = jnp.minimum(t + (w - w // 2), seq_len)
        d = total * (1.0 / (hi - lo).astype(F32)) - ubuf_ref[pl.ds(POOL_HALO, tm), cols]
        yg = jnp.dot(d.astype(BF16), pw_ref[gi], preferred_element_type=F32)
        y_ref[:, cols] = (yg * ps_ref[:, cols]).astype(BF16)

    y_b = jnp.dot(y_ref[...], wb_ref[...], preferred_element_type=F32)
    mixed = g_ref[0, :, :D].astype(F32) * ya_ref[...] + g_ref[0, :, D:].astype(F32) * y_b
    mix = jnp.dot(mixed.astype(BF16), wo_ref[...], preferred_element_type=F32)
    o_ref[0] = x_ref[0] + gate_ref[0] * mix


def _token_mixer(heads, p_all, u_block, g_block, x, gate, w_a_up, w_b_up, pool_w, pool_scale, w_o, tm):
    B, L, D = x.shape
    AW, PW = heads.shape[2], w_b_up.shape[0]
    hb = tm // POOL_HALO
    n_halo = L // POOL_HALO
    const = lambda shape: pl.BlockSpec(shape, lambda b, i: (0,) * len(shape),
                                       pipeline_mode=pl.Buffered(1))
    weights = (_nbytes((AW, D), BF16) + _nbytes((PW, D), BF16) + _nbytes(pool_w.shape, BF16)
               + _nbytes((D, D), BF16))
    blocks = (weights + 2 * (_nbytes((tm, AW), BF16) + _nbytes((tm, PW), BF16)
                             + _nbytes((tm, 2 * D), BF16) + 2 * _nbytes((tm, D), F32))
              + _nbytes((tm + 2 * POOL_HALO, PW), F32) + _nbytes((tm, PW), BF16)
              + 4 * _nbytes((tm, D), F32))
    return pl.pallas_call(
        functools.partial(_mixer_kernel, tm=tm, seq_len=L),
        grid=(B, L // tm),
        in_specs=[
            pl.BlockSpec((1, tm, AW), lambda b, i: (b, i, 0)),
            pl.BlockSpec((1, tm, PW), lambda b, i: (b, i, u_block)),
            pl.BlockSpec((1, POOL_HALO, PW), lambda b, i: (b, jnp.maximum(i * hb - 1, 0), u_block)),
            pl.BlockSpec((1, POOL_HALO, PW), lambda b, i: (b, jnp.minimum((i + 1) * hb, n_halo - 1), u_block)),
            pl.BlockSpec((1, tm, 2 * D), lambda b, i: (b, i, g_block)),
            pl.BlockSpec((1, tm, D), lambda b, i: (b, i, 0)),
            pl.BlockSpec((1, 1, D), lambda b, i: (b, 0, 0)),
            const((AW, D)), const((PW, D)), const(pool_w.shape), const((1, PW)), const((D, D)),
        ],
        out_specs=pl.BlockSpec((1, tm, D), lambda b, i: (b, i, 0)),
        out_shape=jax.ShapeDtypeStruct((B, L, D), F32),
        scratch_shapes=[pltpu.VMEM((tm + 2 * POOL_HALO, PW), F32), pltpu.VMEM((tm, PW), BF16),
                        pltpu.VMEM((tm, D), F32)],
        compiler_params=pltpu.CompilerParams(
            dimension_semantics=("parallel", "parallel"),
            vmem_limit_bytes=_vmem_limit(blocks)),
        name="token_mixer",
    )(heads, p_all, p_all, p_all, p_all, x, gate, w_a_up, w_b_up, pool_w, pool_scale, w_o)


def _mlp_kernel(x_ref, nw_ref, shift_ref, scale_ref, gate_ref, w1_ref, w2_ref, o_ref, h_ref):
    j = pl.program_id(2)
    tm, tf = h_ref.shape[0], w1_ref.shape[1]

    sub = min(tf, MLP_SUB)

    def ffn(h, rows, assign_first):
        for s0 in range(0, tf, sub):
            a = jnp.maximum(jnp.dot(h, w1_ref[:, s0:s0 + sub], preferred_element_type=F32), 0.0)
            a2 = (a * a).astype(BF16)
            for n in range(0, o_ref.shape[2], sub):
                part = jnp.dot(a2, w2_ref[s0:s0 + sub, n:n + sub], preferred_element_type=F32)
                if assign_first and s0 == 0:
                    o_ref[0, rows, n:n + sub] = part
                else:
                    o_ref[0, rows, n:n + sub] += part

    @pl.when(j == 0)
    def _():
        chunk = tm // 4
        for r0 in range(0, tm, chunk):
            rows = pl.ds(r0, chunk)
            h = _modulated_norm(x_ref[0, rows, :], nw_ref[...], shift_ref[0], scale_ref[0]).astype(BF16)
            h_ref[rows, :] = h
            ffn(h, rows, assign_first=True)

    @pl.when(j > 0)
    def _():
        ffn(h_ref[...], pl.ds(0, tm), assign_first=False)

    @pl.when(j == pl.num_programs(2) - 1)
    def _():
        o_ref[0] = x_ref[0] + gate_ref[0] * o_ref[0]


def _mlp(x, norm_w, shift, scale, gate, w1, w2, tm, tf):
    B, L, D = x.shape
    F = w1.shape[1]
    vec = lambda: pl.BlockSpec((1, 1, D), lambda b, i, j: (b, 0, 0))
    blocks = (2 * (2 * _nbytes((tm, D), F32) + _nbytes((D, tf), BF16) + _nbytes((tf, D), BF16))
              + _nbytes((tm, D), BF16) + 3 * _nbytes((tm, min(tf, MLP_SUB)), F32))
    return pl.pallas_call(
        _mlp_kernel,
        grid=(B, L // tm, F // tf),
        in_specs=[
            pl.BlockSpec((1, tm, D), lambda b, i, j: (b, i, 0)),
            pl.BlockSpec((1, D), lambda b, i, j: (0, 0)),
            vec(), vec(), vec(),
            pl.BlockSpec((D, tf), lambda b, i, j: (0, j)),
            pl.BlockSpec((tf, D), lambda b, i, j: (j, 0)),
        ],
        out_specs=pl.BlockSpec((1, tm, D), lambda b, i, j: (b, i, 0)),
        out_shape=jax.ShapeDtypeStruct((B, L, D), F32),
        scratch_shapes=[pltpu.VMEM((tm, D), BF16)],
        compiler_params=pltpu.CompilerParams(
            dimension_semantics=("parallel", "parallel", "arbitrary"),
            vmem_limit_bytes=_vmem_limit(blocks)),
        name="sq_relu_mlp",
    )(x, norm_w, shift, scale, gate, w1, w2)


def _rope_tables(seq_len, rope):
    if not rope:
        ones = jnp.ones((seq_len, V7X_LANES), F32)
        return jnp.stack([ones, jnp.zeros_like(ones), jnp.zeros_like(ones)])
    t = jnp.arange(seq_len)
    row, col = t // GRID_W, t % GRID_W
    half = HEAD_DIM // 2
    quarter = half // 2
    inv_freq = ROPE_THETA ** (-jnp.arange(0, half, 2, dtype=F32) / half)
    lane = jnp.arange(V7X_LANES)
    e = lane % HEAD_DIM
    pos = jnp.where((e // half)[None, :] == 0, row[:, None], col[:, None]).astype(F32)
    ang = pos * inv_freq[e % quarter][None, :]
    cos, sin = jnp.cos(ang), jnp.sin(ang)
    second = ((e % half) >= quarter)[None, :]
    return jnp.stack([cos, jnp.where(second, sin, 0.0), jnp.where(second, 0.0, -sin)])


def _pick(n, pref):
    t = min(pref, n)
    while n % t:
        t //= 2
    return t


def kernel(x, c, ctx, c_ctx, w_mod, b_mod, norm_attn_w, w_in, q_norm_w, k_norm_w, lambda_q1, lambda_k1,
           lambda_q2, lambda_k2, subln_w, pool_w, pool_scale, w_a_up, w_b_up, w_o, norm_mlp_w, w_ff1, w_ff2):
    B, L, D = x.shape
    Lc = ctx.shape[1]
    depth = w_mod.shape[0]
    qk_w = 2 * ATTN_HEADS * HEAD_DIM
    tn = qk_w
    assert depth == 1 and w_in.shape[2] == 4 * tn + 2 * D and L % GRID_W == 0
    assert 2 * D == 2 * tn * 2 and pool_w.shape[1] * pool_w.shape[2] == tn

    for l in range(depth):
        lam_init = 0.8 - 0.6 * math.exp(-0.3 * l)

        rows = 8
        cvec = jnp.zeros((rows, D), F32).at[:B].set(c).at[B].set(c_ctx)
        mod = _modulation(cvec, w_mod[l], b_mod[l][None, :])
        sa, ca, ga, sm, cm, gm = [mod[:B, None, k * D:(k + 1) * D] for k in range(6)]
        sa_c, ca_c = [jnp.broadcast_to(mod[B, k * D:(k + 1) * D], (B, 1, D)) for k in range(2)]

        w_in_bf = w_in[l].astype(BF16)
        qw128 = jnp.tile(q_norm_w[l], 2) * (HEAD_DIM ** -0.5 * math.log2(math.e))
        kw128 = jnp.tile(k_norm_w[l], 2)
        qk_w128 = jnp.stack([qw128, kw128])
        nw = norm_attn_w[l][None, :]

        seg_w = tn // 4
        seg = jnp.kron(jnp.eye(seg_w // HEAD_DIM, dtype=F32),
                       jnp.full((HEAD_DIM, HEAD_DIM), 1.0 / HEAD_DIM)).astype(BF16)
        pc = _input_projection(ctx, nw, sa_c, ca_c, w_in_bf, 1, ("k", "v"), seg, qk_w128,
                               _rope_tables(Lc, rope=False), tm=_pick(Lc, 256), tn=tn)
        p = _input_projection(x, nw, sa, ca, w_in_bf, 0, ("q", "k", "v", "u", "g", "g", "g", "g"), seg,
                              qk_w128, _rope_tables(L, rope=True), tm=_pick(L, 1024), tn=tn)

        lams = [v[l][None, :] for v in (lambda_q1, lambda_k1, lambda_q2, lambda_k2)]
        score_bound = (HEAD_DIM * jnp.max(jnp.abs(qw128)) * jnp.max(jnp.abs(kw128))) * BF16_SLACK
        fast = (score_bound <= SCORE_BOUND).astype(jnp.int32).reshape(1)
        per_block = tn // VALUE_DIM
        col0 = {"q": 0, "k": per_block, "v": 2 * per_block, "kc": 0, "vc": per_block}
        heads = _diff_attention(fast, p, pc, col0, lams, subln_w[l][:, None], lam_init,
                                tq=_pick(L, 512), tk=_pick(L, 1024), subs=_pick(L // _pick(L, 512), 4))

        x = _token_mixer(heads, p, 3, 1, x, ga, w_a_up[l].astype(BF16), w_b_up[l].astype(BF16),
                         pool_w[l].astype(BF16), pool_scale[l][None, :], w_o[l].astype(BF16),
                         tm=_pick(L, 256))
        x = _mlp(x, norm_mlp_w[l][None, :], sm, cm, gm, w_ff1[l].astype(BF16), w_ff2[l].astype(BF16),
                 tm=_pick(L, 1024), tf=1024)
    return x
```

```python
import functools
import math

import jax
import jax.numpy as jnp
from jax import lax
from jax.experimental import pallas as pl
from jax.experimental.pallas import tpu as pltpu

F32 = jnp.float32
BF16 = jnp.bfloat16

ATTN_HEADS = 8
HEAD_DIM = 64
VALUE_DIM = 2 * HEAD_DIM
SCORE_BOUND = 64.0
BF16_SLACK = 1.0 + 2.0 ** -6
GRID_W = 64
ROPE_THETA = 10000.0
POOL_WINDOWS = (2, 4, 8, 16)
POOL_HALO = 16
MLP_SUB = 512
EPS = 1e-6

V7X_LANES = 128
V7X_VMEM_BYTES = 64 * 1024 * 1024
V7X_VMEM_CAP = V7X_VMEM_BYTES - 6 * 1024 * 1024


def _vmem_limit(block_bytes):
    return int(min(V7X_VMEM_CAP, block_bytes * 1.25 + 8 * 1024 * 1024))


def _nbytes(shape, dtype):
    return math.prod(shape) * jnp.dtype(dtype).itemsize


def _mod_kernel(c_ref, w_ref, b_ref, o_ref):
    cv = c_ref[...]
    act = cv * jax.nn.sigmoid(cv)
    o_ref[...] = jnp.dot(act, w_ref[...], preferred_element_type=F32) + b_ref[...]


def _modulation(cvec, w_mod, b_mod, tn=1024):
    rows, d = cvec.shape
    n = w_mod.shape[1]
    blocks = 2 * (_nbytes((d, tn), F32) + _nbytes((rows, tn), F32) * 2) + _nbytes((rows, d), F32)
    return pl.pallas_call(
        _mod_kernel,
        grid=(n // tn,),
        in_specs=[
            pl.BlockSpec((rows, d), lambda j: (0, 0)),
            pl.BlockSpec((d, tn), lambda j: (0, j)),
            pl.BlockSpec((1, tn), lambda j: (0, j)),
        ],
        out_specs=pl.BlockSpec((rows, tn), lambda j: (0, j)),
        out_shape=jax.ShapeDtypeStruct((rows, n), F32),
        compiler_params=pltpu.CompilerParams(
            dimension_semantics=("parallel",), vmem_limit_bytes=_vmem_limit(blocks)),
        name="modulation",
    )(cvec, w_mod, b_mod)


def _modulated_norm(x, norm_w, shift, scale):
    y = x * lax.rsqrt(jnp.mean(x * x, axis=-1, keepdims=True) + EPS)
    return (y * norm_w) * (1.0 + scale) + shift


def _inproj_kernel(x_ref, nw_ref, shift_ref, scale_ref, w_ref, seg_ref, qkw_ref, tab_ref, o_ref, h_ref,
                   *, kinds):
    j = pl.program_id(2)
    tm, tn = o_ref.shape[1], o_ref.shape[2]
    seg_w = seg_ref.shape[0]

    def qk_epilogue(which, first):
        quarter = HEAD_DIM // 4
        chunk = tm // 4
        w128 = qkw_ref[which:which + 1, :]
        for r0 in range(0, tm, chunk):
            rows = pl.ds(r0, chunk)
            if first:
                h = _modulated_norm(x_ref[0, rows, :], nw_ref[...], shift_ref[0], scale_ref[0])
                h_ref[rows, :] = h.astype(BF16)
            cos, sin_prev, sin_next = (tab_ref[i, rows, :] for i in range(3))
            acc = jnp.dot(h_ref[rows, :], w_ref[...], preferred_element_type=F32)
            for s0 in range(0, tn, seg_w):
                a = acc[:, s0:s0 + seg_w]
                ms = jnp.dot((a * a).astype(BF16), seg_ref[...], preferred_element_type=F32)
                y = a * lax.rsqrt(ms + EPS)
                for c in range(0, seg_w, V7X_LANES):
                    yc = y[:, c:c + V7X_LANES] * w128
                    out = (yc * cos + pltpu.roll(yc, quarter, 1) * sin_prev
                           + pltpu.roll(yc, V7X_LANES - quarter, 1) * sin_next)
                    o_ref[0, rows, s0 + c:s0 + c + V7X_LANES] = out.astype(BF16)

    def project():
        return jnp.dot(h_ref[...], w_ref[...], preferred_element_type=F32)

    def in_range(name):
        j0, n = kinds.index(name), kinds.count(name)
        return (j >= j0) & (j < j0 + n)

    assert kinds[0] in ("q", "k") and kinds.count(kinds[0]) == 1
    for name in dict.fromkeys(kinds):
        if name in ("q", "k"):
            pl.when(in_range(name))(functools.partial(qk_epilogue, ("q", "k").index(name), name == kinds[0]))
        elif name == "g":
            @pl.when(in_range(name))
            def _():
                o_ref[0] = (0.5 * jnp.tanh(0.5 * project()) + 0.5).astype(BF16)
        else:
            @pl.when(in_range(name))
            def _():
                o_ref[0] = project().astype(BF16)


def _input_projection(x, norm_w, shift, scale, w_in_bf, col_block0, kinds, seg_mean, qk_w, tables, tm, tn):
    B, L, D = x.shape
    vec = lambda: pl.BlockSpec((1, 1, D), lambda i, b, j: (b, 0, 0))
    blocks = (2 * (_nbytes((tm, D), F32) + _nbytes((D, tn), BF16) + _nbytes((3, tm, V7X_LANES), F32)
                   + _nbytes((tm, tn), BF16))
              + _nbytes((tm, D), BF16) + 2 * _nbytes((tm, tn), F32))
    return pl.pallas_call(
        functools.partial(_inproj_kernel, kinds=tuple(kinds)),
        grid=(L // tm, B, len(kinds)),
        in_specs=[
            pl.BlockSpec((1, tm, D), lambda i, b, j: (b, i, 0)),
            pl.BlockSpec((1, D), lambda i, b, j: (0, 0)),
            vec(), vec(),
            pl.BlockSpec((D, tn), lambda i, b, j: (0, col_block0 + j)),
            pl.BlockSpec(seg_mean.shape, lambda i, b, j: (0, 0)),
            pl.BlockSpec(qk_w.shape, lambda i, b, j: (0, 0)),
            pl.BlockSpec((3, tm, V7X_LANES), lambda i, b, j: (0, i, 0)),
        ],
        out_specs=pl.BlockSpec((1, tm, tn), lambda i, b, j: (b, i, j)),
        out_shape=jax.ShapeDtypeStruct((B, L, tn * len(kinds)), BF16),
        scratch_shapes=[pltpu.VMEM((tm, D), BF16)],
        compiler_params=pltpu.CompilerParams(
            dimension_semantics=("parallel", "parallel", "arbitrary"),
            vmem_limit_bytes=_vmem_limit(blocks)),
        name="input_projection",
    )(x, norm_w, shift, scale, w_in_bf, seg_mean, qk_w, tables)


def _attn_kernel(fast_ref, lq1_ref, lk1_ref, lq2_ref, lk2_ref, q_ref, k_ref, v_ref, kc_ref, vc_ref,
                 sw_ref, o_ref, rhs_ref, acc_ref, *, tq, tk, lam_init):
    subs = q_ref.shape[1] // tq
    lane = lax.broadcasted_iota(jnp.int32, (tq, VALUE_DIM), 1)
    for s in range(subs):
        q = q_ref[0, pl.ds(s * tq, tq), :]
        zero = jnp.zeros_like(q)
        rhs_ref[s, :tq, :] = jnp.where(lane < HEAD_DIM, q, zero)
        rhs_ref[s, tq:, :] = jnp.where(lane >= HEAD_DIM, q, zero)

    def scores(s, kt):
        return lax.dot_general(kt, rhs_ref[s], (((1,), (1,)), ((), ())), preferred_element_type=F32)

    def values_t_dot(v, p):
        return lax.dot_general(v, p, (((0,), (0,)), ((), ())), preferred_element_type=F32)

    lam = (jnp.exp(jnp.sum(lq1_ref[...] * lk1_ref[...])) - jnp.exp(jnp.sum(lq2_ref[...] * lk2_ref[...]))
           + lam_init)

    def finish(s, acc, l):
        o = acc * (1.0 / l)
        od = o[:, :tq] - lam * o[:, tq:]
        y = od * lax.rsqrt(jnp.mean(od * od, axis=0, keepdims=True) + EPS) * sw_ref[...]
        o_ref[0, pl.ds(s * tq, tq), :] = (y * (1.0 - lam_init)).T.astype(BF16)

    n_tiles = k_ref.shape[1] // tk

    @pl.when(fast_ref[0] == 1)
    def _():
        tiles = [(k_ref.at[0, pl.ds(t * tk, tk), :], v_ref.at[0, pl.ds(t * tk, tk), :])
                 for t in range(n_tiles)] + [(kc_ref.at[0], vc_ref.at[0])]

        def run(s, some_tiles, state):
            acc, lsum = state
            for kt_ref, vtt_ref in some_tiles:
                p = jnp.exp2(scores(s, kt_ref[...]))
                part = jnp.sum(p.reshape(p.shape[0] // 8, 8, p.shape[1]), axis=0)
                pv = values_t_dot(vtt_ref[...], p.astype(BF16))
                acc = pv if acc is None else acc + pv
                lsum = part if lsum is None else lsum + part
            return acc, lsum

        state = run(0, tiles, (None, None))
        for s in range(1, subs):
            started = run(s, tiles[:1], (None, None))
            finish(s - 1, state[0], jnp.sum(state[1], axis=0, keepdims=True))
            state = run(s, tiles[1:], started)
        finish(subs - 1, state[0], jnp.sum(state[1], axis=0, keepdims=True))

    @pl.when(fast_ref[0] != 1)
    def _():
        for s in range(subs):
            def update(carry, kt, vtt):
                m_prev, l_prev = carry
                sc = scores(s, kt)
                m_new = jnp.maximum(m_prev, jnp.max(sc, axis=0, keepdims=True))
                alpha = jnp.exp2(m_prev - m_new)
                p = jnp.exp2(sc - m_new)
                acc_ref[...] = alpha * acc_ref[...] + values_t_dot(vtt, p.astype(BF16))
                return m_new, alpha * l_prev + jnp.sum(p, axis=0, keepdims=True)

            def body(t, carry):
                off = pl.multiple_of(t * tk, tk)
                return update(carry, k_ref[0, pl.ds(off, tk), :], v_ref[0, pl.ds(off, tk), :])

            acc_ref[...] = jnp.zeros_like(acc_ref)
            init = (jnp.full((1, 2 * tq), -jnp.inf, F32), jnp.zeros((1, 2 * tq), F32))
            carry = lax.fori_loop(0, n_tiles, body, init)
            _, l = update(carry, kc_ref[0], vc_ref[0])
            finish(s, acc_ref[...], l)


def _diff_attention(fast, p, p_c, col0, lams, subln_w, lam_init, tq, tk, subs):
    B, L, _ = p.shape
    Lc = p_c.shape[1]
    H = ATTN_HEADS
    assert L % tk == 0
    lam_spec = pl.BlockSpec((1, HEAD_DIM), lambda b, h, i: (0, 0))
    keys = lambda rows, c0: pl.BlockSpec((1, rows, VALUE_DIM), lambda b, h, i: (b, 0, c0 + h))
    blocks = (2 * (2 * _nbytes((tq, VALUE_DIM), BF16) + 2 * _nbytes((L, VALUE_DIM), BF16)
                   + 2 * _nbytes((Lc, VALUE_DIM), BF16))
              + 3 * _nbytes((2 * tq, VALUE_DIM), F32) + 4 * _nbytes((2 * tq, tk), F32))
    return pl.pallas_call(
        functools.partial(_attn_kernel, tq=tq, tk=tk, lam_init=lam_init),
        grid=(B, H, L // (subs * tq)),
        in_specs=[
            pl.BlockSpec(memory_space=pltpu.SMEM),
            lam_spec, lam_spec, lam_spec, lam_spec,
            pl.BlockSpec((1, subs * tq, VALUE_DIM), lambda b, h, i: (b, i, col0["q"] + h)),
            keys(L, col0["k"]), keys(L, col0["v"]), keys(Lc, col0["kc"]), keys(Lc, col0["vc"]),
            pl.BlockSpec((VALUE_DIM, 1), lambda b, h, i: (0, 0)),
        ],
        out_specs=pl.BlockSpec((1, subs * tq, VALUE_DIM), lambda b, h, i: (b, i, h)),
        out_shape=jax.ShapeDtypeStruct((B, L, H * VALUE_DIM), BF16),
        scratch_shapes=[pltpu.VMEM((subs, 2 * tq, VALUE_DIM), BF16),
                        pltpu.VMEM((VALUE_DIM, 2 * tq), F32)],
        compiler_params=pltpu.CompilerParams(
            dimension_semantics=("parallel", "parallel", "parallel"),
            vmem_limit_bytes=_vmem_limit(blocks)),
        name="diff_attention",
    )(fast, *lams, p, p, p, p_c, p_c, subln_w)


def _mixer_kernel(heads_ref, u_ref, uprev_ref, unext_ref, g_ref, x_ref, gate_ref,
                  wa_ref, wb_ref, pw_ref, ps_ref, wo_ref, o_ref, ubuf_ref, y_ref, ya_ref, *, tm, seq_len):
    i = pl.program_id(1)
    nt = pl.num_programs(1)
    D = x_ref.shape[2]
    group = u_ref.shape[2] // len(POOL_WINDOWS)

    ubuf_ref[pl.ds(POOL_HALO, tm), :] = u_ref[0].astype(F32)
    ubuf_ref[pl.ds(0, POOL_HALO), :] = jnp.where(i > 0, uprev_ref[0].astype(F32), 0.0)
    ubuf_ref[pl.ds(POOL_HALO + tm, POOL_HALO), :] = jnp.where(i < nt - 1, unext_ref[0].astype(F32), 0.0)

    t = i * tm + lax.broadcasted_iota(jnp.int32, (tm, 1), 0)
    ya_cols = D // len(POOL_WINDOWS)
    for gi, w in enumerate(POOL_WINDOWS):
        a_cols = slice(gi * ya_cols, (gi + 1) * ya_cols)
        ya_ref[:, a_cols] = jnp.dot(heads_ref[0], wa_ref[:, a_cols], preferred_element_type=F32)

        cols = slice(gi * group, (gi + 1) * group)
        total = None
        for off in range(-(w // 2), w - w // 2):
            part = ubuf_ref[pl.ds(POOL_HALO + off, tm), cols]
            total = part if total is None else total + part
        lo = jnp.maximum(t - w // 2, 0)
        hi = jnp.minimum(t + (w - w // 2), seq_len)
        d = total * (1.0 / (hi - lo).astype(F32)) - ubuf_ref[pl.ds(POOL_HALO, tm), cols]
        yg = jnp.dot(d.astype(BF16), pw_ref[gi], preferred_element_type=F32)
        y_ref[:, cols] = (yg * ps_ref[:, cols]).astype(BF16)

    y_b = jnp.dot(y_ref[...], wb_ref[...], preferred_element_type=F32)
    mixed = g_ref[0, :, :D].astype(F32) * ya_ref[...] + g_ref[0, :, D:].astype(F32) * y_b
    mix = jnp.dot(mixed.astype(BF16), wo_ref[...], preferred_element_type=F32)
    o_ref[0] = x_ref[0] + gate_ref[0] * mix


def _token_mixer(heads, p_all, u_block, g_block, x, gate, w_a_up, w_b_up, pool_w, pool_scale, w_o, tm):
    B, L, D = x.shape
    AW, PW = heads.shape[2], w_b_up.shape[0]
    hb = tm // POOL_HALO
    n_halo = L // POOL_HALO
    const = lambda shape: pl.BlockSpec(shape, lambda b, i: (0,) * len(shape),
                                       pipeline_mode=pl.Buffered(1))
    weights = (_nbytes((AW, D), BF16) + _nbytes((PW, D), BF16) + _nbytes(pool_w.shape, BF16)
               + _nbytes((D, D), BF16))
    blocks = (weights + 2 * (_nbytes((tm, AW), BF16) + _nbytes((tm, PW), BF16)
                             + _nbytes((tm, 2 * D), BF16) + 2 * _nbytes((tm, D), F32))
              + _nbytes((tm + 2 * POOL_HALO, PW), F32) + _nbytes((tm, PW), BF16)
              + 4 * _nbytes((tm, D), F32))
    return pl.pallas_call(
        functools.partial(_mixer_kernel, tm=tm, seq_len=L),
        grid=(B, L // tm),
        in_specs=[
            pl.BlockSpec((1, tm, AW), lambda b, i: (b, i, 0)),
            pl.BlockSpec((1, tm, PW), lambda b, i: (b, i, u_block)),
            pl.BlockSpec((1, POOL_HALO, PW), lambda b, i: (b, jnp.maximum(i * hb - 1, 0), u_block)),
            pl.BlockSpec((1, POOL_HALO, PW), lambda b, i: (b, jnp.minimum((i + 1) * hb, n_halo - 1), u_block)),
            pl.BlockSpec((1, tm, 2 * D), lambda b, i: (b, i, g_block)),
            pl.BlockSpec((1, tm, D), lambda b, i: (b, i, 0)),
            pl.BlockSpec((1, 1, D), lambda b, i: (b, 0, 0)),
            const((AW, D)), const((PW, D)), const(pool_w.shape), const((1, PW)), const((D, D)),
        ],
        out_specs=pl.BlockSpec((1, tm, D), lambda b, i: (b, i, 0)),
        out_shape=jax.ShapeDtypeStruct((B, L, D), F32),
        scratch_shapes=[pltpu.VMEM((tm + 2 * POOL_HALO, PW), F32), pltpu.VMEM((tm, PW), BF16),
                        pltpu.VMEM((tm, D), F32)],
        compiler_params=pltpu.CompilerParams(
            dimension_semantics=("parallel", "parallel"),
            vmem_limit_bytes=_vmem_limit(blocks)),
        name="token_mixer",
    )(heads, p_all, p_all, p_all, p_all, x, gate, w_a_up, w_b_up, pool_w, pool_scale, w_o)


def _mlp_kernel(x_ref, nw_ref, shift_ref, scale_ref, gate_ref, w1_ref, w2_ref, o_ref, h_ref):
    j = pl.program_id(2)
    tm, tf = h_ref.shape[0], w1_ref.shape[1]

    sub = min(tf, MLP_SUB)

    def ffn(h, rows, assign_first=False, finish=False):
        for s0 in range(0, tf, sub):
            a = jnp.maximum(jnp.dot(h, w1_ref[:, s0:s0 + sub], preferred_element_type=F32), 0.0)
            a2 = (a * a).astype(BF16)
            for n in range(0, o_ref.shape[2], sub):
                cols = slice(n, n + sub)
                part = jnp.dot(a2, w2_ref[s0:s0 + sub, cols], preferred_element_type=F32)
                if assign_first and s0 == 0:
                    o_ref[0, rows, cols] = part
                elif finish and s0 + sub == tf:
                    o_ref[0, rows, cols] = (x_ref[0, rows, cols]
                                            + gate_ref[0, :, cols] * (o_ref[0, rows, cols] + part))
                else:
                    o_ref[0, rows, cols] += part

    @pl.when(j == 0)
    def _():
        chunk = tm // 4
        for r0 in range(0, tm, chunk):
            rows = pl.ds(r0, chunk)
            h = _modulated_norm(x_ref[0, rows, :], nw_ref[...], shift_ref[0], scale_ref[0]).astype(BF16)
            h_ref[rows, :] = h
            ffn(h, rows, assign_first=True)

    last = pl.num_programs(2) - 1

    @pl.when((j > 0) & (j < last))
    def _():
        ffn(h_ref[...], pl.ds(0, tm))

    @pl.when(j == last)
    def _():
        ffn(h_ref[...], pl.ds(0, tm), finish=True)


def _mlp(x, norm_w, shift, scale, gate, w1, w2, tm, tf):
    B, L, D = x.shape
    F = w1.shape[1]
    assert F // tf >= 2
    vec = lambda: pl.BlockSpec((1, 1, D), lambda b, i, j: (b, 0, 0))
    blocks = (2 * (2 * _nbytes((tm, D), F32) + _nbytes((D, tf), BF16) + _nbytes((tf, D), BF16))
              + _nbytes((tm, D), BF16) + 3 * _nbytes((tm, min(tf, MLP_SUB)), F32))
    return pl.pallas_call(
        _mlp_kernel,
        grid=(B, L // tm, F // tf),
        in_specs=[
            pl.BlockSpec((1, tm, D), lambda b, i, j: (b, i, 0)),
            pl.BlockSpec((1, D), lambda b, i, j: (0, 0)),
            vec(), vec(), vec(),
            pl.BlockSpec((D, tf), lambda b, i, j: (0, j)),
            pl.BlockSpec((tf, D), lambda b, i, j: (j, 0)),
        ],
        out_specs=pl.BlockSpec((1, tm, D), lambda b, i, j: (b, i, 0)),
        out_shape=jax.ShapeDtypeStruct((B, L, D), F32),
        scratch_shapes=[pltpu.VMEM((tm, D), BF16)],
        compiler_params=pltpu.CompilerParams(
            dimension_semantics=("parallel", "parallel", "arbitrary"),
            vmem_limit_bytes=_vmem_limit(blocks)),
        name="sq_relu_mlp",
    )(x, norm_w, shift, scale, gate, w1, w2)


def _rope_tables(seq_len, rope):
    if not rope:
        ones = jnp.ones((seq_len, V7X_LANES), F32)
        return jnp.stack([ones, jnp.zeros_like(ones), jnp.zeros_like(ones)])
    t = jnp.arange(seq_len)
    row, col = t // GRID_W, t % GRID_W
    half = HEAD_DIM // 2
    quarter = half // 2
    inv_freq = ROPE_THETA ** (-jnp.arange(0, half, 2, dtype=F32) / half)
    lane = jnp.arange(V7X_LANES)
    e = lane % HEAD_DIM
    pos = jnp.where((e // half)[None, :] == 0, row[:, None], col[:, None]).astype(F32)
    ang = pos * inv_freq[e % quarter][None, :]
    cos, sin = jnp.cos(ang), jnp.sin(ang)
    second = ((e % half) >= quarter)[None, :]
    return jnp.stack([cos, jnp.where(second, sin, 0.0), jnp.where(second, 0.0, -sin)])


def _pick(n, pref):
    t = min(pref, n)
    while n % t:
        t //= 2
    return t


def kernel(x, c, ctx, c_ctx, w_mod, b_mod, norm_attn_w, w_in, q_norm_w, k_norm_w, lambda_q1, lambda_k1,
           lambda_q2, lambda_k2, subln_w, pool_w, pool_scale, w_a_up, w_b_up, w_o, norm_mlp_w, w_ff1, w_ff2):
    B, L, D = x.shape
    Lc = ctx.shape[1]
    depth = w_mod.shape[0]
    qk_w = 2 * ATTN_HEADS * HEAD_DIM
    tn = qk_w
    assert depth == 1 and w_in.shape[2] == 4 * tn + 2 * D and L % GRID_W == 0
    assert 2 * D == 2 * tn * 2 and pool_w.shape[1] * pool_w.shape[2] == tn

    for l in range(depth):
        lam_init = 0.8 - 0.6 * math.exp(-0.3 * l)

        rows = 8
        cvec = jnp.zeros((rows, D), F32).at[:B].set(c).at[B].set(c_ctx)
        mod = _modulation(cvec, w_mod[l], b_mod[l][None, :])
        sa, ca, ga, sm, cm, gm = [mod[:B, None, k * D:(k + 1) * D] for k in range(6)]
        sa_c, ca_c = [jnp.broadcast_to(mod[B, k * D:(k + 1) * D], (B, 1, D)) for k in range(2)]

        w_in_bf = w_in[l].astype(BF16)
        qw128 = jnp.tile(q_norm_w[l], 2) * (HEAD_DIM ** -0.5 * math.log2(math.e))
        kw128 = jnp.tile(k_norm_w[l], 2)
        qk_w128 = jnp.stack([qw128, kw128])
        nw = norm_attn_w[l][None, :]

        seg_w = tn // 4
        seg = jnp.kron(jnp.eye(seg_w // HEAD_DIM, dtype=F32),
                       jnp.full((HEAD_DIM, HEAD_DIM), 1.0 / HEAD_DIM)).astype(BF16)
        pc = _input_projection(ctx, nw, sa_c, ca_c, w_in_bf, 1, ("k", "v"), seg, qk_w128,
                               _rope_tables(Lc, rope=False), tm=_pick(Lc, 256), tn=tn)
        p = _input_projection(x, nw, sa, ca, w_in_bf, 0, ("q", "k", "v", "u", "g", "g", "g", "g"), seg,
                              qk_w128, _rope_tables(L, rope=True), tm=_pick(L, 1024), tn=tn)

        lams = [v[l][None, :] for v in (lambda_q1, lambda_k1, lambda_q2, lambda_k2)]
        score_bound = (HEAD_DIM * jnp.max(jnp.abs(qw128)) * jnp.max(jnp.abs(kw128))) * BF16_SLACK
        fast = (score_bound <= SCORE_BOUND).astype(jnp.int32).reshape(1)
        per_block = tn // VALUE_DIM
        col0 = {"q": 0, "k": per_block, "v": 2 * per_block, "kc": 0, "vc": per_block}
        heads = _diff_attention(fast, p, pc, col0, lams, subln_w[l][:, None], lam_init,
                                tq=_pick(L, 512), tk=_pick(L, 1024), subs=_pick(L // _pick(L, 512), 4))

        x = _token_mixer(heads, p, 3, 1, x, ga, w_a_up[l].astype(BF16), w_b_up[l].astype(BF16),
                         pool_w[l].astype(BF16), pool_scale[l][None, :], w_o[l].astype(BF16),
                         tm=_pick(L, 256))
        x = _mlp(x, norm_mlp_w[l][None, :], sm, cm, gm, w_ff1[l].astype(BF16), w_ff2[l].astype(BF16),
                 tm=_pick(L, 1024), tf=1024)
    return x
```

```python
import functools
import math

import jax
import jax.numpy as jnp
from jax import lax
from jax.experimental import pallas as pl
from jax.experimental.pallas import tpu as pltpu

F32 = jnp.float32
BF16 = jnp.bfloat16

ATTN_HEADS = 8
HEAD_DIM = 64
VALUE_DIM = 2 * HEAD_DIM
SCORE_BOUND = 64.0
BF16_SLACK = 1.0 + 2.0 ** -6
GRID_W = 64
ROPE_THETA = 10000.0
POOL_WINDOWS = (2, 4, 8, 16)
POOL_HALO = 16
MLP_SUB = 512
EPS = 1e-6

V7X_LANES = 128
V7X_VMEM_BYTES = 64 * 1024 * 1024
V7X_VMEM_CAP = V7X_VMEM_BYTES - 6 * 1024 * 1024


def _vmem_limit(block_bytes):
    return int(min(V7X_VMEM_CAP, block_bytes * 1.25 + 8 * 1024 * 1024))


def _nbytes(shape, dtype):
    return math.prod(shape) * jnp.dtype(dtype).itemsize


def _mod_kernel(c_ref, w_ref, b_ref, o_ref):
    cv = c_ref[...]
    act = cv * jax.nn.sigmoid(cv)
    o_ref[...] = jnp.dot(act, w_ref[...], preferred_element_type=F32) + b_ref[...]


def _modulation(cvec, w_mod, b_mod, tn=1024):
    rows, d = cvec.shape
    n = w_mod.shape[1]
    blocks = 2 * (_nbytes((d, tn), F32) + _nbytes((rows, tn), F32) * 2) + _nbytes((rows, d), F32)
    return pl.pallas_call(
        _mod_kernel,
        grid=(n // tn,),
        in_specs=[
            pl.BlockSpec((rows, d), lambda j: (0, 0)),
            pl.BlockSpec((d, tn), lambda j: (0, j)),
            pl.BlockSpec((1, tn), lambda j: (0, j)),
        ],
        out_specs=pl.BlockSpec((rows, tn), lambda j: (0, j)),
        out_shape=jax.ShapeDtypeStruct((rows, n), F32),
        compiler_params=pltpu.CompilerParams(
            dimension_semantics=("parallel",), vmem_limit_bytes=_vmem_limit(blocks)),
        name="modulation",
    )(cvec, w_mod, b_mod)


def _modulated_norm(x, norm_w, shift, scale):
    y = x * lax.rsqrt(jnp.mean(x * x, axis=-1, keepdims=True) + EPS)
    return (y * norm_w) * (1.0 + scale) + shift


def _inproj_kernel(x_ref, nw_ref, shift_ref, scale_ref, w_ref, seg_ref, qkw_ref, tab_ref, o_ref, h_ref,
                   *, kinds):
    j = pl.program_id(2)
    tm, tn = o_ref.shape[1], o_ref.shape[2]
    seg_w = seg_ref.shape[0]

    def qk_epilogue(which, first):
        quarter = HEAD_DIM // 4
        chunk = tm // 4
        w128 = qkw_ref[which:which + 1, :]
        for r0 in range(0, tm, chunk):
            rows = pl.ds(r0, chunk)
            if first:
                h = _modulated_norm(x_ref[0, rows, :], nw_ref[...], shift_ref[0], scale_ref[0])
                h_ref[rows, :] = h.astype(BF16)
            cos, sin_prev, sin_next = (tab_ref[i, rows, :] for i in range(3))
            acc = jnp.dot(h_ref[rows, :], w_ref[...], preferred_element_type=F32)
            for s0 in range(0, tn, seg_w):
                a = acc[:, s0:s0 + seg_w]
                ms = jnp.dot((a * a).astype(BF16), seg_ref[...], preferred_element_type=F32)
                y = a * lax.rsqrt(ms + EPS)
                for c in range(0, seg_w, V7X_LANES):
                    yc = y[:, c:c + V7X_LANES] * w128
                    out = (yc * cos + pltpu.roll(yc, quarter, 1) * sin_prev
                           + pltpu.roll(yc, V7X_LANES - quarter, 1) * sin_next)
                    o_ref[0, rows, s0 + c:s0 + c + V7X_LANES] = out.astype(BF16)

    def project():
        return jnp.dot(h_ref[...], w_ref[...], preferred_element_type=F32)

    def in_range(name):
        j0, n = kinds.index(name), kinds.count(name)
        return (j >= j0) & (j < j0 + n)

    assert kinds[0] in ("q", "k") and kinds.count(kinds[0]) == 1
    for name in dict.fromkeys(kinds):
        if name in ("q", "k"):
            pl.when(in_range(name))(functools.partial(qk_epilogue, ("q", "k").index(name), name == kinds[0]))
        elif name == "g":
            @pl.when(in_range(name))
            def _():
                o_ref[0] = (0.5 * jnp.tanh(0.5 * project()) + 0.5).astype(BF16)
        else:
            @pl.when(in_range(name))
            def _():
                o_ref[0] = project().astype(BF16)


def _input_projection(x, norm_w, shift, scale, w_in_bf, col_block0, kinds, seg_mean, qk_w, tables, tm, tn):
    B, L, D = x.shape
    vec = lambda: pl.BlockSpec((1, 1, D), lambda i, b, j: (b, 0, 0))
    blocks = (2 * (_nbytes((tm, D), F32) + _nbytes((D, tn), BF16) + _nbytes((3, tm, V7X_LANES), F32)
                   + _nbytes((tm, tn), BF16))
              + _nbytes((tm, D), BF16) + 2 * _nbytes((tm, tn), F32))
    return pl.pallas_call(
        functools.partial(_inproj_kernel, kinds=tuple(kinds)),
        grid=(L // tm, B, len(kinds)),
        in_specs=[
            pl.BlockSpec((1, tm, D), lambda i, b, j: (b, i, 0)),
            pl.BlockSpec((1, D), lambda i, b, j: (0, 0)),
            vec(), vec(),
            pl.BlockSpec((D, tn), lambda i, b, j: (0, col_block0 + j)),
            pl.BlockSpec(seg_mean.shape, lambda i, b, j: (0, 0)),
            pl.BlockSpec(qk_w.shape, lambda i, b, j: (0, 0)),
            pl.BlockSpec((3, tm, V7X_LANES), lambda i, b, j: (0, i, 0)),
        ],
        out_specs=pl.BlockSpec((1, tm, tn), lambda i, b, j: (b, i, j)),
        out_shape=jax.ShapeDtypeStruct((B, L, tn * len(kinds)), BF16),
        scratch_shapes=[pltpu.VMEM((tm, D), BF16)],
        compiler_params=pltpu.CompilerParams(
            dimension_semantics=("parallel", "parallel", "arbitrary"),
            vmem_limit_bytes=_vmem_limit(blocks)),
        name="input_projection",
    )(x, norm_w, shift, scale, w_in_bf, seg_mean, qk_w, tables)


def _attn_kernel(fast_ref, lq1_ref, lk1_ref, lq2_ref, lk2_ref, q_ref, k_ref, v_ref, kc_ref, vc_ref,
                 sw_ref, o_ref, rhs_ref, acc_ref, *, tq, tk, lam_init):
    subs = q_ref.shape[1] // tq
    lane = lax.broadcasted_iota(jnp.int32, (tq, VALUE_DIM), 1)
    for s in range(subs):
        q = q_ref[0, pl.ds(s * tq, tq), :]
        zero = jnp.zeros_like(q)
        rhs_ref[s, :tq, :] = jnp.where(lane < HEAD_DIM, q, zero)
        rhs_ref[s, tq:, :] = jnp.where(lane >= HEAD_DIM, q, zero)

    def scores(s, kt):
        return lax.dot_general(kt, rhs_ref[s], (((1,), (1,)), ((), ())), preferred_element_type=F32)

    def values_t_dot(v, p):
        return lax.dot_general(v, p, (((0,), (0,)), ((), ())), preferred_element_type=F32)

    lam = (jnp.exp(jnp.sum(lq1_ref[...] * lk1_ref[...])) - jnp.exp(jnp.sum(lq2_ref[...] * lk2_ref[...]))
           + lam_init)

    def finish(s, acc, l):
        o = acc * (1.0 / l)
        od = o[:, :tq] - lam * o[:, tq:]
        y = od * lax.rsqrt(jnp.mean(od * od, axis=0, keepdims=True) + EPS) * sw_ref[...]
        o_ref[0, pl.ds(s * tq, tq), :] = (y * (1.0 - lam_init)).T.astype(BF16)

    n_tiles = k_ref.shape[1] // tk

    @pl.when(fast_ref[0] == 1)
    def _():
        tiles = [(k_ref.at[0, pl.ds(t * tk, tk), :], v_ref.at[0, pl.ds(t * tk, tk), :])
                 for t in range(n_tiles)] + [(kc_ref.at[0], vc_ref.at[0])]

        def run(s, some_tiles, state):
            acc, lsum = state
            for kt_ref, vtt_ref in some_tiles:
                p = jnp.exp2(scores(s, kt_ref[...]))
                part = jnp.sum(p.reshape(p.shape[0] // 8, 8, p.shape[1]), axis=0)
                pv = values_t_dot(vtt_ref[...], p.astype(BF16))
                acc = pv if acc is None else acc + pv
                lsum = part if lsum is None else lsum + part
            return acc, lsum

        state = run(0, tiles, (None, None))
        for s in range(1, subs):
            started = run(s, tiles[:1], (None, None))
            finish(s - 1, state[0], jnp.sum(state[1], axis=0, keepdims=True))
            state = run(s, tiles[1:], started)
        finish(subs - 1, state[0], jnp.sum(state[1], axis=0, keepdims=True))

    @pl.when(fast_ref[0] != 1)
    def _():
        for s in range(subs):
            def update(carry, kt, vtt):
                m_prev, l_prev = carry
                sc = scores(s, kt)
                m_new = jnp.maximum(m_prev, jnp.max(sc, axis=0, keepdims=True))
                alpha = jnp.exp2(m_prev - m_new)
                p = jnp.exp2(sc - m_new)
                acc_ref[...] = alpha * acc_ref[...] + values_t_dot(vtt, p.astype(BF16))
                return m_new, alpha * l_prev + jnp.sum(p, axis=0, keepdims=True)

            def body(t, carry):
                off = pl.multiple_of(t * tk, tk)
                return update(carry, k_ref[0, pl.ds(off, tk), :], v_ref[0, pl.ds(off, tk), :])

            acc_ref[...] = jnp.zeros_like(acc_ref)
            init = (jnp.full((1, 2 * tq), -jnp.inf, F32), jnp.zeros((1, 2 * tq), F32))
            carry = lax.fori_loop(0, n_tiles, body, init)
            _, l = update(carry, kc_ref[0], vc_ref[0])
            finish(s, acc_ref[...], l)


def _diff_attention(fast, p, p_c, col0, lams, subln_w, lam_init, tq, tk, subs):
    B, L, _ = p.shape
    Lc = p_c.shape[1]
    H = ATTN_HEADS
    assert L % tk == 0
    lam_spec = pl.BlockSpec((1, HEAD_DIM), lambda b, h, i: (0, 0))
    keys = lambda rows, c0: pl.BlockSpec((1, rows, VALUE_DIM), lambda b, h, i: (b, 0, c0 + h))
    blocks = (2 * (2 * _nbytes((tq, VALUE_DIM), BF16) + 2 * _nbytes((L, VALUE_DIM), BF16)
                   + 2 * _nbytes((Lc, VALUE_DIM), BF16))
              + 3 * _nbytes((2 * tq, VALUE_DIM), F32) + 4 * _nbytes((2 * tq, tk), F32))
    return pl.pallas_call(
        functools.partial(_attn_kernel, tq=tq, tk=tk, lam_init=lam_init),
        grid=(B, H, L // (subs * tq)),
        in_specs=[
            pl.BlockSpec(memory_space=pltpu.SMEM),
            lam_spec, lam_spec, lam_spec, lam_spec,
            pl.BlockSpec((1, subs * tq, VALUE_DIM), lambda b, h, i: (b, i, col0["q"] + h)),
            keys(L, col0["k"]), keys(L, col0["v"]), keys(Lc, col0["kc"]), keys(Lc, col0["vc"]),
            pl.BlockSpec((VALUE_DIM, 1), lambda b, h, i: (0, 0)),
        ],
        out_specs=pl.BlockSpec((1, subs * tq, VALUE_DIM), lambda b, h, i: (b, i, h)),
        out_shape=jax.ShapeDtypeStruct((B, L, H * VALUE_DIM), BF16),
        scratch_shapes=[pltpu.VMEM((subs, 2 * tq, VALUE_DIM), BF16),
                        pltpu.VMEM((VALUE_DIM, 2 * tq), F32)],
        compiler_params=pltpu.CompilerParams(
            dimension_semantics=("parallel", "parallel", "parallel"),
            vmem_limit_bytes=_vmem_limit(blocks)),
        name="diff_attention",
    )(fast, *lams, p, p, p, p_c, p_c, subln_w)


def _mixer_kernel(heads_ref, u_ref, uprev_ref, unext_ref, g_ref, x_ref, gate_ref,
                  wa_ref, wb_ref, pw_ref, ps_ref, wo_ref, band_ref, o_ref, ubuf_ref, y_ref, *, tm, seq_len):
    i = pl.program_id(1)
    nt = pl.num_programs(1)
    D = x_ref.shape[2]
    group = u_ref.shape[2] // len(POOL_WINDOWS)

    y_a = jnp.dot(heads_ref[0], wa_ref[...], preferred_element_type=F32)

    ubuf_ref[pl.ds(POOL_HALO, tm), :] = u_ref[0]
    ubuf_ref[pl.ds(0, POOL_HALO), :] = jnp.where(i > 0, uprev_ref[0], jnp.zeros_like(uprev_ref[0]))
    ubuf_ref[pl.ds(POOL_HALO + tm, POOL_HALO), :] = jnp.where(i < nt - 1, unext_ref[0],
                                                             jnp.zeros_like(unext_ref[0]))

    t = i * tm + lax.broadcasted_iota(jnp.int32, (tm, 1), 0)
    totals = [jnp.dot(band_ref[gi], ubuf_ref[:, gi * group:(gi + 1) * group], preferred_element_type=F32)
              for gi in range(len(POOL_WINDOWS))]
    ds = []
    for gi, w in enumerate(POOL_WINDOWS):
        cols = slice(gi * group, (gi + 1) * group)
        lo = jnp.maximum(t - w // 2, 0)
        hi = jnp.minimum(t + (w - w // 2), seq_len)
        d = totals[gi] * (1.0 / (hi - lo).astype(F32)) - u_ref[0, :, cols].astype(F32)
        ds.append(d.astype(BF16))
    for gi in range(len(POOL_WINDOWS)):
        cols = slice(gi * group, (gi + 1) * group)
        yg = jnp.dot(ds[gi], pw_ref[gi], preferred_element_type=F32)
        y_ref[:, cols] = (yg * ps_ref[:, cols]).astype(BF16)

    y_b = jnp.dot(y_ref[...], wb_ref[...], preferred_element_type=F32)
    mixed = g_ref[0, :, :D].astype(F32) * y_a + g_ref[0, :, D:].astype(F32) * y_b
    mix = jnp.dot(mixed.astype(BF16), wo_ref[...], preferred_element_type=F32)
    o_ref[0] = x_ref[0] + gate_ref[0] * mix


def _token_mixer(heads, p_all, u_block, g_block, x, gate, w_a_up, w_b_up, pool_w, pool_scale, w_o, tm):
    B, L, D = x.shape
    AW, PW = heads.shape[2], w_b_up.shape[0]
    hb = tm // POOL_HALO
    offset = jnp.arange(tm + 2 * POOL_HALO)[None, :] - POOL_HALO - jnp.arange(tm)[:, None]
    band = jnp.stack([(offset >= -(w // 2)) & (offset < w - w // 2) for w in POOL_WINDOWS]).astype(BF16)
    n_halo = L // POOL_HALO
    const = lambda shape: pl.BlockSpec(shape, lambda b, i: (0,) * len(shape),
                                       pipeline_mode=pl.Buffered(1))
    weights = (_nbytes((AW, D), BF16) + _nbytes((PW, D), BF16) + _nbytes(pool_w.shape, BF16)
               + _nbytes((D, D), BF16))
    blocks = (weights + 2 * (_nbytes((tm, AW), BF16) + _nbytes((tm, PW), BF16)
                             + _nbytes((tm, 2 * D), BF16) + 2 * _nbytes((tm, D), F32))
              + _nbytes((tm + 2 * POOL_HALO, PW), F32) + _nbytes((tm, PW), BF16)
              + 4 * _nbytes((tm, D), F32))
    return pl.pallas_call(
        functools.partial(_mixer_kernel, tm=tm, seq_len=L),
        grid=(B, L // tm),
        in_specs=[
            pl.BlockSpec((1, tm, AW), lambda b, i: (b, i, 0)),
            pl.BlockSpec((1, tm, PW), lambda b, i: (b, i, u_block)),
            pl.BlockSpec((1, POOL_HALO, PW), lambda b, i: (b, jnp.maximum(i * hb - 1, 0), u_block)),
            pl.BlockSpec((1, POOL_HALO, PW), lambda b, i: (b, jnp.minimum((i + 1) * hb, n_halo - 1), u_block)),
            pl.BlockSpec((1, tm, 2 * D), lambda b, i: (b, i, g_block)),
            pl.BlockSpec((1, tm, D), lambda b, i: (b, i, 0)),
            pl.BlockSpec((1, 1, D), lambda b, i: (b, 0, 0)),
            const((AW, D)), const((PW, D)), const(pool_w.shape), const((1, PW)), const((D, D)),
            const(band.shape),
        ],
        out_specs=pl.BlockSpec((1, tm, D), lambda b, i: (b, i, 0)),
        out_shape=jax.ShapeDtypeStruct((B, L, D), F32),
        scratch_shapes=[pltpu.VMEM((tm + 2 * POOL_HALO, PW), BF16), pltpu.VMEM((tm, PW), BF16)],
        compiler_params=pltpu.CompilerParams(
            dimension_semantics=("parallel", "parallel"),
            vmem_limit_bytes=_vmem_limit(blocks)),
        name="token_mixer",
    )(heads, p_all, p_all, p_all, p_all, x, gate, w_a_up, w_b_up, pool_w, pool_scale, w_o, band)


def _mlp_kernel(x_ref, nw_ref, shift_ref, scale_ref, gate_ref, w1_ref, w2_ref, o_ref, h_ref):
    j = pl.program_id(2)
    tm, tf = h_ref.shape[0], w1_ref.shape[1]

    sub = min(tf, MLP_SUB)

    def ffn(h, rows, assign_first=False, finish=False):
        for s0 in range(0, tf, sub):
            a = jnp.maximum(jnp.dot(h, w1_ref[:, s0:s0 + sub], preferred_element_type=F32), 0.0)
            a2 = (a * a).astype(BF16)
            for n in range(0, o_ref.shape[2], sub):
                cols = slice(n, n + sub)
                part = jnp.dot(a2, w2_ref[s0:s0 + sub, cols], preferred_element_type=F32)
                if assign_first and s0 == 0:
                    o_ref[0, rows, cols] = part
                elif finish and s0 + sub == tf:
                    o_ref[0, rows, cols] = (x_ref[0, rows, cols]
                                            + gate_ref[0, :, cols] * (o_ref[0, rows, cols] + part))
                else:
                    o_ref[0, rows, cols] += part

    @pl.when(j == 0)
    def _():
        chunk = tm // 4
        for r0 in range(0, tm, chunk):
            rows = pl.ds(r0, chunk)
            h = _modulated_norm(x_ref[0, rows, :], nw_ref[...], shift_ref[0], scale_ref[0]).astype(BF16)
            h_ref[rows, :] = h
            ffn(h, rows, assign_first=True)

    last = pl.num_programs(2) - 1

    @pl.when((j > 0) & (j < last))
    def _():
        ffn(h_ref[...], pl.ds(0, tm))

    @pl.when(j == last)
    def _():
        ffn(h_ref[...], pl.ds(0, tm), finish=True)


def _mlp(x, norm_w, shift, scale, gate, w1, w2, tm, tf):
    B, L, D = x.shape
    F = w1.shape[1]
    assert F // tf >= 2
    vec = lambda: pl.BlockSpec((1, 1, D), lambda b, i, j: (b, 0, 0))
    blocks = (2 * (2 * _nbytes((tm, D), F32) + _nbytes((D, tf), BF16) + _nbytes((tf, D), BF16))
              + _nbytes((tm, D), BF16) + 3 * _nbytes((tm, min(tf, MLP_SUB)), F32))
    return pl.pallas_call(
        _mlp_kernel,
        grid=(B, L // tm, F // tf),
        in_specs=[
            pl.BlockSpec((1, tm, D), lambda b, i, j: (b, i, 0)),
            pl.BlockSpec((1, D), lambda b, i, j: (0, 0)),
            vec(), vec(), vec(),
            pl.BlockSpec((D, tf), lambda b, i, j: (0, j)),
            pl.BlockSpec((tf, D), lambda b, i, j: (j, 0)),
        ],
        out_specs=pl.BlockSpec((1, tm, D), lambda b, i, j: (b, i, 0)),
        out_shape=jax.ShapeDtypeStruct((B, L, D), F32),
        scratch_shapes=[pltpu.VMEM((tm, D), BF16)],
        compiler_params=pltpu.CompilerParams(
            dimension_semantics=("parallel", "parallel", "arbitrary"),
            vmem_limit_bytes=_vmem_limit(blocks)),
        name="sq_relu_mlp",
    )(x, norm_w, shift, scale, gate, w1, w2)


def _rope_tables(seq_len, rope):
    if not rope:
        ones = jnp.ones((seq_len, V7X_LANES), F32)
        return jnp.stack([ones, jnp.zeros_like(ones), jnp.zeros_like(ones)])
    t = jnp.arange(seq_len)
    row, col = t // GRID_W, t % GRID_W
    half = HEAD_DIM // 2
    quarter = half // 2
    inv_freq = ROPE_THETA ** (-jnp.arange(0, half, 2, dtype=F32) / half)
    lane = jnp.arange(V7X_LANES)
    e = lane % HEAD_DIM
    pos = jnp.where((e // half)[None, :] == 0, row[:, None], col[:, None]).astype(F32)
    ang = pos * inv_freq[e % quarter][None, :]
    cos, sin = jnp.cos(ang), jnp.sin(ang)
    second = ((e % half) >= quarter)[None, :]
    return jnp.stack([cos, jnp.where(second, sin, 0.0), jnp.where(second, 0.0, -sin)])


def _pick(n, pref):
    t = min(pref, n)
    while n % t:
        t //= 2
    return t


def kernel(x, c, ctx, c_ctx, w_mod, b_mod, norm_attn_w, w_in, q_norm_w, k_norm_w, lambda_q1, lambda_k1,
           lambda_q2, lambda_k2, subln_w, pool_w, pool_scale, w_a_up, w_b_up, w_o, norm_mlp_w, w_ff1, w_ff2):
    B, L, D = x.shape
    Lc = ctx.shape[1]
    depth = w_mod.shape[0]
    qk_w = 2 * ATTN_HEADS * HEAD_DIM
    tn = qk_w
    assert depth == 1 and w_in.shape[2] == 4 * tn + 2 * D and L % GRID_W == 0
    assert 2 * D == 2 * tn * 2 and pool_w.shape[1] * pool_w.shape[2] == tn

    for l in range(depth):
        lam_init = 0.8 - 0.6 * math.exp(-0.3 * l)

        rows = 8
        cvec = jnp.zeros((rows, D), F32).at[:B].set(c).at[B].set(c_ctx)
        mod = _modulation(cvec, w_mod[l], b_mod[l][None, :])
        sa, ca, ga, sm, cm, gm = [mod[:B, None, k * D:(k + 1) * D] for k in range(6)]
        sa_c, ca_c = [jnp.broadcast_to(mod[B, k * D:(k + 1) * D], (B, 1, D)) for k in range(2)]

        w_in_bf = w_in[l].astype(BF16)
        qw128 = jnp.tile(q_norm_w[l], 2) * (HEAD_DIM ** -0.5 * math.log2(math.e))
        kw128 = jnp.tile(k_norm_w[l], 2)
        qk_w128 = jnp.stack([qw128, kw128])
        nw = norm_attn_w[l][None, :]

        seg_w = tn // 4
        seg = jnp.kron(jnp.eye(seg_w // HEAD_DIM, dtype=F32),
                       jnp.full((HEAD_DIM, HEAD_DIM), 1.0 / HEAD_DIM)).astype(BF16)
        pc = _input_projection(ctx, nw, sa_c, ca_c, w_in_bf, 1, ("k", "v"), seg, qk_w128,
                               _rope_tables(Lc, rope=False), tm=_pick(Lc, 256), tn=tn)
        p = _input_projection(x, nw, sa, ca, w_in_bf, 0, ("q", "k", "v", "u", "g", "g", "g", "g"), seg,
                              qk_w128, _rope_tables(L, rope=True), tm=_pick(L, 1024), tn=tn)

        lams = [v[l][None, :] for v in (lambda_q1, lambda_k1, lambda_q2, lambda_k2)]
        score_bound = (HEAD_DIM * jnp.max(jnp.abs(qw128)) * jnp.max(jnp.abs(kw128))) * BF16_SLACK
        fast = (score_bound <= SCORE_BOUND).astype(jnp.int32).reshape(1)
        per_block = tn // VALUE_DIM
        col0 = {"q": 0, "k": per_block, "v": 2 * per_block, "kc": 0, "vc": per_block}
        heads = _diff_attention(fast, p, pc, col0, lams, subln_w[l][:, None], lam_init,
                                tq=_pick(L, 512), tk=_pick(L, 1024), subs=_pick(L // _pick(L, 512), 4))

        x = _token_mixer(heads, p, 3, 1, x, ga, w_a_up[l].astype(BF16), w_b_up[l].astype(BF16),
                         pool_w[l].astype(BF16), pool_scale[l][None, :], w_o[l].astype(BF16),
                         tm=_pick(L, 256))
        x = _mlp(x, norm_mlp_w[l][None, :], sm, cm, gm, w_ff1[l].astype(BF16), w_ff2[l].astype(BF16),
                 tm=_pick(L, 1024), tf=1024)
    return x
```

```python
import functools
import math

import jax
import jax.numpy as jnp
from jax import lax
from jax.experimental import pallas as pl
from jax.experimental.pallas import tpu as pltpu

F32 = jnp.float32
BF16 = jnp.bfloat16

ATTN_HEADS = 8
HEAD_DIM = 64
VALUE_DIM = 2 * HEAD_DIM
SCORE_BOUND = 64.0
BF16_SLACK = 1.0 + 2.0 ** -6
GRID_W = 64
ROPE_THETA = 10000.0
POOL_WINDOWS = (2, 4, 8, 16)
POOL_HALO = 16
MLP_SUB = 512
EPS = 1e-6

V7X_LANES = 128
V7X_VMEM_BYTES = 64 * 1024 * 1024
V7X_VMEM_CAP = V7X_VMEM_BYTES - 6 * 1024 * 1024


def _vmem_limit(block_bytes):
    return int(min(V7X_VMEM_CAP, block_bytes * 1.25 + 8 * 1024 * 1024))


def _nbytes(shape, dtype):
    return math.prod(shape) * jnp.dtype(dtype).itemsize


def _mod_kernel(c_ref, w_ref, b_ref, o_ref):
    cv = c_ref[...]
    act = cv * jax.nn.sigmoid(cv)
    o_ref[...] = jnp.dot(act, w_ref[...], preferred_element_type=F32) + b_ref[...]


def _modulation(cvec, w_mod, b_mod, tn=1024):
    rows, d = cvec.shape
    n = w_mod.shape[1]
    blocks = 2 * (_nbytes((d, tn), F32) + _nbytes((rows, tn), F32) * 2) + _nbytes((rows, d), F32)
    return pl.pallas_call(
        _mod_kernel,
        grid=(n // tn,),
        in_specs=[
            pl.BlockSpec((rows, d), lambda j: (0, 0)),
            pl.BlockSpec((d, tn), lambda j: (0, j)),
            pl.BlockSpec((1, tn), lambda j: (0, j)),
        ],
        out_specs=pl.BlockSpec((rows, tn), lambda j: (0, j)),
        out_shape=jax.ShapeDtypeStruct((rows, n), F32),
        compiler_params=pltpu.CompilerParams(
            dimension_semantics=("parallel",), vmem_limit_bytes=_vmem_limit(blocks)),
        name="modulation",
    )(cvec, w_mod, b_mod)


def _modulated_norm(x, norm_w, shift, scale):
    y = x * lax.rsqrt(jnp.mean(x * x, axis=-1, keepdims=True) + EPS)
    return (y * norm_w) * (1.0 + scale) + shift


def _inproj_kernel(x_ref, nw_ref, shift_ref, scale_ref, w_ref, seg_ref, qkw_ref, tab_ref, o_ref, h_ref,
                   *, kinds):
    j = pl.program_id(2)
    tm, tn = o_ref.shape[1], o_ref.shape[2]
    seg_w = seg_ref.shape[0]

    def qk_epilogue(which, first):
        quarter = HEAD_DIM // 4
        chunk = tm // 4
        w128 = qkw_ref[which:which + 1, :]
        for r0 in range(0, tm, chunk):
            rows = pl.ds(r0, chunk)
            if first:
                h = _modulated_norm(x_ref[0, rows, :], nw_ref[...], shift_ref[0], scale_ref[0])
                h_ref[rows, :] = h.astype(BF16)
            cos, sin_prev, sin_next = (tab_ref[i, rows, :] for i in range(3))
            acc = jnp.dot(h_ref[rows, :], w_ref[...], preferred_element_type=F32)
            for s0 in range(0, tn, seg_w):
                a = acc[:, s0:s0 + seg_w]
                ms = jnp.dot((a * a).astype(BF16), seg_ref[...], preferred_element_type=F32)
                y = a * lax.rsqrt(ms + EPS)
                for c in range(0, seg_w, V7X_LANES):
                    yc = y[:, c:c + V7X_LANES] * w128
                    out = (yc * cos + pltpu.roll(yc, quarter, 1) * sin_prev
                           + pltpu.roll(yc, V7X_LANES - quarter, 1) * sin_next)
                    o_ref[0, rows, s0 + c:s0 + c + V7X_LANES] = out.astype(BF16)

    def project():
        return jnp.dot(h_ref[...], w_ref[...], preferred_element_type=F32)

    def in_range(name):
        j0, n = kinds.index(name), kinds.count(name)
        return (j >= j0) & (j < j0 + n)

    assert kinds[0] in ("q", "k") and kinds.count(kinds[0]) == 1
    for name in dict.fromkeys(kinds):
        if name in ("q", "k"):
            pl.when(in_range(name))(functools.partial(qk_epilogue, ("q", "k").index(name), name == kinds[0]))
        elif name == "g":
            @pl.when(in_range(name))
            def _():
                o_ref[0] = (0.5 * jnp.tanh(0.5 * project()) + 0.5).astype(BF16)
        else:
            @pl.when(in_range(name))
            def _():
                o_ref[0] = project().astype(BF16)


def _input_projection(x, norm_w, shift, scale, w_in_bf, col_block0, kinds, seg_mean, qk_w, tables, tm, tn):
    B, L, D = x.shape
    vec = lambda: pl.BlockSpec((1, 1, D), lambda i, b, j: (b, 0, 0))
    blocks = (2 * (_nbytes((tm, D), F32) + _nbytes((D, tn), BF16) + _nbytes((3, tm, V7X_LANES), F32)
                   + _nbytes((tm, tn), BF16))
              + _nbytes((tm, D), BF16) + 2 * _nbytes((tm, tn), F32))
    return pl.pallas_call(
        functools.partial(_inproj_kernel, kinds=tuple(kinds)),
        grid=(L // tm, B, len(kinds)),
        in_specs=[
            pl.BlockSpec((1, tm, D), lambda i, b, j: (b, i, 0)),
            pl.BlockSpec((1, D), lambda i, b, j: (0, 0)),
            vec(), vec(),
            pl.BlockSpec((D, tn), lambda i, b, j: (0, col_block0 + j)),
            pl.BlockSpec(seg_mean.shape, lambda i, b, j: (0, 0)),
            pl.BlockSpec(qk_w.shape, lambda i, b, j: (0, 0)),
            pl.BlockSpec((3, tm, V7X_LANES), lambda i, b, j: (0, i, 0)),
        ],
        out_specs=pl.BlockSpec((1, tm, tn), lambda i, b, j: (b, i, j)),
        out_shape=jax.ShapeDtypeStruct((B, L, tn * len(kinds)), BF16),
        scratch_shapes=[pltpu.VMEM((tm, D), BF16)],
        compiler_params=pltpu.CompilerParams(
            dimension_semantics=("parallel", "parallel", "arbitrary"),
            vmem_limit_bytes=_vmem_limit(blocks)),
        name="input_projection",
    )(x, norm_w, shift, scale, w_in_bf, seg_mean, qk_w, tables)


def _attn_kernel(fast_ref, lq1_ref, lk1_ref, lq2_ref, lk2_ref, q_ref, k_ref, v_ref, kc_ref, vc_ref,
                 sw_ref, o_ref, rhs_ref, acc_ref, *, tq, tk, lam_init):
    subs = q_ref.shape[1] // tq
    lane = lax.broadcasted_iota(jnp.int32, (tq, VALUE_DIM), 1)
    for s in range(subs):
        q = q_ref[0, pl.ds(s * tq, tq), :]
        zero = jnp.zeros_like(q)
        rhs_ref[s, :tq, :] = jnp.where(lane < HEAD_DIM, q, zero)
        rhs_ref[s, tq:, :] = jnp.where(lane >= HEAD_DIM, q, zero)

    def scores(s, kt):
        return lax.dot_general(kt, rhs_ref[s], (((1,), (1,)), ((), ())), preferred_element_type=F32)

    def values_t_dot(v, p):
        return lax.dot_general(v, p, (((0,), (0,)), ((), ())), preferred_element_type=F32)

    lam = (jnp.exp(jnp.sum(lq1_ref[...] * lk1_ref[...])) - jnp.exp(jnp.sum(lq2_ref[...] * lk2_ref[...]))
           + lam_init)

    def finish(s, acc, l):
        o = acc * (1.0 / l)
        od = o[:, :tq] - lam * o[:, tq:]
        y = od * lax.rsqrt(jnp.mean(od * od, axis=0, keepdims=True) + EPS) * sw_ref[...]
        o_ref[0, pl.ds(s * tq, tq), :] = (y * (1.0 - lam_init)).T.astype(BF16)

    n_tiles = k_ref.shape[1] // tk

    @pl.when(fast_ref[0] == 1)
    def _():
        tiles = [(k_ref.at[0, pl.ds(t * tk, tk), :], v_ref.at[0, pl.ds(t * tk, tk), :])
                 for t in range(n_tiles)] + [(kc_ref.at[0], vc_ref.at[0])]

        def run(s, some_tiles, state):
            acc, lsum = state
            for kt_ref, vtt_ref in some_tiles:
                p = jnp.exp2(scores(s, kt_ref[...]))
                part = jnp.sum(p.reshape(p.shape[0] // 8, 8, p.shape[1]), axis=0)
                pv = values_t_dot(vtt_ref[...], p.astype(BF16))
                acc = pv if acc is None else acc + pv
                lsum = part if lsum is None else lsum + part
            return acc, lsum

        state = run(0, tiles, (None, None))
        for s in range(1, subs):
            started = run(s, tiles[:1], (None, None))
            finish(s - 1, state[0], jnp.sum(state[1], axis=0, keepdims=True))
            state = run(s, tiles[1:], started)
        finish(subs - 1, state[0], jnp.sum(state[1], axis=0, keepdims=True))

    @pl.when(fast_ref[0] != 1)
    def _():
        for s in range(subs):
            def update(carry, kt, vtt):
                m_prev, l_prev = carry
                sc = scores(s, kt)
                m_new = jnp.maximum(m_prev, jnp.max(sc, axis=0, keepdims=True))
                alpha = jnp.exp2(m_prev - m_new)
                p = jnp.exp2(sc - m_new)
                acc_ref[...] = alpha * acc_ref[...] + values_t_dot(vtt, p.astype(BF16))
                return m_new, alpha * l_prev + jnp.sum(p, axis=0, keepdims=True)

            def body(t, carry):
                off = pl.multiple_of(t * tk, tk)
                return update(carry, k_ref[0, pl.ds(off, tk), :], v_ref[0, pl.ds(off, tk), :])

            acc_ref[...] = jnp.zeros_like(acc_ref)
            init = (jnp.full((1, 2 * tq), -jnp.inf, F32), jnp.zeros((1, 2 * tq), F32))
            carry = lax.fori_loop(0, n_tiles, body, init)
            _, l = update(carry, kc_ref[0], vc_ref[0])
            finish(s, acc_ref[...], l)


def _diff_attention(fast, p, p_c, col0, lams, subln_w, lam_init, tq, tk, subs):
    B, L, _ = p.shape
    Lc = p_c.shape[1]
    H = ATTN_HEADS
    assert L % tk == 0
    lam_spec = pl.BlockSpec((1, HEAD_DIM), lambda b, h, i: (0, 0))
    keys = lambda rows, c0: pl.BlockSpec((1, rows, VALUE_DIM), lambda b, h, i: (b, 0, c0 + h))
    blocks = (2 * (2 * _nbytes((tq, VALUE_DIM), BF16) + 2 * _nbytes((L, VALUE_DIM), BF16)
                   + 2 * _nbytes((Lc, VALUE_DIM), BF16))
              + 3 * _nbytes((2 * tq, VALUE_DIM), F32) + 4 * _nbytes((2 * tq, tk), F32))
    return pl.pallas_call(
        functools.partial(_attn_kernel, tq=tq, tk=tk, lam_init=lam_init),
        grid=(B, H, L // (subs * tq)),
        in_specs=[
            pl.BlockSpec(memory_space=pltpu.SMEM),
            lam_spec, lam_spec, lam_spec, lam_spec,
            pl.BlockSpec((1, subs * tq, VALUE_DIM), lambda b, h, i: (b, i, col0["q"] + h)),
            keys(L, col0["k"]), keys(L, col0["v"]), keys(Lc, col0["kc"]), keys(Lc, col0["vc"]),
            pl.BlockSpec((VALUE_DIM, 1), lambda b, h, i: (0, 0)),
        ],
        out_specs=pl.BlockSpec((1, subs * tq, VALUE_DIM), lambda b, h, i: (b, i, h)),
        out_shape=jax.ShapeDtypeStruct((B, L, H * VALUE_DIM), BF16),
        scratch_shapes=[pltpu.VMEM((subs, 2 * tq, VALUE_DIM), BF16),
                        pltpu.VMEM((VALUE_DIM, 2 * tq), F32)],
        compiler_params=pltpu.CompilerParams(
            dimension_semantics=("parallel", "parallel", "parallel"),
            vmem_limit_bytes=_vmem_limit(blocks)),
        name="diff_attention",
    )(fast, *lams, p, p, p, p_c, p_c, subln_w)


def _mixer_kernel(heads_ref, u_ref, uprev_ref, unext_ref, g_ref, x_ref, gate_ref,
                  wa_ref, wb_ref, pw_ref, ps_ref, wo_ref, band_ref, o_ref, ubuf_ref, y_ref, *, tm, seq_len):
    i = pl.program_id(1)
    nt = pl.num_programs(1)
    D = x_ref.shape[2]
    group = u_ref.shape[2] // len(POOL_WINDOWS)

    y_a = jnp.dot(heads_ref[0], wa_ref[...], preferred_element_type=F32)

    ubuf_ref[pl.ds(POOL_HALO, tm), :] = u_ref[0]
    ubuf_ref[pl.ds(0, POOL_HALO), :] = jnp.where(i > 0, uprev_ref[0], jnp.zeros_like(uprev_ref[0]))
    ubuf_ref[pl.ds(POOL_HALO + tm, POOL_HALO), :] = jnp.where(i < nt - 1, unext_ref[0],
                                                             jnp.zeros_like(unext_ref[0]))

    t = i * tm + lax.broadcasted_iota(jnp.int32, (tm, 1), 0)
    totals = [jnp.dot(band_ref[gi], ubuf_ref[:, gi * group:(gi + 1) * group], preferred_element_type=F32)
              for gi in range(len(POOL_WINDOWS))]
    ds = []
    for gi, w in enumerate(POOL_WINDOWS):
        cols = slice(gi * group, (gi + 1) * group)
        lo = jnp.maximum(t - w // 2, 0)
        hi = jnp.minimum(t + (w - w // 2), seq_len)
        d = totals[gi] * (1.0 / (hi - lo).astype(F32)) - u_ref[0, :, cols].astype(F32)
        ds.append(d.astype(BF16))
    for gi in range(len(POOL_WINDOWS)):
        cols = slice(gi * group, (gi + 1) * group)
        yg = jnp.dot(ds[gi], pw_ref[gi], preferred_element_type=F32)
        y_ref[:, cols] = (yg * ps_ref[:, cols]).astype(BF16)

    y_b = jnp.dot(y_ref[...], wb_ref[...], preferred_element_type=F32)
    mixed = g_ref[0, :, :D].astype(F32) * y_a + g_ref[0, :, D:].astype(F32) * y_b
    mix = jnp.dot(mixed.astype(BF16), wo_ref[...], preferred_element_type=F32)
    o_ref[0] = x_ref[0] + gate_ref[0] * mix


def _token_mixer(heads, p_all, u_block, g_block, x, gate, w_a_up, w_b_up, pool_w, pool_scale, w_o, tm):
    B, L, D = x.shape
    AW, PW = heads.shape[2], w_b_up.shape[0]
    hb = tm // POOL_HALO
    offset = jnp.arange(tm + 2 * POOL_HALO)[None, :] - POOL_HALO - jnp.arange(tm)[:, None]
    band = jnp.stack([(offset >= -(w // 2)) & (offset < w - w // 2) for w in POOL_WINDOWS]).astype(BF16)
    n_halo = L // POOL_HALO
    const = lambda shape: pl.BlockSpec(shape, lambda b, i: (0,) * len(shape),
                                       pipeline_mode=pl.Buffered(1))
    weights = (_nbytes((AW, D), BF16) + _nbytes((PW, D), BF16) + _nbytes(pool_w.shape, BF16)
               + _nbytes((D, D), BF16))
    blocks = (weights + 2 * (_nbytes((tm, AW), BF16) + _nbytes((tm, PW), BF16)
                             + _nbytes((tm, 2 * D), BF16) + 2 * _nbytes((tm, D), F32))
              + _nbytes((tm + 2 * POOL_HALO, PW), F32) + _nbytes((tm, PW), BF16)
              + 4 * _nbytes((tm, D), F32))
    return pl.pallas_call(
        functools.partial(_mixer_kernel, tm=tm, seq_len=L),
        grid=(B, L // tm),
        in_specs=[
            pl.BlockSpec((1, tm, AW), lambda b, i: (b, i, 0)),
            pl.BlockSpec((1, tm, PW), lambda b, i: (b, i, u_block)),
            pl.BlockSpec((1, POOL_HALO, PW), lambda b, i: (b, jnp.maximum(i * hb - 1, 0), u_block)),
            pl.BlockSpec((1, POOL_HALO, PW), lambda b, i: (b, jnp.minimum((i + 1) * hb, n_halo - 1), u_block)),
            pl.BlockSpec((1, tm, 2 * D), lambda b, i: (b, i, g_block)),
            pl.BlockSpec((1, tm, D), lambda b, i: (b, i, 0)),
            pl.BlockSpec((1, 1, D), lambda b, i: (b, 0, 0)),
            const((AW, D)), const((PW, D)), const(pool_w.shape), const((1, PW)), const((D, D)),
            const(band.shape),
        ],
        out_specs=pl.BlockSpec((1, tm, D), lambda b, i: (b, i, 0)),
        out_shape=jax.ShapeDtypeStruct((B, L, D), F32),
        scratch_shapes=[pltpu.VMEM((tm + 2 * POOL_HALO, PW), BF16), pltpu.VMEM((tm, PW), BF16)],
        compiler_params=pltpu.CompilerParams(
            dimension_semantics=("parallel", "parallel"),
            vmem_limit_bytes=_vmem_limit(blocks)),
        name="token_mixer",
    )(heads, p_all, p_all, p_all, p_all, x, gate, w_a_up, w_b_up, pool_w, pool_scale, w_o, band)


def _mlp_kernel(x_ref, nw_ref, shift_ref, scale_ref, gate_ref, w1_ref, w2_ref, o_ref, h_ref):
    j = pl.program_id(2)
    tm, tf = h_ref.shape[0], w1_ref.shape[1]

    sub = min(tf, MLP_SUB)

    def ffn(h, rows, assign_first=False, finish=False):
        for s0 in range(0, tf, sub):
            a = jnp.maximum(jnp.dot(h, w1_ref[:, s0:s0 + sub], preferred_element_type=F32), 0.0)
            a2 = (a * a).astype(BF16)
            for n in range(0, o_ref.shape[2], sub):
                cols = slice(n, n + sub)
                part = jnp.dot(a2, w2_ref[s0:s0 + sub, cols], preferred_element_type=F32)
                if assign_first and s0 == 0:
                    o_ref[0, rows, cols] = part
                elif finish and s0 + sub == tf:
                    o_ref[0, rows, cols] = (x_ref[0, rows, cols]
                                            + gate_ref[0, :, cols] * (o_ref[0, rows, cols] + part))
                else:
                    o_ref[0, rows, cols] += part

    @pl.when(j == 0)
    def _():
        chunk = tm // 4
        for r0 in range(0, tm, chunk):
            rows = pl.ds(r0, chunk)
            h = _modulated_norm(x_ref[0, rows, :], nw_ref[...], shift_ref[0], scale_ref[0]).astype(BF16)
            h_ref[rows, :] = h
            ffn(h, rows, assign_first=True)

    last = pl.num_programs(2) - 1

    @pl.when((j > 0) & (j < last))
    def _():
        ffn(h_ref[...], pl.ds(0, tm))

    @pl.when(j == last)
    def _():
        ffn(h_ref[...], pl.ds(0, tm), finish=True)


def _mlp(x, norm_w, shift, scale, gate, w1, w2, tm, tf):
    B, L, D = x.shape
    F = w1.shape[1]
    assert F // tf >= 2
    vec = lambda: pl.BlockSpec((1, 1, D), lambda b, i, j: (b, 0, 0))
    blocks = (2 * (2 * _nbytes((tm, D), F32) + _nbytes((D, tf), BF16) + _nbytes((tf, D), BF16))
              + _nbytes((tm, D), BF16) + 3 * _nbytes((tm, min(tf, MLP_SUB)), F32))
    return pl.pallas_call(
        _mlp_kernel,
        grid=(B, L // tm, F // tf),
        in_specs=[
            pl.BlockSpec((1, tm, D), lambda b, i, j: (b, i, 0)),
            pl.BlockSpec((1, D), lambda b, i, j: (0, 0)),
            vec(), vec(), vec(),
            pl.BlockSpec((D, tf), lambda b, i, j: (0, j)),
            pl.BlockSpec((tf, D), lambda b, i, j: (j, 0)),
        ],
        out_specs=pl.BlockSpec((1, tm, D), lambda b, i, j: (b, i, 0)),
        out_shape=jax.ShapeDtypeStruct((B, L, D), F32),
        scratch_shapes=[pltpu.VMEM((tm, D), BF16)],
        compiler_params=pltpu.CompilerParams(
            dimension_semantics=("parallel", "parallel", "arbitrary"),
            vmem_limit_bytes=_vmem_limit(blocks)),
        name="sq_relu_mlp",
    )(x, norm_w, shift, scale, gate, w1, w2)


def _rope_tables(seq_len, rope):
    if not rope:
        ones = jnp.ones((seq_len, V7X_LANES), F32)
        return jnp.stack([ones, jnp.zeros_like(ones), jnp.zeros_like(ones)])
    t = jnp.arange(seq_len)
    row, col = t // GRID_W, t % GRID_W
    half = HEAD_DIM // 2
    quarter = half // 2
    inv_freq = ROPE_THETA ** (-jnp.arange(0, half, 2, dtype=F32) / half)
    lane = jnp.arange(V7X_LANES)
    e = lane % HEAD_DIM
    pos = jnp.where((e // half)[None, :] == 0, row[:, None], col[:, None]).astype(F32)
    ang = pos * inv_freq[e % quarter][None, :]
    cos, sin = jnp.cos(ang), jnp.sin(ang)
    second = ((e % half) >= quarter)[None, :]
    return jnp.stack([cos, jnp.where(second, sin, 0.0), jnp.where(second, 0.0, -sin)])


def _pick(n, pref):
    t = min(pref, n)
    while n % t:
        t //= 2
    return t


def kernel(x, c, ctx, c_ctx, w_mod, b_mod, norm_attn_w, w_in, q_norm_w, k_norm_w, lambda_q1, lambda_k1,
           lambda_q2, lambda_k2, subln_w, pool_w, pool_scale, w_a_up, w_b_up, w_o, norm_mlp_w, w_ff1, w_ff2):
    B, L, D = x.shape
    Lc = ctx.shape[1]
    depth = w_mod.shape[0]
    qk_w = 2 * ATTN_HEADS * HEAD_DIM
    tn = qk_w
    assert depth == 1 and w_in.shape[2] == 4 * tn + 2 * D and L % GRID_W == 0
    assert 2 * D == 2 * tn * 2 and pool_w.shape[1] * pool_w.shape[2] == tn

    for l in range(depth):
        lam_init = 0.8 - 0.6 * math.exp(-0.3 * l)

        rows = 8
        cvec = jnp.zeros((rows, D), F32).at[:B].set(c).at[B].set(c_ctx)
        mod = _modulation(cvec, w_mod[l], b_mod[l][None, :])
        sa, ca, ga, sm, cm, gm = [mod[:B, None, k * D:(k + 1) * D] for k in range(6)]
        sa_c, ca_c = [mod[B, k * D:(k + 1) * D].reshape(1, 1, D) for k in range(2)]

        w_in_bf = w_in[l].astype(BF16)
        qw128 = jnp.tile(q_norm_w[l], 2) * (HEAD_DIM ** -0.5 * math.log2(math.e))
        kw128 = jnp.tile(k_norm_w[l], 2)
        qk_w128 = jnp.stack([qw128, kw128])
        nw = norm_attn_w[l][None, :]

        seg_w = tn // 4
        seg = jnp.kron(jnp.eye(seg_w // HEAD_DIM, dtype=F32),
                       jnp.full((HEAD_DIM, HEAD_DIM), 1.0 / HEAD_DIM)).astype(BF16)
        pc = _input_projection(ctx.reshape(1, B * Lc, D), nw, sa_c, ca_c, w_in_bf, 1, ("k", "v"), seg,
                               qk_w128, _rope_tables(B * Lc, rope=False), tm=_pick(B * Lc, 1024), tn=tn)
        pc = pc.reshape(B, Lc, pc.shape[2])
        p = _input_projection(x, nw, sa, ca, w_in_bf, 0, ("q", "k", "v", "u", "g", "g", "g", "g"), seg,
                              qk_w128, _rope_tables(L, rope=True), tm=_pick(L, 1024), tn=tn)

        lams = [v[l][None, :] for v in (lambda_q1, lambda_k1, lambda_q2, lambda_k2)]
        score_bound = (HEAD_DIM * jnp.max(jnp.abs(qw128)) * jnp.max(jnp.abs(kw128))) * BF16_SLACK
        fast = (score_bound <= SCORE_BOUND).astype(jnp.int32).reshape(1)
        per_block = tn // VALUE_DIM
        col0 = {"q": 0, "k": per_block, "v": 2 * per_block, "kc": 0, "vc": per_block}
        heads = _diff_attention(fast, p, pc, col0, lams, subln_w[l][:, None], lam_init,
                                tq=_pick(L, 512), tk=_pick(L, 1024), subs=_pick(L // _pick(L, 512), 4))

        x = _token_mixer(heads, p, 3, 1, x, ga, w_a_up[l].astype(BF16), w_b_up[l].astype(BF16),
                         pool_w[l].astype(BF16), pool_scale[l][None, :], w_o[l].astype(BF16),
                         tm=_pick(L, 256))
        x = _mlp(x, norm_mlp_w[l][None, :], sm, cm, gm, w_ff1[l].astype(BF16), w_ff2[l].astype(BF16),
                 tm=_pick(L, 1024), tf=1024)
    return x
```

```python
import functools
import math

import jax
import jax.numpy as jnp
from jax import lax
from jax.experimental import pallas as pl
from jax.experimental.pallas import tpu as pltpu

F32 = jnp.float32
BF16 = jnp.bfloat16

ATTN_HEADS = 8
HEAD_DIM = 64
VALUE_DIM = 2 * HEAD_DIM
SCORE_BOUND = 64.0
BF16_SLACK = 1.0 + 2.0 ** -6
GRID_W = 64
ROPE_THETA = 10000.0
POOL_WINDOWS = (2, 4, 8, 16)
POOL_HALO = 16
MLP_SUB = 512
EPS = 1e-6

V7X_LANES = 128
V7X_SUBLANES = 8
V7X_VMEM_BYTES = 64 * 1024 * 1024
V7X_VMEM_CAP = V7X_VMEM_BYTES - 6 * 1024 * 1024


def _vmem_limit(block_bytes):
    return int(min(V7X_VMEM_CAP, block_bytes * 1.25 + 8 * 1024 * 1024))


def _nbytes(shape, dtype):
    return math.prod(shape) * jnp.dtype(dtype).itemsize


def _mod_kernel(c_ref, w_ref, b_ref, o_ref):
    cv = c_ref[...]
    act = cv * jax.nn.sigmoid(cv)
    o_ref[...] = jnp.dot(act, w_ref[...], preferred_element_type=F32) + b_ref[...]


def _modulation(cvec, w_mod, b_mod, tn=1024):
    rows, d = cvec.shape
    n = w_mod.shape[1]
    blocks = 2 * (_nbytes((d, tn), F32) + _nbytes((rows, tn), F32) * 2) + _nbytes((rows, d), F32)
    return pl.pallas_call(
        _mod_kernel,
        grid=(n // tn,),
        in_specs=[
            pl.BlockSpec((rows, d), lambda j: (0, 0)),
            pl.BlockSpec((d, tn), lambda j: (0, j)),
            pl.BlockSpec((1, tn), lambda j: (0, j)),
        ],
        out_specs=pl.BlockSpec((rows, tn), lambda j: (0, j)),
        out_shape=jax.ShapeDtypeStruct((rows, n), F32),
        compiler_params=pltpu.CompilerParams(
            dimension_semantics=("parallel",), vmem_limit_bytes=_vmem_limit(blocks)),
        name="modulation",
    )(cvec, w_mod, b_mod)


def _modulated_norm(x, norm_w, shift, scale):
    y = x * lax.rsqrt(jnp.mean(x * x, axis=-1, keepdims=True) + EPS)
    return (y * norm_w) * (1.0 + scale) + shift


def _inproj_kernel(x_ref, nw_ref, shift_ref, scale_ref, w_ref, seg_ref, qkw_ref, tab_ref, o_ref, h_ref,
                   *, kinds):
    j = pl.program_id(2)
    tm, tn = o_ref.shape[1], o_ref.shape[2]
    seg_w = seg_ref.shape[0]

    def qk_epilogue(which, first):
        quarter = HEAD_DIM // 4
        chunk = tm // 4
        w128 = qkw_ref[which:which + 1, :]
        for r0 in range(0, tm, chunk):
            rows = pl.ds(r0, chunk)
            if first:
                h = _modulated_norm(x_ref[0, rows, :], nw_ref[...], shift_ref[0], scale_ref[0])
                h_ref[rows, :] = h.astype(BF16)
            cos, sin_prev, sin_next = (tab_ref[i, rows, :] for i in range(3))
            acc = jnp.dot(h_ref[rows, :], w_ref[...], preferred_element_type=F32)
            for s0 in range(0, tn, seg_w):
                a = acc[:, s0:s0 + seg_w]
                ms = jnp.dot((a * a).astype(BF16), seg_ref[...], preferred_element_type=F32)
                y = a * lax.rsqrt(ms + EPS)
                for c in range(0, seg_w, V7X_LANES):
                    yc = y[:, c:c + V7X_LANES] * w128
                    out = (yc * cos + pltpu.roll(yc, quarter, 1) * sin_prev
                           + pltpu.roll(yc, V7X_LANES - quarter, 1) * sin_next)
                    o_ref[0, rows, s0 + c:s0 + c + V7X_LANES] = out.astype(BF16)

    def project():
        return jnp.dot(h_ref[...], w_ref[...], preferred_element_type=F32)

    def in_range(name):
        j0, n = kinds.index(name), kinds.count(name)
        return (j >= j0) & (j < j0 + n)

    assert kinds[0] in ("q", "k") and kinds.count(kinds[0]) == 1
    for name in dict.fromkeys(kinds):
        if name in ("q", "k"):
            pl.when(in_range(name))(functools.partial(qk_epilogue, ("q", "k").index(name), name == kinds[0]))
        elif name == "g":
            @pl.when(in_range(name))
            def _():
                o_ref[0] = (0.5 * jnp.tanh(0.5 * project()) + 0.5).astype(BF16)
        else:
            @pl.when(in_range(name))
            def _():
                o_ref[0] = project().astype(BF16)


def _input_projection(x, norm_w, shift, scale, w_in_bf, col_block0, kinds, seg_mean, qk_w, tables, tm, tn):
    B, L, D = x.shape
    vec = lambda: pl.BlockSpec((1, 1, D), lambda i, b, j: (b, 0, 0))
    blocks = (2 * (_nbytes((tm, D), F32) + _nbytes((D, tn), BF16) + _nbytes((3, tm, V7X_LANES), F32)
                   + _nbytes((tm, tn), BF16))
              + _nbytes((tm, D), BF16) + 2 * _nbytes((tm, tn), F32))
    return pl.pallas_call(
        functools.partial(_inproj_kernel, kinds=tuple(kinds)),
        grid=(L // tm, B, len(kinds)),
        in_specs=[
            pl.BlockSpec((1, tm, D), lambda i, b, j: (b, i, 0)),
            pl.BlockSpec((1, D), lambda i, b, j: (0, 0)),
            vec(), vec(),
            pl.BlockSpec((D, tn), lambda i, b, j: (0, col_block0 + j)),
            pl.BlockSpec(seg_mean.shape, lambda i, b, j: (0, 0)),
            pl.BlockSpec(qk_w.shape, lambda i, b, j: (0, 0)),
            pl.BlockSpec((3, tm, V7X_LANES), lambda i, b, j: (0, i, 0)),
        ],
        out_specs=pl.BlockSpec((1, tm, tn), lambda i, b, j: (b, i, j)),
        out_shape=jax.ShapeDtypeStruct((B, L, tn * len(kinds)), BF16),
        scratch_shapes=[pltpu.VMEM((tm, D), BF16)],
        compiler_params=pltpu.CompilerParams(
            dimension_semantics=("parallel", "parallel", "arbitrary"),
            vmem_limit_bytes=_vmem_limit(blocks)),
        name="input_projection",
    )(x, norm_w, shift, scale, w_in_bf, seg_mean, qk_w, tables)


def _attn_kernel(fast_ref, lq1_ref, lk1_ref, lq2_ref, lk2_ref, q_ref, k_ref, v_ref, kc_ref, vc_ref,
                 sw_ref, o_ref, rhs_ref, acc_ref, *, tq, tk, lam_init):
    subs = q_ref.shape[1] // tq
    lane = lax.broadcasted_iota(jnp.int32, (tq, VALUE_DIM), 1)
    for s in range(subs):
        q = q_ref[0, pl.ds(s * tq, tq), :]
        zero = jnp.zeros_like(q)
        rhs_ref[s, :tq, :] = jnp.where(lane < HEAD_DIM, q, zero)
        rhs_ref[s, tq:, :] = jnp.where(lane >= HEAD_DIM, q, zero)

    def scores(s, kt):
        return lax.dot_general(kt, rhs_ref[s], (((1,), (1,)), ((), ())), preferred_element_type=F32)

    def values_t_dot(v, p):
        return lax.dot_general(v, p, (((0,), (0,)), ((), ())), preferred_element_type=F32)

    lam = (jnp.exp(jnp.sum(lq1_ref[...] * lk1_ref[...])) - jnp.exp(jnp.sum(lq2_ref[...] * lk2_ref[...]))
           + lam_init)

    def finish(s, acc, l):
        o = acc * (1.0 / l)
        od = o[:, :tq] - lam * o[:, tq:]
        y = od * lax.rsqrt(jnp.mean(od * od, axis=0, keepdims=True) + EPS) * sw_ref[...]
        o_ref[0, pl.ds(s * tq, tq), :] = (y * (1.0 - lam_init)).T.astype(BF16)

    n_tiles = k_ref.shape[1] // tk

    @pl.when(fast_ref[0] == 1)
    def _():
        tiles = [(k_ref.at[0, pl.ds(t * tk, tk), :], v_ref.at[0, pl.ds(t * tk, tk), :])
                 for t in range(n_tiles)] + [(kc_ref.at[0], vc_ref.at[0])]

        def run(s, some_tiles, state):
            acc, lsum = state
            for kt_ref, vtt_ref in some_tiles:
                p = jnp.exp2(scores(s, kt_ref[...]))
                part = jnp.sum(p.reshape(p.shape[0] // V7X_SUBLANES, V7X_SUBLANES, p.shape[1]), axis=0)
                pv = values_t_dot(vtt_ref[...], p.astype(BF16))
                acc = pv if acc is None else acc + pv
                lsum = part if lsum is None else lsum + part
            return acc, lsum

        state = run(0, tiles, (None, None))
        for s in range(1, subs):
            started = run(s, tiles[:1], (None, None))
            finish(s - 1, state[0], jnp.sum(state[1], axis=0, keepdims=True))
            state = run(s, tiles[1:], started)
        finish(subs - 1, state[0], jnp.sum(state[1], axis=0, keepdims=True))

    @pl.when(fast_ref[0] != 1)
    def _():
        for s in range(subs):
            def update(carry, kt, vtt):
                m_prev, l_prev = carry
                sc = scores(s, kt)
                m_new = jnp.maximum(m_prev, jnp.max(sc, axis=0, keepdims=True))
                alpha = jnp.exp2(m_prev - m_new)
                p = jnp.exp2(sc - m_new)
                acc_ref[...] = alpha * acc_ref[...] + values_t_dot(vtt, p.astype(BF16))
                return m_new, alpha * l_prev + jnp.sum(p, axis=0, keepdims=True)

            def body(t, carry):
                off = pl.multiple_of(t * tk, tk)
                return update(carry, k_ref[0, pl.ds(off, tk), :], v_ref[0, pl.ds(off, tk), :])

            acc_ref[...] = jnp.zeros_like(acc_ref)
            init = (jnp.full((1, 2 * tq), -jnp.inf, F32), jnp.zeros((1, 2 * tq), F32))
            carry = lax.fori_loop(0, n_tiles, body, init)
            _, l = update(carry, kc_ref[0], vc_ref[0])
            finish(s, acc_ref[...], l)


def _diff_attention(fast, p, p_c, col0, lams, subln_w, lam_init, tq, tk, subs):
    B, L, _ = p.shape
    Lc = p_c.shape[1]
    H = ATTN_HEADS
    assert L % tk == 0
    lam_spec = pl.BlockSpec((1, HEAD_DIM), lambda b, h, i: (0, 0))
    keys = lambda rows, c0: pl.BlockSpec((1, rows, VALUE_DIM), lambda b, h, i: (b, 0, c0 + h))
    blocks = (2 * (2 * _nbytes((tq, VALUE_DIM), BF16) + 2 * _nbytes((L, VALUE_DIM), BF16)
                   + 2 * _nbytes((Lc, VALUE_DIM), BF16))
              + 3 * _nbytes((2 * tq, VALUE_DIM), F32) + 4 * _nbytes((2 * tq, tk), F32))
    return pl.pallas_call(
        functools.partial(_attn_kernel, tq=tq, tk=tk, lam_init=lam_init),
        grid=(B, H, L // (subs * tq)),
        in_specs=[
            pl.BlockSpec(memory_space=pltpu.SMEM),
            lam_spec, lam_spec, lam_spec, lam_spec,
            pl.BlockSpec((1, subs * tq, VALUE_DIM), lambda b, h, i: (b, i, col0["q"] + h)),
            keys(L, col0["k"]), keys(L, col0["v"]), keys(Lc, col0["kc"]), keys(Lc, col0["vc"]),
            pl.BlockSpec((VALUE_DIM, 1), lambda b, h, i: (0, 0)),
        ],
        out_specs=pl.BlockSpec((1, subs * tq, VALUE_DIM), lambda b, h, i: (b, i, h)),
        out_shape=jax.ShapeDtypeStruct((B, L, H * VALUE_DIM), BF16),
        scratch_shapes=[pltpu.VMEM((subs, 2 * tq, VALUE_DIM), BF16),
                        pltpu.VMEM((VALUE_DIM, 2 * tq), F32)],
        compiler_params=pltpu.CompilerParams(
            dimension_semantics=("parallel", "parallel", "parallel"),
            vmem_limit_bytes=_vmem_limit(blocks)),
        name="diff_attention",
    )(fast, *lams, p, p, p, p_c, p_c, subln_w)


def _mixer_kernel(heads_ref, u_ref, uprev_ref, unext_ref, g_ref, x_ref, gate_ref,
                  wa_ref, wb_ref, pw_ref, ps_ref, wo_ref, band_ref, o_ref, ubuf_ref, y_ref, *, tm, seq_len):
    i = pl.program_id(1)
    nt = pl.num_programs(1)
    D = x_ref.shape[2]
    group = u_ref.shape[2] // len(POOL_WINDOWS)

    y_a = jnp.dot(heads_ref[0], wa_ref[...], preferred_element_type=F32)

    ubuf_ref[pl.ds(POOL_HALO, tm), :] = u_ref[0]
    ubuf_ref[pl.ds(0, POOL_HALO), :] = jnp.where(i > 0, uprev_ref[0], jnp.zeros_like(uprev_ref[0]))
    ubuf_ref[pl.ds(POOL_HALO + tm, POOL_HALO), :] = jnp.where(i < nt - 1, unext_ref[0],
                                                             jnp.zeros_like(unext_ref[0]))

    t = i * tm + lax.broadcasted_iota(jnp.int32, (tm, 1), 0)
    totals = [jnp.dot(band_ref[gi], ubuf_ref[:, gi * group:(gi + 1) * group], preferred_element_type=F32)
              for gi in range(len(POOL_WINDOWS))]
    ds = []
    for gi, w in enumerate(POOL_WINDOWS):
        cols = slice(gi * group, (gi + 1) * group)
        lo = jnp.maximum(t - w // 2, 0)
        hi = jnp.minimum(t + (w - w // 2), seq_len)
        d = totals[gi] * (1.0 / (hi - lo).astype(F32)) - u_ref[0, :, cols].astype(F32)
        ds.append(d.astype(BF16))
    for gi in range(len(POOL_WINDOWS)):
        cols = slice(gi * group, (gi + 1) * group)
        yg = jnp.dot(ds[gi], pw_ref[gi], preferred_element_type=F32)
        y_ref[:, cols] = (yg * ps_ref[:, cols]).astype(BF16)

    y_b = jnp.dot(y_ref[...], wb_ref[...], preferred_element_type=F32)
    mixed = g_ref[0, :, :D].astype(F32) * y_a + g_ref[0, :, D:].astype(F32) * y_b
    mix = jnp.dot(mixed.astype(BF16), wo_ref[...], preferred_element_type=F32)
    o_ref[0] = x_ref[0] + gate_ref[0] * mix


def _token_mixer(heads, p_all, u_block, g_block, x, gate, w_a_up, w_b_up, pool_w, pool_scale, w_o, tm):
    B, L, D = x.shape
    AW, PW = heads.shape[2], w_b_up.shape[0]
    hb = tm // POOL_HALO
    offset = jnp.arange(tm + 2 * POOL_HALO)[None, :] - POOL_HALO - jnp.arange(tm)[:, None]
    band = jnp.stack([(offset >= -(w // 2)) & (offset < w - w // 2) for w in POOL_WINDOWS]).astype(BF16)
    n_halo = L // POOL_HALO
    const = lambda shape: pl.BlockSpec(shape, lambda b, i: (0,) * len(shape),
                                       pipeline_mode=pl.Buffered(1))
    weights = (_nbytes((AW, D), BF16) + _nbytes((PW, D), BF16) + _nbytes(pool_w.shape, BF16)
               + _nbytes((D, D), BF16))
    blocks = (weights + 2 * (_nbytes((tm, AW), BF16) + _nbytes((tm, PW), BF16)
                             + _nbytes((tm, 2 * D), BF16) + 2 * _nbytes((tm, D), F32))
              + _nbytes((tm + 2 * POOL_HALO, PW), F32) + _nbytes((tm, PW), BF16)
              + 4 * _nbytes((tm, D), F32))
    return pl.pallas_call(
        functools.partial(_mixer_kernel, tm=tm, seq_len=L),
        grid=(B, L // tm),
        in_specs=[
            pl.BlockSpec((1, tm, AW), lambda b, i: (b, i, 0)),
            pl.BlockSpec((1, tm, PW), lambda b, i: (b, i, u_block)),
            pl.BlockSpec((1, POOL_HALO, PW), lambda b, i: (b, jnp.maximum(i * hb - 1, 0), u_block)),
            pl.BlockSpec((1, POOL_HALO, PW), lambda b, i: (b, jnp.minimum((i + 1) * hb, n_halo - 1), u_block)),
            pl.BlockSpec((1, tm, 2 * D), lambda b, i: (b, i, g_block)),
            pl.BlockSpec((1, tm, D), lambda b, i: (b, i, 0)),
            pl.BlockSpec((1, 1, D), lambda b, i: (b, 0, 0)),
            const((AW, D)), const((PW, D)), const(pool_w.shape), const((1, PW)), const((D, D)),
            const(band.shape),
        ],
        out_specs=pl.BlockSpec((1, tm, D), lambda b, i: (b, i, 0)),
        out_shape=jax.ShapeDtypeStruct((B, L, D), F32),
        scratch_shapes=[pltpu.VMEM((tm + 2 * POOL_HALO, PW), BF16), pltpu.VMEM((tm, PW), BF16)],
        compiler_params=pltpu.CompilerParams(
            dimension_semantics=("parallel", "parallel"),
            vmem_limit_bytes=_vmem_limit(blocks)),
        name="token_mixer",
    )(heads, p_all, p_all, p_all, p_all, x, gate, w_a_up, w_b_up, pool_w, pool_scale, w_o, band)


def _mlp_kernel(x_ref, nw_ref, shift_ref, scale_ref, gate_ref, w1_ref, w2_ref, o_ref, h_ref):
    j = pl.program_id(2)
    tm, tf = h_ref.shape[0], w1_ref.shape[1]

    sub = min(tf, MLP_SUB)

    def ffn(h, rows, assign_first=False, finish=False):
        for s0 in range(0, tf, sub):
            a = jnp.maximum(jnp.dot(h, w1_ref[:, s0:s0 + sub], preferred_element_type=F32), 0.0)
            a2 = (a * a).astype(BF16)
            for n in range(0, o_ref.shape[2], sub):
                cols = slice(n, n + sub)
                part = jnp.dot(a2, w2_ref[s0:s0 + sub, cols], preferred_element_type=F32)
                if assign_first and s0 == 0:
                    o_ref[0, rows, cols] = part
                elif finish and s0 + sub == tf:
                    o_ref[0, rows, cols] = (x_ref[0, rows, cols]
                                            + gate_ref[0, :, cols] * (o_ref[0, rows, cols] + part))
                else:
                    o_ref[0, rows, cols] += part

    @pl.when(j == 0)
    def _():
        chunk = tm // 4
        for r0 in range(0, tm, chunk):
            rows = pl.ds(r0, chunk)
            h = _modulated_norm(x_ref[0, rows, :], nw_ref[...], shift_ref[0], scale_ref[0]).astype(BF16)
            h_ref[rows, :] = h
            ffn(h, rows, assign_first=True)

    last = pl.num_programs(2) - 1

    @pl.when((j > 0) & (j < last))
    def _():
        ffn(h_ref[...], pl.ds(0, tm))

    @pl.when(j == last)
    def _():
        ffn(h_ref[...], pl.ds(0, tm), finish=True)


def _mlp(x, norm_w, shift, scale, gate, w1, w2, tm, tf):
    B, L, D = x.shape
    F = w1.shape[1]
    assert F // tf >= 2
    vec = lambda: pl.BlockSpec((1, 1, D), lambda b, i, j: (b, 0, 0))
    blocks = (2 * (2 * _nbytes((tm, D), F32) + _nbytes((D, tf), BF16) + _nbytes((tf, D), BF16))
              + _nbytes((tm, D), BF16) + 3 * _nbytes((tm, min(tf, MLP_SUB)), F32))
    return pl.pallas_call(
        _mlp_kernel,
        grid=(B, L // tm, F // tf),
        in_specs=[
            pl.BlockSpec((1, tm, D), lambda b, i, j: (b, i, 0)),
            pl.BlockSpec((1, D), lambda b, i, j: (0, 0)),
            vec(), vec(), vec(),
            pl.BlockSpec((D, tf), lambda b, i, j: (0, j)),
            pl.BlockSpec((tf, D), lambda b, i, j: (j, 0)),
        ],
        out_specs=pl.BlockSpec((1, tm, D), lambda b, i, j: (b, i, 0)),
        out_shape=jax.ShapeDtypeStruct((B, L, D), F32),
        scratch_shapes=[pltpu.VMEM((tm, D), BF16)],
        compiler_params=pltpu.CompilerParams(
            dimension_semantics=("parallel", "parallel", "arbitrary"),
            vmem_limit_bytes=_vmem_limit(blocks)),
        name="sq_relu_mlp",
    )(x, norm_w, shift, scale, gate, w1, w2)


def _rope_tables(seq_len, rope):
    if not rope:
        ones = jnp.ones((seq_len, V7X_LANES), F32)
        return jnp.stack([ones, jnp.zeros_like(ones), jnp.zeros_like(ones)])
    t = jnp.arange(seq_len)
    row, col = t // GRID_W, t % GRID_W
    half = HEAD_DIM // 2
    quarter = half // 2
    inv_freq = ROPE_THETA ** (-jnp.arange(0, half, 2, dtype=F32) / half)
    lane = jnp.arange(V7X_LANES)
    e = lane % HEAD_DIM
    pos = jnp.where((e // half)[None, :] == 0, row[:, None], col[:, None]).astype(F32)
    ang = pos * inv_freq[e % quarter][None, :]
    cos, sin = jnp.cos(ang), jnp.sin(ang)
    second = ((e % half) >= quarter)[None, :]
    return jnp.stack([cos, jnp.where(second, sin, 0.0), jnp.where(second, 0.0, -sin)])


def _pick(n, pref):
    t = min(pref, n)
    while n % t:
        t //= 2
    return t


def kernel(x, c, ctx, c_ctx, w_mod, b_mod, norm_attn_w, w_in, q_norm_w, k_norm_w, lambda_q1, lambda_k1,
           lambda_q2, lambda_k2, subln_w, pool_w, pool_scale, w_a_up, w_b_up, w_o, norm_mlp_w, w_ff1, w_ff2):
    B, L, D = x.shape
    Lc = ctx.shape[1]
    depth = w_mod.shape[0]
    qk_w = 2 * ATTN_HEADS * HEAD_DIM
    tn = qk_w
    assert depth == 1 and w_in.shape[2] == 4 * tn + 2 * D and L % GRID_W == 0
    assert 2 * D == 2 * tn * 2 and pool_w.shape[1] * pool_w.shape[2] == tn

    for l in range(depth):
        lam_init = 0.8 - 0.6 * math.exp(-0.3 * l)

        rows = -(-(B + 1) // V7X_SUBLANES) * V7X_SUBLANES
        cvec = jnp.zeros((rows, D), F32).at[:B].set(c).at[B].set(c_ctx)
        mod = _modulation(cvec, w_mod[l], b_mod[l][None, :])
        sa, ca, ga, sm, cm, gm = [mod[:B, None, k * D:(k + 1) * D] for k in range(6)]
        sa_c, ca_c = [mod[B, k * D:(k + 1) * D].reshape(1, 1, D) for k in range(2)]

        w_in_bf = w_in[l].astype(BF16)
        qw128 = jnp.tile(q_norm_w[l], 2) * (HEAD_DIM ** -0.5 * math.log2(math.e))
        kw128 = jnp.tile(k_norm_w[l], 2)
        qk_w128 = jnp.stack([qw128, kw128])
        nw = norm_attn_w[l][None, :]

        seg_w = tn // 4
        seg = jnp.kron(jnp.eye(seg_w // HEAD_DIM, dtype=F32),
                       jnp.full((HEAD_DIM, HEAD_DIM), 1.0 / HEAD_DIM)).astype(BF16)
        pc = _input_projection(ctx.reshape(1, B * Lc, D), nw, sa_c, ca_c, w_in_bf, 1, ("k", "v"), seg,
                               qk_w128, _rope_tables(B * Lc, rope=False), tm=_pick(B * Lc, 1024), tn=tn)
        pc = pc.reshape(B, Lc, pc.shape[2])
        p = _input_projection(x, nw, sa, ca, w_in_bf, 0, ("q", "k", "v", "u", "g", "g", "g", "g"), seg,
                              qk_w128, _rope_tables(L, rope=True), tm=_pick(L, 1024), tn=tn)

        lams = [v[l][None, :] for v in (lambda_q1, lambda_k1, lambda_q2, lambda_k2)]
        score_bound = (HEAD_DIM * jnp.max(jnp.abs(qw128)) * jnp.max(jnp.abs(kw128))) * BF16_SLACK
        fast = (score_bound <= SCORE_BOUND).astype(jnp.int32).reshape(1)
        per_block = tn // VALUE_DIM
        col0 = {"q": 0, "k": per_block, "v": 2 * per_block, "kc": 0, "vc": per_block}
        heads = _diff_attention(fast, p, pc, col0, lams, subln_w[l][:, None], lam_init,
                                tq=_pick(L, 512), tk=_pick(L, 1024), subs=_pick(L // _pick(L, 512), 4))

        x = _token_mixer(heads, p, 3, 1, x, ga, w_a_up[l].astype(BF16), w_b_up[l].astype(BF16),
                         pool_w[l].astype(BF16), pool_scale[l][None, :], w_o[l].astype(BF16),
                         tm=_pick(L, 256))
        x = _mlp(x, norm_mlp_w[l][None, :], sm, cm, gm, w_ff1[l].astype(BF16), w_ff2[l].astype(BF16),
                 tm=_pick(L, 1024), tf=1024)
    return x
```

```python
import functools
import math

import jax
import jax.numpy as jnp
from jax import lax
from jax.experimental import pallas as pl
from jax.experimental.pallas import tpu as pltpu

F32 = jnp.float32
BF16 = jnp.bfloat16

ATTN_HEADS = 8
HEAD_DIM = 64
VALUE_DIM = 2 * HEAD_DIM
SCORE_BOUND = 64.0
BF16_SLACK = 1.0 + 2.0 ** -6
GRID_W = 64
ROPE_THETA = 10000.0
POOL_WINDOWS = (2, 4, 8, 16)
POOL_HALO = 16
MLP_SUB = 512
EPS = 1e-6

V7X_LANES = 128
V7X_SUBLANES = 8
V7X_VMEM_BYTES = 64 * 1024 * 1024
V7X_VMEM_CAP = V7X_VMEM_BYTES - 6 * 1024 * 1024


def _vmem_limit(block_bytes):
    return int(min(V7X_VMEM_CAP, block_bytes * 1.25 + 8 * 1024 * 1024))


def _nbytes(shape, dtype):
    return math.prod(shape) * jnp.dtype(dtype).itemsize


def _mod_kernel(c_ref, w_ref, b_ref, o_ref):
    cv = c_ref[...]
    act = cv * jax.nn.sigmoid(cv)
    o_ref[...] = jnp.dot(act, w_ref[...], preferred_element_type=F32) + b_ref[...]


def _modulation(cvec, w_mod, b_mod, tn=1024):
    rows, d = cvec.shape
    n = w_mod.shape[1]
    blocks = 2 * (_nbytes((d, tn), F32) + _nbytes((rows, tn), F32) * 2) + _nbytes((rows, d), F32)
    return pl.pallas_call(
        _mod_kernel,
        grid=(n // tn,),
        in_specs=[
            pl.BlockSpec((rows, d), lambda j: (0, 0)),
            pl.BlockSpec((d, tn), lambda j: (0, j)),
            pl.BlockSpec((1, tn), lambda j: (0, j)),
        ],
        out_specs=pl.BlockSpec((rows, tn), lambda j: (0, j)),
        out_shape=jax.ShapeDtypeStruct((rows, n), F32),
        compiler_params=pltpu.CompilerParams(
            dimension_semantics=("parallel",), vmem_limit_bytes=_vmem_limit(blocks)),
        name="modulation",
    )(cvec, w_mod, b_mod)


def _modulated_norm(x, norm_w, shift, scale):
    y = x * lax.rsqrt(jnp.mean(x * x, axis=-1, keepdims=True) + EPS)
    return (y * norm_w) * (1.0 + scale) + shift


def _inproj_kernel(x_ref, nw_ref, shift_ref, scale_ref, w_ref, seg_ref, qkw_ref, tab_ref, o_ref, h_ref,
                   *, kinds, tn):
    j = pl.program_id(2)
    tm = o_ref.shape[1]
    per_step = o_ref.shape[2] // tn
    seg_w = seg_ref.shape[0]

    def qk_epilogue(which, first, c0):
        quarter = HEAD_DIM // 4
        chunk = tm // 4
        w128 = qkw_ref[which:which + 1, :]
        for r0 in range(0, tm, chunk):
            rows = pl.ds(r0, chunk)
            if first:
                h = _modulated_norm(x_ref[0, rows, :], nw_ref[...], shift_ref[0], scale_ref[0])
                h_ref[rows, :] = h.astype(BF16)
            cos, sin_prev, sin_next = (tab_ref[i, rows, :] for i in range(3))
            acc = jnp.dot(h_ref[rows, :], w_ref[:, c0:c0 + tn], preferred_element_type=F32)
            for s0 in range(0, tn, seg_w):
                a = acc[:, s0:s0 + seg_w]
                ms = jnp.dot((a * a).astype(BF16), seg_ref[...], preferred_element_type=F32)
                y = a * lax.rsqrt(ms + EPS)
                for c in range(0, seg_w, V7X_LANES):
                    yc = y[:, c:c + V7X_LANES] * w128
                    out = (yc * cos + pltpu.roll(yc, quarter, 1) * sin_prev
                           + pltpu.roll(yc, V7X_LANES - quarter, 1) * sin_next)
                    o_ref[0, rows, c0 + s0 + c:c0 + s0 + c + V7X_LANES] = out.astype(BF16)

    def elementwise_epilogue(fn, c0):
        acc = jnp.dot(h_ref[...], w_ref[:, c0:c0 + tn], preferred_element_type=F32)
        o_ref[0, :, c0:c0 + tn] = fn(acc).astype(BF16)

    def run_step(combo, first_step):
        for n, kind in enumerate(combo):
            if kind in ("q", "k"):
                qk_epilogue(("q", "k").index(kind), first_step and n == 0, n * tn)
            elif kind == "g":
                elementwise_epilogue(lambda z: 0.5 * jnp.tanh(0.5 * z) + 0.5, n * tn)
            else:
                elementwise_epilogue(lambda z: z, n * tn)

    steps = [tuple(kinds[s * per_step:(s + 1) * per_step]) for s in range(len(kinds) // per_step)]
    assert steps[0][0] in ("q", "k") and steps.count(steps[0]) == 1
    for combo in dict.fromkeys(steps):
        s_lo, n = steps.index(combo), steps.count(combo)
        assert steps[s_lo:s_lo + n] == [combo] * n
        pl.when((j >= s_lo) & (j < s_lo + n))(functools.partial(run_step, combo, s_lo == 0))


def _input_projection(x, norm_w, shift, scale, w_in_bf, col_block0, kinds, seg_mean, qk_w, tables, tm, tn,
                      per_step):
    B, L, D = x.shape
    wide = per_step * tn
    assert len(kinds) % per_step == 0
    vec = lambda: pl.BlockSpec((1, 1, D), lambda i, b, j: (b, 0, 0))
    blocks = (2 * (_nbytes((tm, D), F32) + _nbytes((D, wide), BF16) + _nbytes((3, tm, V7X_LANES), F32)
                   + _nbytes((tm, wide), BF16))
              + _nbytes((tm, D), BF16) + 2 * _nbytes((tm, tn), F32))
    return pl.pallas_call(
        functools.partial(_inproj_kernel, kinds=tuple(kinds), tn=tn),
        grid=(L // tm, B, len(kinds) // per_step),
        in_specs=[
            pl.BlockSpec((1, tm, D), lambda i, b, j: (b, i, 0)),
            pl.BlockSpec((1, D), lambda i, b, j: (0, 0)),
            vec(), vec(),
            pl.BlockSpec((D, wide), lambda i, b, j: (0, col_block0 + j)),
            pl.BlockSpec(seg_mean.shape, lambda i, b, j: (0, 0)),
            pl.BlockSpec(qk_w.shape, lambda i, b, j: (0, 0)),
            pl.BlockSpec((3, tm, V7X_LANES), lambda i, b, j: (0, i, 0)),
        ],
        out_specs=pl.BlockSpec((1, tm, wide), lambda i, b, j: (b, i, j)),
        out_shape=jax.ShapeDtypeStruct((B, L, tn * len(kinds)), BF16),
        scratch_shapes=[pltpu.VMEM((tm, D), BF16)],
        compiler_params=pltpu.CompilerParams(
            dimension_semantics=("parallel", "parallel", "arbitrary"),
            vmem_limit_bytes=_vmem_limit(blocks)),
        name="input_projection",
    )(x, norm_w, shift, scale, w_in_bf, seg_mean, qk_w, tables)


def _attn_kernel(fast_ref, lq1_ref, lk1_ref, lq2_ref, lk2_ref, q_ref, k_ref, v_ref, kc_ref, vc_ref,
                 sw_ref, o_ref, rhs_ref, acc_ref, *, tq, tk, lam_init):
    subs = q_ref.shape[1] // tq
    lane = lax.broadcasted_iota(jnp.int32, (tq, VALUE_DIM), 1)
    for s in range(subs):
        q = q_ref[0, pl.ds(s * tq, tq), :]
        zero = jnp.zeros_like(q)
        rhs_ref[s, :tq, :] = jnp.where(lane < HEAD_DIM, q, zero)
        rhs_ref[s, tq:, :] = jnp.where(lane >= HEAD_DIM, q, zero)

    def scores(s, kt):
        return lax.dot_general(kt, rhs_ref[s], (((1,), (1,)), ((), ())), preferred_element_type=F32)

    def values_t_dot(v, p):
        return lax.dot_general(v, p, (((0,), (0,)), ((), ())), preferred_element_type=F32)

    lam = (jnp.exp(jnp.sum(lq1_ref[...] * lk1_ref[...])) - jnp.exp(jnp.sum(lq2_ref[...] * lk2_ref[...]))
           + lam_init)

    def finish(s, acc, l):
        o = acc * (1.0 / l)
        od = o[:, :tq] - lam * o[:, tq:]
        y = od * lax.rsqrt(jnp.mean(od * od, axis=0, keepdims=True) + EPS) * sw_ref[...]
        o_ref[0, pl.ds(s * tq, tq), :] = (y * (1.0 - lam_init)).T.astype(BF16)

    n_tiles = k_ref.shape[1] // tk

    @pl.when(fast_ref[0] == 1)
    def _():
        tiles = [(k_ref.at[0, pl.ds(t * tk, tk), :], v_ref.at[0, pl.ds(t * tk, tk), :])
                 for t in range(n_tiles)] + [(kc_ref.at[0], vc_ref.at[0])]

        def run(s, some_tiles, state):
            acc, lsum = state
            for kt_ref, vtt_ref in some_tiles:
                p = jnp.exp2(scores(s, kt_ref[...]))
                part = jnp.sum(p.reshape(p.shape[0] // V7X_SUBLANES, V7X_SUBLANES, p.shape[1]), axis=0)
                pv = values_t_dot(vtt_ref[...], p.astype(BF16))
                acc = pv if acc is None else acc + pv
                lsum = part if lsum is None else lsum + part
            return acc, lsum

        state = run(0, tiles, (None, None))
        for s in range(1, subs):
            started = run(s, tiles[:1], (None, None))
            finish(s - 1, state[0], jnp.sum(state[1], axis=0, keepdims=True))
            state = run(s, tiles[1:], started)
        finish(subs - 1, state[0], jnp.sum(state[1], axis=0, keepdims=True))

    @pl.when(fast_ref[0] != 1)
    def _():
        for s in range(subs):
            def update(carry, kt, vtt):
                m_prev, l_prev = carry
                sc = scores(s, kt)
                m_new = jnp.maximum(m_prev, jnp.max(sc, axis=0, keepdims=True))
                alpha = jnp.exp2(m_prev - m_new)
                p = jnp.exp2(sc - m_new)
                acc_ref[...] = alpha * acc_ref[...] + values_t_dot(vtt, p.astype(BF16))
                return m_new, alpha * l_prev + jnp.sum(p, axis=0, keepdims=True)

            def body(t, carry):
                off = pl.multiple_of(t * tk, tk)
                return update(carry, k_ref[0, pl.ds(off, tk), :], v_ref[0, pl.ds(off, tk), :])

            acc_ref[...] = jnp.zeros_like(acc_ref)
            init = (jnp.full((1, 2 * tq), -jnp.inf, F32), jnp.zeros((1, 2 * tq), F32))
            carry = lax.fori_loop(0, n_tiles, body, init)
            _, l = update(carry, kc_ref[0], vc_ref[0])
            finish(s, acc_ref[...], l)


def _diff_attention(fast, p, p_c, col0, lams, subln_w, lam_init, tq, tk, subs):
    B, L, _ = p.shape
    Lc = p_c.shape[1]
    H = ATTN_HEADS
    assert L % tk == 0
    lam_spec = pl.BlockSpec((1, HEAD_DIM), lambda b, h, i: (0, 0))
    keys = lambda rows, c0: pl.BlockSpec((1, rows, VALUE_DIM), lambda b, h, i: (b, 0, c0 + h))
    blocks = (2 * (2 * _nbytes((tq, VALUE_DIM), BF16) + 2 * _nbytes((L, VALUE_DIM), BF16)
                   + 2 * _nbytes((Lc, VALUE_DIM), BF16))
              + 3 * _nbytes((2 * tq, VALUE_DIM), F32) + 4 * _nbytes((2 * tq, tk), F32))
    return pl.pallas_call(
        functools.partial(_attn_kernel, tq=tq, tk=tk, lam_init=lam_init),
        grid=(B, H, L // (subs * tq)),
        in_specs=[
            pl.BlockSpec(memory_space=pltpu.SMEM),
            lam_spec, lam_spec, lam_spec, lam_spec,
            pl.BlockSpec((1, subs * tq, VALUE_DIM), lambda b, h, i: (b, i, col0["q"] + h)),
            keys(L, col0["k"]), keys(L, col0["v"]), keys(Lc, col0["kc"]), keys(Lc, col0["vc"]),
            pl.BlockSpec((VALUE_DIM, 1), lambda b, h, i: (0, 0)),
        ],
        out_specs=pl.BlockSpec((1, subs * tq, VALUE_DIM), lambda b, h, i: (b, i, h)),
        out_shape=jax.ShapeDtypeStruct((B, L, H * VALUE_DIM), BF16),
        scratch_shapes=[pltpu.VMEM((subs, 2 * tq, VALUE_DIM), BF16),
                        pltpu.VMEM((VALUE_DIM, 2 * tq), F32)],
        compiler_params=pltpu.CompilerParams(
            dimension_semantics=("parallel", "parallel", "parallel"),
            vmem_limit_bytes=_vmem_limit(blocks)),
        name="diff_attention",
    )(fast, *lams, p, p, p, p_c, p_c, subln_w)


def _mixer_kernel(heads_ref, u_ref, uprev_ref, unext_ref, g_ref, x_ref, gate_ref,
                  wa_ref, wb_ref, pw_ref, ps_ref, wo_ref, band_ref, o_ref, ubuf_ref, y_ref, *, tm, seq_len):
    i = pl.program_id(1)
    nt = pl.num_programs(1)
    D = x_ref.shape[2]
    group = u_ref.shape[2] // len(POOL_WINDOWS)

    y_a = jnp.dot(heads_ref[0], wa_ref[...], preferred_element_type=F32)

    ubuf_ref[pl.ds(POOL_HALO, tm), :] = u_ref[0]
    ubuf_ref[pl.ds(0, POOL_HALO), :] = jnp.where(i > 0, uprev_ref[0], jnp.zeros_like(uprev_ref[0]))
    ubuf_ref[pl.ds(POOL_HALO + tm, POOL_HALO), :] = jnp.where(i < nt - 1, unext_ref[0],
                                                             jnp.zeros_like(unext_ref[0]))

    t = i * tm + lax.broadcasted_iota(jnp.int32, (tm, 1), 0)
    totals = [jnp.dot(band_ref[gi], ubuf_ref[:, gi * group:(gi + 1) * group], preferred_element_type=F32)
              for gi in range(len(POOL_WINDOWS))]
    ds = []
    for gi, w in enumerate(POOL_WINDOWS):
        cols = slice(gi * group, (gi + 1) * group)
        lo = jnp.maximum(t - w // 2, 0)
        hi = jnp.minimum(t + (w - w // 2), seq_len)
        d = totals[gi] * (1.0 / (hi - lo).astype(F32)) - u_ref[0, :, cols].astype(F32)
        ds.append(d.astype(BF16))
    for gi in range(len(POOL_WINDOWS)):
        cols = slice(gi * group, (gi + 1) * group)
        yg = jnp.dot(ds[gi], pw_ref[gi], preferred_element_type=F32)
        y_ref[:, cols] = (yg * ps_ref[:, cols]).astype(BF16)

    y_b = jnp.dot(y_ref[...], wb_ref[...], preferred_element_type=F32)
    mixed = g_ref[0, :, :D].astype(F32) * y_a + g_ref[0, :, D:].astype(F32) * y_b
    mix = jnp.dot(mixed.astype(BF16), wo_ref[...], preferred_element_type=F32)
    o_ref[0] = x_ref[0] + gate_ref[0] * mix


def _token_mixer(heads, p_all, u_block, g_block, x, gate, w_a_up, w_b_up, pool_w, pool_scale, w_o, tm):
    B, L, D = x.shape
    AW, PW = heads.shape[2], w_b_up.shape[0]
    hb = tm // POOL_HALO
    offset = jnp.arange(tm + 2 * POOL_HALO)[None, :] - POOL_HALO - jnp.arange(tm)[:, None]
    band = jnp.stack([(offset >= -(w // 2)) & (offset < w - w // 2) for w in POOL_WINDOWS]).astype(BF16)
    n_halo = L // POOL_HALO
    const = lambda shape: pl.BlockSpec(shape, lambda b, i: (0,) * len(shape),
                                       pipeline_mode=pl.Buffered(1))
    weights = (_nbytes((AW, D), BF16) + _nbytes((PW, D), BF16) + _nbytes(pool_w.shape, BF16)
               + _nbytes((D, D), BF16))
    blocks = (weights + 2 * (_nbytes((tm, AW), BF16) + _nbytes((tm, PW), BF16)
                             + _nbytes((tm, 2 * D), BF16) + 2 * _nbytes((tm, D), F32))
              + _nbytes((tm + 2 * POOL_HALO, PW), F32) + _nbytes((tm, PW), BF16)
              + 4 * _nbytes((tm, D), F32))
    return pl.pallas_call(
        functools.partial(_mixer_kernel, tm=tm, seq_len=L),
        grid=(B, L // tm),
        in_specs=[
            pl.BlockSpec((1, tm, AW), lambda b, i: (b, i, 0)),
            pl.BlockSpec((1, tm, PW), lambda b, i: (b, i, u_block)),
            pl.BlockSpec((1, POOL_HALO, PW), lambda b, i: (b, jnp.maximum(i * hb - 1, 0), u_block)),
            pl.BlockSpec((1, POOL_HALO, PW), lambda b, i: (b, jnp.minimum((i + 1) * hb, n_halo - 1), u_block)),
            pl.BlockSpec((1, tm, 2 * D), lambda b, i: (b, i, g_block)),
            pl.BlockSpec((1, tm, D), lambda b, i: (b, i, 0)),
            pl.BlockSpec((1, 1, D), lambda b, i: (b, 0, 0)),
            const((AW, D)), const((PW, D)), const(pool_w.shape), const((1, PW)), const((D, D)),
            const(band.shape),
        ],
        out_specs=pl.BlockSpec((1, tm, D), lambda b, i: (b, i, 0)),
        out_shape=jax.ShapeDtypeStruct((B, L, D), F32),
        scratch_shapes=[pltpu.VMEM((tm + 2 * POOL_HALO, PW), BF16), pltpu.VMEM((tm, PW), BF16)],
        compiler_params=pltpu.CompilerParams(
            dimension_semantics=("parallel", "parallel"),
            vmem_limit_bytes=_vmem_limit(blocks)),
        name="token_mixer",
    )(heads, p_all, p_all, p_all, p_all, x, gate, w_a_up, w_b_up, pool_w, pool_scale, w_o, band)


def _mlp_kernel(x_ref, nw_ref, shift_ref, scale_ref, gate_ref, w1_ref, w2_ref, o_ref, h_ref):
    j = pl.program_id(2)
    tm, tf = h_ref.shape[0], w1_ref.shape[1]

    sub = min(tf, MLP_SUB)

    def ffn(h, rows, assign_first=False, finish=False):
        for s0 in range(0, tf, sub):
            a = jnp.maximum(jnp.dot(h, w1_ref[:, s0:s0 + sub], preferred_element_type=F32), 0.0)
            a2 = (a * a).astype(BF16)
            for n in range(0, o_ref.shape[2], sub):
                cols = slice(n, n + sub)
                part = jnp.dot(a2, w2_ref[s0:s0 + sub, cols], preferred_element_type=F32)
                if assign_first and s0 == 0:
                    o_ref[0, rows, cols] = part
                elif finish and s0 + sub == tf:
                    o_ref[0, rows, cols] = (x_ref[0, rows, cols]
                                            + gate_ref[0, :, cols] * (o_ref[0, rows, cols] + part))
                else:
                    o_ref[0, rows, cols] += part

    @pl.when(j == 0)
    def _():
        chunk = tm // 4
        for r0 in range(0, tm, chunk):
            rows = pl.ds(r0, chunk)
            h = _modulated_norm(x_ref[0, rows, :], nw_ref[...], shift_ref[0], scale_ref[0]).astype(BF16)
            h_ref[rows, :] = h
            ffn(h, rows, assign_first=True)

    last = pl.num_programs(2) - 1

    @pl.when((j > 0) & (j < last))
    def _():
        ffn(h_ref[...], pl.ds(0, tm))

    @pl.when(j == last)
    def _():
        ffn(h_ref[...], pl.ds(0, tm), finish=True)


def _mlp(x, norm_w, shift, scale, gate, w1, w2, tm, tf):
    B, L, D = x.shape
    F = w1.shape[1]
    assert F // tf >= 2
    vec = lambda: pl.BlockSpec((1, 1, D), lambda b, i, j: (b, 0, 0))
    blocks = (2 * (2 * _nbytes((tm, D), F32) + _nbytes((D, tf), BF16) + _nbytes((tf, D), BF16))
              + _nbytes((tm, D), BF16) + 3 * _nbytes((tm, min(tf, MLP_SUB)), F32))
    return pl.pallas_call(
        _mlp_kernel,
        grid=(B, L // tm, F // tf),
        in_specs=[
            pl.BlockSpec((1, tm, D), lambda b, i, j: (b, i, 0)),
            pl.BlockSpec((1, D), lambda b, i, j: (0, 0)),
            vec(), vec(), vec(),
            pl.BlockSpec((D, tf), lambda b, i, j: (0, j)),
            pl.BlockSpec((tf, D), lambda b, i, j: (j, 0)),
        ],
        out_specs=pl.BlockSpec((1, tm, D), lambda b, i, j: (b, i, 0)),
        out_shape=jax.ShapeDtypeStruct((B, L, D), F32),
        scratch_shapes=[pltpu.VMEM((tm, D), BF16)],
        compiler_params=pltpu.CompilerParams(
            dimension_semantics=("parallel", "parallel", "arbitrary"),
            vmem_limit_bytes=_vmem_limit(blocks)),
        name="sq_relu_mlp",
    )(x, norm_w, shift, scale, gate, w1, w2)


def _rope_tables(seq_len, rope):
    if not rope:
        ones = jnp.ones((seq_len, V7X_LANES), F32)
        return jnp.stack([ones, jnp.zeros_like(ones), jnp.zeros_like(ones)])
    t = jnp.arange(seq_len)
    row, col = t // GRID_W, t % GRID_W
    half = HEAD_DIM // 2
    quarter = half // 2
    inv_freq = ROPE_THETA ** (-jnp.arange(0, half, 2, dtype=F32) / half)
    lane = jnp.arange(V7X_LANES)
    e = lane % HEAD_DIM
    pos = jnp.where((e // half)[None, :] == 0, row[:, None], col[:, None]).astype(F32)
    ang = pos * inv_freq[e % quarter][None, :]
    cos, sin = jnp.cos(ang), jnp.sin(ang)
    second = ((e % half) >= quarter)[None, :]
    return jnp.stack([cos, jnp.where(second, sin, 0.0), jnp.where(second, 0.0, -sin)])


def _pick(n, pref):
    t = min(pref, n)
    while n % t:
        t //= 2
    return t


def kernel(x, c, ctx, c_ctx, w_mod, b_mod, norm_attn_w, w_in, q_norm_w, k_norm_w, lambda_q1, lambda_k1,
           lambda_q2, lambda_k2, subln_w, pool_w, pool_scale, w_a_up, w_b_up, w_o, norm_mlp_w, w_ff1, w_ff2):
    B, L, D = x.shape
    Lc = ctx.shape[1]
    depth = w_mod.shape[0]
    qk_w = 2 * ATTN_HEADS * HEAD_DIM
    tn = qk_w
    assert depth == 1 and w_in.shape[2] == 4 * tn + 2 * D and L % GRID_W == 0
    assert 2 * D == 2 * tn * 2 and pool_w.shape[1] * pool_w.shape[2] == tn

    for l in range(depth):
        lam_init = 0.8 - 0.6 * math.exp(-0.3 * l)

        rows = -(-(B + 1) // V7X_SUBLANES) * V7X_SUBLANES
        cvec = jnp.zeros((rows, D), F32).at[:B].set(c).at[B].set(c_ctx)
        mod = _modulation(cvec, w_mod[l], b_mod[l][None, :])
        sa, ca, ga, sm, cm, gm = [mod[:B, None, k * D:(k + 1) * D] for k in range(6)]
        sa_c, ca_c = [mod[B, k * D:(k + 1) * D].reshape(1, 1, D) for k in range(2)]

        w_in_bf = w_in[l].astype(BF16)
        qw128 = jnp.tile(q_norm_w[l], 2) * (HEAD_DIM ** -0.5 * math.log2(math.e))
        kw128 = jnp.tile(k_norm_w[l], 2)
        qk_w128 = jnp.stack([qw128, kw128])
        nw = norm_attn_w[l][None, :]

        seg_w = tn // 4
        seg = jnp.kron(jnp.eye(seg_w // HEAD_DIM, dtype=F32),
                       jnp.full((HEAD_DIM, HEAD_DIM), 1.0 / HEAD_DIM)).astype(BF16)
        pc = _input_projection(ctx.reshape(1, B * Lc, D), nw, sa_c, ca_c, w_in_bf, 1, ("k", "v"), seg,
                               qk_w128, _rope_tables(B * Lc, rope=False), tm=_pick(B * Lc, 1024), tn=tn, per_step=1)
        pc = pc.reshape(B, Lc, pc.shape[2])
        p = _input_projection(x, nw, sa, ca, w_in_bf, 0, ("q", "k", "v", "u", "g", "g", "g", "g"), seg,
                              qk_w128, _rope_tables(L, rope=True), tm=_pick(L, 1024), tn=tn, per_step=2)

        lams = [v[l][None, :] for v in (lambda_q1, lambda_k1, lambda_q2, lambda_k2)]
        score_bound = (HEAD_DIM * jnp.max(jnp.abs(qw128)) * jnp.max(jnp.abs(kw128))) * BF16_SLACK
        fast = (score_bound <= SCORE_BOUND).astype(jnp.int32).reshape(1)
        per_block = tn // VALUE_DIM
        col0 = {"q": 0, "k": per_block, "v": 2 * per_block, "kc": 0, "vc": per_block}
        heads = _diff_attention(fast, p, pc, col0, lams, subln_w[l][:, None], lam_init,
                                tq=_pick(L, 512), tk=_pick(L, 1024), subs=_pick(L // _pick(L, 512), 4))

        x = _token_mixer(heads, p, 3, 1, x, ga, w_a_up[l].astype(BF16), w_b_up[l].astype(BF16),
                         pool_w[l].astype(BF16), pool_scale[l][None, :], w_o[l].astype(BF16),
                         tm=_pick(L, 256))
        x = _mlp(x, norm_mlp_w[l][None, :], sm, cm, gm, w_ff1[l].astype(BF16), w_ff2[l].astype(BF16),
                 tm=_pick(L, 1024), tf=1024)
    return x
```

```python
import functools
import math

import jax
import jax.numpy as jnp
from jax import lax
from jax.experimental import pallas as pl
from jax.experimental.pallas import tpu as pltpu

F32 = jnp.float32
BF16 = jnp.bfloat16

ATTN_HEADS = 8
HEAD_DIM = 64
VALUE_DIM = 2 * HEAD_DIM
SCORE_BOUND = 64.0
BF16_SLACK = 1.0 + 2.0 ** -6
GRID_W = 64
ROPE_THETA = 10000.0
POOL_WINDOWS = (2, 4, 8, 16)
POOL_HALO = 16
MLP_SUB = 512
MIX_COLS = 512
EPS = 1e-6

V7X_LANES = 128
V7X_SUBLANES = 8
V7X_VMEM_BYTES = 64 * 1024 * 1024
V7X_VMEM_CAP = V7X_VMEM_BYTES - 6 * 1024 * 1024


def _vmem_limit(block_bytes):
    return int(min(V7X_VMEM_CAP, block_bytes * 1.25 + 8 * 1024 * 1024))


def _nbytes(shape, dtype):
    return math.prod(shape) * jnp.dtype(dtype).itemsize


def _mod_kernel(c_ref, w_ref, b_ref, o_ref):
    cv = c_ref[...]
    act = cv * jax.nn.sigmoid(cv)
    o_ref[...] = jnp.dot(act, w_ref[...], preferred_element_type=F32) + b_ref[...]


def _modulation(cvec, w_mod, b_mod, tn=1024):
    rows, d = cvec.shape
    n = w_mod.shape[1]
    blocks = 2 * (_nbytes((d, tn), F32) + _nbytes((rows, tn), F32) * 2) + _nbytes((rows, d), F32)
    return pl.pallas_call(
        _mod_kernel,
        grid=(n // tn,),
        in_specs=[
            pl.BlockSpec((rows, d), lambda j: (0, 0)),
            pl.BlockSpec((d, tn), lambda j: (0, j)),
            pl.BlockSpec((1, tn), lambda j: (0, j)),
        ],
        out_specs=pl.BlockSpec((rows, tn), lambda j: (0, j)),
        out_shape=jax.ShapeDtypeStruct((rows, n), F32),
        compiler_params=pltpu.CompilerParams(
            dimension_semantics=("parallel",), vmem_limit_bytes=_vmem_limit(blocks)),
        name="modulation",
    )(cvec, w_mod, b_mod)


def _modulated_norm(x, norm_w, shift, scale):
    y = x * lax.rsqrt(jnp.mean(x * x, axis=-1, keepdims=True) + EPS)
    return (y * norm_w) * (1.0 + scale) + shift


def _inproj_kernel(x_ref, nw_ref, shift_ref, scale_ref, w_ref, seg_ref, qkw_ref, tab_ref, o_ref, h_ref,
                   *, kinds, tn):
    j = pl.program_id(2)
    tm = o_ref.shape[1]
    per_step = o_ref.shape[2] // tn
    seg_w = seg_ref.shape[0]

    def qk_epilogue(which, first, c0):
        quarter = HEAD_DIM // 4
        chunk = tm // 4
        w128 = qkw_ref[which:which + 1, :]
        for r0 in range(0, tm, chunk):
            rows = pl.ds(r0, chunk)
            if first:
                h = _modulated_norm(x_ref[0, rows, :], nw_ref[...], shift_ref[0], scale_ref[0])
                h_ref[rows, :] = h.astype(BF16)
            cos, sin_prev, sin_next = (tab_ref[i, rows, :] for i in range(3))
            acc = jnp.dot(h_ref[rows, :], w_ref[:, c0:c0 + tn], preferred_element_type=F32)
            for s0 in range(0, tn, seg_w):
                a = acc[:, s0:s0 + seg_w]
                ms = jnp.dot((a * a).astype(BF16), seg_ref[...], preferred_element_type=F32)
                y = a * lax.rsqrt(ms + EPS)
                for c in range(0, seg_w, V7X_LANES):
                    yc = y[:, c:c + V7X_LANES] * w128
                    out = (yc * cos + pltpu.roll(yc, quarter, 1) * sin_prev
                           + pltpu.roll(yc, V7X_LANES - quarter, 1) * sin_next)
                    o_ref[0, rows, c0 + s0 + c:c0 + s0 + c + V7X_LANES] = out.astype(BF16)

    def elementwise_epilogue(fn, c0):
        acc = jnp.dot(h_ref[...], w_ref[:, c0:c0 + tn], preferred_element_type=F32)
        o_ref[0, :, c0:c0 + tn] = fn(acc).astype(BF16)

    def run_step(combo, first_step):
        for n, kind in enumerate(combo):
            if kind in ("q", "k"):
                qk_epilogue(("q", "k").index(kind), first_step and n == 0, n * tn)
            elif kind == "g":
                elementwise_epilogue(lambda z: 0.5 * jnp.tanh(0.5 * z) + 0.5, n * tn)
            else:
                elementwise_epilogue(lambda z: z, n * tn)

    steps = [tuple(kinds[s * per_step:(s + 1) * per_step]) for s in range(len(kinds) // per_step)]
    assert steps[0][0] in ("q", "k") and steps.count(steps[0]) == 1
    for combo in dict.fromkeys(steps):
        s_lo, n = steps.index(combo), steps.count(combo)
        assert steps[s_lo:s_lo + n] == [combo] * n
        pl.when((j >= s_lo) & (j < s_lo + n))(functools.partial(run_step, combo, s_lo == 0))


def _input_projection(x, norm_w, shift, scale, w_in_bf, col_block0, kinds, seg_mean, qk_w, tables, tm, tn,
                      per_step):
    B, L, D = x.shape
    wide = per_step * tn
    assert len(kinds) % per_step == 0
    vec = lambda: pl.BlockSpec((1, 1, D), lambda i, b, j: (b, 0, 0))
    blocks = (2 * (_nbytes((tm, D), F32) + _nbytes((D, wide), BF16) + _nbytes((3, tm, V7X_LANES), F32)
                   + _nbytes((tm, wide), BF16))
              + _nbytes((tm, D), BF16) + 2 * _nbytes((tm, tn), F32))
    return pl.pallas_call(
        functools.partial(_inproj_kernel, kinds=tuple(kinds), tn=tn),
        grid=(L // tm, B, len(kinds) // per_step),
        in_specs=[
            pl.BlockSpec((1, tm, D), lambda i, b, j: (b, i, 0)),
            pl.BlockSpec((1, D), lambda i, b, j: (0, 0)),
            vec(), vec(),
            pl.BlockSpec((D, wide), lambda i, b, j: (0, col_block0 + j)),
            pl.BlockSpec(seg_mean.shape, lambda i, b, j: (0, 0)),
            pl.BlockSpec(qk_w.shape, lambda i, b, j: (0, 0)),
            pl.BlockSpec((3, tm, V7X_LANES), lambda i, b, j: (0, i, 0)),
        ],
        out_specs=pl.BlockSpec((1, tm, wide), lambda i, b, j: (b, i, j)),
        out_shape=jax.ShapeDtypeStruct((B, L, tn * len(kinds)), BF16),
        scratch_shapes=[pltpu.VMEM((tm, D), BF16)],
        compiler_params=pltpu.CompilerParams(
            dimension_semantics=("parallel", "parallel", "arbitrary"),
            vmem_limit_bytes=_vmem_limit(blocks)),
        name="input_projection",
    )(x, norm_w, shift, scale, w_in_bf, seg_mean, qk_w, tables)


def _attn_kernel(fast_ref, lq1_ref, lk1_ref, lq2_ref, lk2_ref, q_ref, k_ref, v_ref, kc_ref, vc_ref,
                 sw_ref, o_ref, rhs_ref, acc_ref, *, tq, tk, lam_init):
    subs = q_ref.shape[1] // tq
    lane = lax.broadcasted_iota(jnp.int32, (tq, VALUE_DIM), 1)
    for s in range(subs):
        q = q_ref[0, pl.ds(s * tq, tq), :]
        zero = jnp.zeros_like(q)
        rhs_ref[s, :tq, :] = jnp.where(lane < HEAD_DIM, q, zero)
        rhs_ref[s, tq:, :] = jnp.where(lane >= HEAD_DIM, q, zero)

    def scores(s, kt):
        return lax.dot_general(kt, rhs_ref[s], (((1,), (1,)), ((), ())), preferred_element_type=F32)

    def values_t_dot(v, p):
        return lax.dot_general(v, p, (((0,), (0,)), ((), ())), preferred_element_type=F32)

    lam = (jnp.exp(jnp.sum(lq1_ref[...] * lk1_ref[...])) - jnp.exp(jnp.sum(lq2_ref[...] * lk2_ref[...]))
           + lam_init)

    def finish(s, acc, l):
        o = acc * (1.0 / l)
        od = o[:, :tq] - lam * o[:, tq:]
        y = od * lax.rsqrt(jnp.mean(od * od, axis=0, keepdims=True) + EPS) * sw_ref[...]
        o_ref[0, pl.ds(s * tq, tq), :] = (y * (1.0 - lam_init)).T.astype(BF16)

    n_tiles = k_ref.shape[1] // tk

    @pl.when(fast_ref[0] == 1)
    def _():
        tiles = [(k_ref.at[0, pl.ds(t * tk, tk), :], v_ref.at[0, pl.ds(t * tk, tk), :])
                 for t in range(n_tiles)] + [(kc_ref.at[0], vc_ref.at[0])]

        def run(s, some_tiles, state):
            acc, lsum = state
            for kt_ref, vtt_ref in some_tiles:
                p = jnp.exp2(scores(s, kt_ref[...]))
                part = jnp.sum(p.reshape(p.shape[0] // V7X_SUBLANES, V7X_SUBLANES, p.shape[1]), axis=0)
                pv = values_t_dot(vtt_ref[...], p.astype(BF16))
                acc = pv if acc is None else acc + pv
                lsum = part if lsum is None else lsum + part
            return acc, lsum

        state = run(0, tiles, (None, None))
        for s in range(1, subs):
            started = run(s, tiles[:1], (None, None))
            finish(s - 1, state[0], jnp.sum(state[1], axis=0, keepdims=True))
            state = run(s, tiles[1:], started)
        finish(subs - 1, state[0], jnp.sum(state[1], axis=0, keepdims=True))

    @pl.when(fast_ref[0] != 1)
    def _():
        for s in range(subs):
            def update(carry, kt, vtt):
                m_prev, l_prev = carry
                sc = scores(s, kt)
                m_new = jnp.maximum(m_prev, jnp.max(sc, axis=0, keepdims=True))
                alpha = jnp.exp2(m_prev - m_new)
                p = jnp.exp2(sc - m_new)
                acc_ref[...] = alpha * acc_ref[...] + values_t_dot(vtt, p.astype(BF16))
                return m_new, alpha * l_prev + jnp.sum(p, axis=0, keepdims=True)

            def body(t, carry):
                off = pl.multiple_of(t * tk, tk)
                return update(carry, k_ref[0, pl.ds(off, tk), :], v_ref[0, pl.ds(off, tk), :])

            acc_ref[...] = jnp.zeros_like(acc_ref)
            init = (jnp.full((1, 2 * tq), -jnp.inf, F32), jnp.zeros((1, 2 * tq), F32))
            carry = lax.fori_loop(0, n_tiles, body, init)
            _, l = update(carry, kc_ref[0], vc_ref[0])
            finish(s, acc_ref[...], l)


def _diff_attention(fast, p, p_c, col0, lams, subln_w, lam_init, tq, tk, subs):
    B, L, _ = p.shape
    Lc = p_c.shape[1]
    H = ATTN_HEADS
    assert L % tk == 0
    lam_spec = pl.BlockSpec((1, HEAD_DIM), lambda b, h, i: (0, 0))
    keys = lambda rows, c0: pl.BlockSpec((1, rows, VALUE_DIM), lambda b, h, i: (b, 0, c0 + h))
    blocks = (2 * (2 * _nbytes((tq, VALUE_DIM), BF16) + 2 * _nbytes((L, VALUE_DIM), BF16)
                   + 2 * _nbytes((Lc, VALUE_DIM), BF16))
              + 3 * _nbytes((2 * tq, VALUE_DIM), F32) + 4 * _nbytes((2 * tq, tk), F32))
    return pl.pallas_call(
        functools.partial(_attn_kernel, tq=tq, tk=tk, lam_init=lam_init),
        grid=(B, H, L // (subs * tq)),
        in_specs=[
            pl.BlockSpec(memory_space=pltpu.SMEM),
            lam_spec, lam_spec, lam_spec, lam_spec,
            pl.BlockSpec((1, subs * tq, VALUE_DIM), lambda b, h, i: (b, i, col0["q"] + h)),
            keys(L, col0["k"]), keys(L, col0["v"]), keys(Lc, col0["kc"]), keys(Lc, col0["vc"]),
            pl.BlockSpec((VALUE_DIM, 1), lambda b, h, i: (0, 0)),
        ],
        out_specs=pl.BlockSpec((1, subs * tq, VALUE_DIM), lambda b, h, i: (b, i, h)),
        out_shape=jax.ShapeDtypeStruct((B, L, H * VALUE_DIM), BF16),
        scratch_shapes=[pltpu.VMEM((subs, 2 * tq, VALUE_DIM), BF16),
                        pltpu.VMEM((VALUE_DIM, 2 * tq), F32)],
        compiler_params=pltpu.CompilerParams(
            dimension_semantics=("parallel", "parallel", "parallel"),
            vmem_limit_bytes=_vmem_limit(blocks)),
        name="diff_attention",
    )(fast, *lams, p, p, p, p_c, p_c, subln_w)


def _mixer_kernel(heads_ref, u_ref, uprev_ref, unext_ref, g_ref, x_ref, gate_ref,
                  wa_ref, wb_ref, pw_ref, ps_ref, wo_ref, band_ref, o_ref, ubuf_ref, y_ref, mixed_ref,
                  *, tm, seq_len):
    i = pl.program_id(1)
    nt = pl.num_programs(1)
    D = x_ref.shape[2]
    group = u_ref.shape[2] // len(POOL_WINDOWS)

    ubuf_ref[pl.ds(POOL_HALO, tm), :] = u_ref[0]
    ubuf_ref[pl.ds(0, POOL_HALO), :] = jnp.where(i > 0, uprev_ref[0], jnp.zeros_like(uprev_ref[0]))
    ubuf_ref[pl.ds(POOL_HALO + tm, POOL_HALO), :] = jnp.where(i < nt - 1, unext_ref[0],
                                                             jnp.zeros_like(unext_ref[0]))

    t = i * tm + lax.broadcasted_iota(jnp.int32, (tm, 1), 0)
    totals = [jnp.dot(band_ref[gi], ubuf_ref[:, gi * group:(gi + 1) * group], preferred_element_type=F32)
              for gi in range(len(POOL_WINDOWS))]
    ds = []
    for gi, w in enumerate(POOL_WINDOWS):
        cols = slice(gi * group, (gi + 1) * group)
        lo = jnp.maximum(t - w // 2, 0)
        hi = jnp.minimum(t + (w - w // 2), seq_len)
        d = totals[gi] * (1.0 / (hi - lo).astype(F32)) - u_ref[0, :, cols].astype(F32)
        ds.append(d.astype(BF16))
    for gi in range(len(POOL_WINDOWS)):
        cols = slice(gi * group, (gi + 1) * group)
        yg = jnp.dot(ds[gi], pw_ref[gi], preferred_element_type=F32)
        y_ref[:, cols] = (yg * ps_ref[:, cols]).astype(BF16)

    for c0 in range(0, D, MIX_COLS):
        cols = slice(c0, c0 + MIX_COLS)
        y_a = jnp.dot(heads_ref[0], wa_ref[:, cols], preferred_element_type=F32)
        y_b = jnp.dot(y_ref[...], wb_ref[:, cols], preferred_element_type=F32)
        mixed = (g_ref[0, :, cols].astype(F32) * y_a
                 + g_ref[0, :, D + c0:D + c0 + MIX_COLS].astype(F32) * y_b)
        mixed_ref[:, cols] = mixed.astype(BF16)
    for c0 in range(0, D, MIX_COLS):
        cols = slice(c0, c0 + MIX_COLS)
        mix = jnp.dot(mixed_ref[...], wo_ref[:, cols], preferred_element_type=F32)
        o_ref[0, :, cols] = x_ref[0, :, cols] + gate_ref[0, :, cols] * mix


def _token_mixer(heads, p_all, u_block, g_block, x, gate, w_a_up, w_b_up, pool_w, pool_scale, w_o, tm):
    B, L, D = x.shape
    AW, PW = heads.shape[2], w_b_up.shape[0]
    hb = tm // POOL_HALO
    offset = jnp.arange(tm + 2 * POOL_HALO)[None, :] - POOL_HALO - jnp.arange(tm)[:, None]
    band = jnp.stack([(offset >= -(w // 2)) & (offset < w - w // 2) for w in POOL_WINDOWS]).astype(BF16)
    n_halo = L // POOL_HALO
    const = lambda shape: pl.BlockSpec(shape, lambda b, i: (0,) * len(shape),
                                       pipeline_mode=pl.Buffered(1))
    weights = (_nbytes((AW, D), BF16) + _nbytes((PW, D), BF16) + _nbytes(pool_w.shape, BF16)
               + _nbytes((D, D), BF16))
    blocks = (weights + 2 * (_nbytes((tm, AW), BF16) + _nbytes((tm, PW), BF16)
                             + _nbytes((tm, 2 * D), BF16) + 2 * _nbytes((tm, D), F32))
              + _nbytes((tm + 2 * POOL_HALO, PW), BF16) + _nbytes((tm, PW), BF16) + _nbytes((tm, D), BF16)
              + 2 * _nbytes((tm, PW), F32) + 3 * _nbytes((tm, MIX_COLS), F32))
    return pl.pallas_call(
        functools.partial(_mixer_kernel, tm=tm, seq_len=L),
        grid=(B, L // tm),
        in_specs=[
            pl.BlockSpec((1, tm, AW), lambda b, i: (b, i, 0)),
            pl.BlockSpec((1, tm, PW), lambda b, i: (b, i, u_block)),
            pl.BlockSpec((1, POOL_HALO, PW), lambda b, i: (b, jnp.maximum(i * hb - 1, 0), u_block)),
            pl.BlockSpec((1, POOL_HALO, PW), lambda b, i: (b, jnp.minimum((i + 1) * hb, n_halo - 1), u_block)),
            pl.BlockSpec((1, tm, 2 * D), lambda b, i: (b, i, g_block)),
            pl.BlockSpec((1, tm, D), lambda b, i: (b, i, 0)),
            pl.BlockSpec((1, 1, D), lambda b, i: (b, 0, 0)),
            const((AW, D)), const((PW, D)), const(pool_w.shape), const((1, PW)), const((D, D)),
            const(band.shape),
        ],
        out_specs=pl.BlockSpec((1, tm, D), lambda b, i: (b, i, 0)),
        out_shape=jax.ShapeDtypeStruct((B, L, D), F32),
        scratch_shapes=[pltpu.VMEM((tm + 2 * POOL_HALO, PW), BF16), pltpu.VMEM((tm, PW), BF16),
                        pltpu.VMEM((tm, D), BF16)],
        compiler_params=pltpu.CompilerParams(
            dimension_semantics=("parallel", "parallel"),
            vmem_limit_bytes=_vmem_limit(blocks)),
        name="token_mixer",
    )(heads, p_all, p_all, p_all, p_all, x, gate, w_a_up, w_b_up, pool_w, pool_scale, w_o, band)


def _mlp_kernel(x_ref, nw_ref, shift_ref, scale_ref, gate_ref, w1_ref, w2_ref, o_ref, h_ref):
    j = pl.program_id(2)
    tm, tf = h_ref.shape[0], w1_ref.shape[1]

    sub = min(tf, MLP_SUB)

    def ffn(h, rows, assign_first=False, finish=False):
        for s0 in range(0, tf, sub):
            a = jnp.maximum(jnp.dot(h, w1_ref[:, s0:s0 + sub], preferred_element_type=F32), 0.0)
            a2 = (a * a).astype(BF16)
            for n in range(0, o_ref.shape[2], sub):
                cols = slice(n, n + sub)
                part = jnp.dot(a2, w2_ref[s0:s0 + sub, cols], preferred_element_type=F32)
                if assign_first and s0 == 0:
                    o_ref[0, rows, cols] = part
                elif finish and s0 + sub == tf:
                    o_ref[0, rows, cols] = (x_ref[0, rows, cols]
                                            + gate_ref[0, :, cols] * (o_ref[0, rows, cols] + part))
                else:
                    o_ref[0, rows, cols] += part

    @pl.when(j == 0)
    def _():
        chunk = tm // 4
        for r0 in range(0, tm, chunk):
            rows = pl.ds(r0, chunk)
            h = _modulated_norm(x_ref[0, rows, :], nw_ref[...], shift_ref[0], scale_ref[0]).astype(BF16)
            h_ref[rows, :] = h
            ffn(h, rows, assign_first=True)

    last = pl.num_programs(2) - 1

    @pl.when((j > 0) & (j < last))
    def _():
        ffn(h_ref[...], pl.ds(0, tm))

    @pl.when(j == last)
    def _():
        ffn(h_ref[...], pl.ds(0, tm), finish=True)


def _mlp(x, norm_w, shift, scale, gate, w1, w2, tm, tf):
    B, L, D = x.shape
    F = w1.shape[1]
    assert F // tf >= 2
    vec = lambda: pl.BlockSpec((1, 1, D), lambda b, i, j: (b, 0, 0))
    blocks = (2 * (2 * _nbytes((tm, D), F32) + _nbytes((D, tf), BF16) + _nbytes((tf, D), BF16))
              + _nbytes((tm, D), BF16) + 3 * _nbytes((tm, min(tf, MLP_SUB)), F32))
    return pl.pallas_call(
        _mlp_kernel,
        grid=(B, L // tm, F // tf),
        in_specs=[
            pl.BlockSpec((1, tm, D), lambda b, i, j: (b, i, 0)),
            pl.BlockSpec((1, D), lambda b, i, j: (0, 0)),
            vec(), vec(), vec(),
            pl.BlockSpec((D, tf), lambda b, i, j: (0, j)),
            pl.BlockSpec((tf, D), lambda b, i, j: (j, 0)),
        ],
        out_specs=pl.BlockSpec((1, tm, D), lambda b, i, j: (b, i, 0)),
        out_shape=jax.ShapeDtypeStruct((B, L, D), F32),
        scratch_shapes=[pltpu.VMEM((tm, D), BF16)],
        compiler_params=pltpu.CompilerParams(
            dimension_semantics=("parallel", "parallel", "arbitrary"),
            vmem_limit_bytes=_vmem_limit(blocks)),
        name="sq_relu_mlp",
    )(x, norm_w, shift, scale, gate, w1, w2)


def _rope_tables(seq_len, rope):
    if not rope:
        ones = jnp.ones((seq_len, V7X_LANES), F32)
        return jnp.stack([ones, jnp.zeros_like(ones), jnp.zeros_like(ones)])
    t = jnp.arange(seq_len)
    row, col = t // GRID_W, t % GRID_W
    half = HEAD_DIM // 2
    quarter = half // 2
    inv_freq = ROPE_THETA ** (-jnp.arange(0, half, 2, dtype=F32) / half)
    lane = jnp.arange(V7X_LANES)
    e = lane % HEAD_DIM
    pos = jnp.where((e // half)[None, :] == 0, row[:, None], col[:, None]).astype(F32)
    ang = pos * inv_freq[e % quarter][None, :]
    cos, sin = jnp.cos(ang), jnp.sin(ang)
    second = ((e % half) >= quarter)[None, :]
    return jnp.stack([cos, jnp.where(second, sin, 0.0), jnp.where(second, 0.0, -sin)])


def _pick(n, pref):
    t = min(pref, n)
    while n % t:
        t //= 2
    return t


def kernel(x, c, ctx, c_ctx, w_mod, b_mod, norm_attn_w, w_in, q_norm_w, k_norm_w, lambda_q1, lambda_k1,
           lambda_q2, lambda_k2, subln_w, pool_w, pool_scale, w_a_up, w_b_up, w_o, norm_mlp_w, w_ff1, w_ff2):
    B, L, D = x.shape
    Lc = ctx.shape[1]
    depth = w_mod.shape[0]
    qk_w = 2 * ATTN_HEADS * HEAD_DIM
    tn = qk_w
    assert depth == 1 and w_in.shape[2] == 4 * tn + 2 * D and L % GRID_W == 0
    assert 2 * D == 2 * tn * 2 and pool_w.shape[1] * pool_w.shape[2] == tn

    for l in range(depth):
        lam_init = 0.8 - 0.6 * math.exp(-0.3 * l)

        rows = -(-(B + 1) // V7X_SUBLANES) * V7X_SUBLANES
        cvec = jnp.zeros((rows, D), F32).at[:B].set(c).at[B].set(c_ctx)
        mod = _modulation(cvec, w_mod[l], b_mod[l][None, :])
        sa, ca, ga, sm, cm, gm = [mod[:B, None, k * D:(k + 1) * D] for k in range(6)]
        sa_c, ca_c = [mod[B, k * D:(k + 1) * D].reshape(1, 1, D) for k in range(2)]

        w_in_bf = w_in[l].astype(BF16)
        qw128 = jnp.tile(q_norm_w[l], 2) * (HEAD_DIM ** -0.5 * math.log2(math.e))
        kw128 = jnp.tile(k_norm_w[l], 2)
        qk_w128 = jnp.stack([qw128, kw128])
        nw = norm_attn_w[l][None, :]

        seg_w = tn // 4
        seg = jnp.kron(jnp.eye(seg_w // HEAD_DIM, dtype=F32),
                       jnp.full((HEAD_DIM, HEAD_DIM), 1.0 / HEAD_DIM)).astype(BF16)
        pc = _input_projection(ctx.reshape(1, B * Lc, D), nw, sa_c, ca_c, w_in_bf, 1, ("k", "v"), seg,
                               qk_w128, _rope_tables(B * Lc, rope=False), tm=_pick(B * Lc, 1024), tn=tn, per_step=1)
        pc = pc.reshape(B, Lc, pc.shape[2])
        p = _input_projection(x, nw, sa, ca, w_in_bf, 0, ("q", "k", "v", "u", "g", "g", "g", "g"), seg,
                              qk_w128, _rope_tables(L, rope=True), tm=_pick(L, 1024), tn=tn, per_step=2)

        lams = [v[l][None, :] for v in (lambda_q1, lambda_k1, lambda_q2, lambda_k2)]
        score_bound = (HEAD_DIM * jnp.max(jnp.abs(qw128)) * jnp.max(jnp.abs(kw128))) * BF16_SLACK
        fast = (score_bound <= SCORE_BOUND).astype(jnp.int32).reshape(1)
        per_block = tn // VALUE_DIM
        col0 = {"q": 0, "k": per_block, "v": 2 * per_block, "kc": 0, "vc": per_block}
        heads = _diff_attention(fast, p, pc, col0, lams, subln_w[l][:, None], lam_init,
                                tq=_pick(L, 512), tk=_pick(L, 1024), subs=_pick(L // _pick(L, 512), 4))

        x = _token_mixer(heads, p, 3, 1, x, ga, w_a_up[l].astype(BF16), w_b_up[l].astype(BF16),
                         pool_w[l].astype(BF16), pool_scale[l][None, :], w_o[l].astype(BF16),
                         tm=_pick(L, 512))
        x = _mlp(x, norm_mlp_w[l][None, :], sm, cm, gm, w_ff1[l].astype(BF16), w_ff2[l].astype(BF16),
                 tm=_pick(L, 1024), tf=1024)
    return x
```

```python
import functools
import math

import jax
import jax.numpy as jnp
from jax import lax
from jax.experimental import pallas as pl
from jax.experimental.pallas import tpu as pltpu

F32 = jnp.float32
BF16 = jnp.bfloat16

ATTN_HEADS = 8
HEAD_DIM = 64
VALUE_DIM = 2 * HEAD_DIM
SCORE_BOUND = 64.0
BF16_SLACK = 1.0 + 2.0 ** -6
GRID_W = 64
ROPE_THETA = 10000.0
POOL_WINDOWS = (2, 4, 8, 16)
POOL_HALO = 16
MLP_SUB = 512
EPS = 1e-6

V7X_LANES = 128
V7X_SUBLANES = 8
V7X_VMEM_BYTES = 64 * 1024 * 1024
V7X_VMEM_CAP = V7X_VMEM_BYTES - 6 * 1024 * 1024


def _vmem_limit(block_bytes):
    return int(min(V7X_VMEM_CAP, block_bytes * 1.25 + 8 * 1024 * 1024))


def _nbytes(shape, dtype):
    return math.prod(shape) * jnp.dtype(dtype).itemsize


def _mod_kernel(c_ref, w_ref, b_ref, o_ref):
    cv = c_ref[...]
    act = cv * jax.nn.sigmoid(cv)
    o_ref[...] = jnp.dot(act, w_ref[...], preferred_element_type=F32) + b_ref[...]


def _modulation(cvec, w_mod, b_mod, tn=1024):
    rows, d = cvec.shape
    n = w_mod.shape[1]
    blocks = 2 * (_nbytes((d, tn), F32) + _nbytes((rows, tn), F32) * 2) + _nbytes((rows, d), F32)
    return pl.pallas_call(
        _mod_kernel,
        grid=(n // tn,),
        in_specs=[
            pl.BlockSpec((rows, d), lambda j: (0, 0)),
            pl.BlockSpec((d, tn), lambda j: (0, j)),
            pl.BlockSpec((1, tn), lambda j: (0, j)),
        ],
        out_specs=pl.BlockSpec((rows, tn), lambda j: (0, j)),
        out_shape=jax.ShapeDtypeStruct((rows, n), F32),
        compiler_params=pltpu.CompilerParams(
            dimension_semantics=("parallel",), vmem_limit_bytes=_vmem_limit(blocks)),
        name="modulation",
    )(cvec, w_mod, b_mod)


def _modulated_norm(x, norm_w, shift, scale):
    y = x * lax.rsqrt(jnp.mean(x * x, axis=-1, keepdims=True) + EPS)
    return (y * norm_w) * (1.0 + scale) + shift


def _inproj_kernel(x_ref, nw_ref, shift_ref, scale_ref, w_ref, seg_ref, qkw_ref, tab_ref, o_ref, h_ref,
                   *, kinds, tn):
    j = pl.program_id(2)
    tm = o_ref.shape[1]
    per_step = o_ref.shape[2] // tn
    seg_w = seg_ref.shape[0]

    def qk_epilogue(which, first, c0):
        quarter = HEAD_DIM // 4
        chunk = tm // 4
        w128 = qkw_ref[which:which + 1, :]
        for r0 in range(0, tm, chunk):
            rows = pl.ds(r0, chunk)
            if first:
                h = _modulated_norm(x_ref[0, rows, :], nw_ref[...], shift_ref[0], scale_ref[0])
                h_ref[rows, :] = h.astype(BF16)
            cos, sin_prev, sin_next = (tab_ref[i, rows, :] for i in range(3))
            acc = jnp.dot(h_ref[rows, :], w_ref[:, c0:c0 + tn], preferred_element_type=F32)
            for s0 in range(0, tn, seg_w):
                a = acc[:, s0:s0 + seg_w]
                ms = jnp.dot((a * a).astype(BF16), seg_ref[...], preferred_element_type=F32)
                y = a * lax.rsqrt(ms + EPS)
                for c in range(0, seg_w, V7X_LANES):
                    yc = y[:, c:c + V7X_LANES] * w128
                    out = (yc * cos + pltpu.roll(yc, quarter, 1) * sin_prev
                           + pltpu.roll(yc, V7X_LANES - quarter, 1) * sin_next)
                    o_ref[0, rows, c0 + s0 + c:c0 + s0 + c + V7X_LANES] = out.astype(BF16)

    def elementwise_epilogue(fn, c0):
        acc = jnp.dot(h_ref[...], w_ref[:, c0:c0 + tn], preferred_element_type=F32)
        o_ref[0, :, c0:c0 + tn] = fn(acc).astype(BF16)

    def run_step(combo, first_step):
        for n, kind in enumerate(combo):
            if kind in ("q", "k"):
                qk_epilogue(("q", "k").index(kind), first_step and n == 0, n * tn)
            elif kind == "g":
                elementwise_epilogue(lambda z: 0.5 * jnp.tanh(0.5 * z) + 0.5, n * tn)
            else:
                elementwise_epilogue(lambda z: z, n * tn)

    steps = [tuple(kinds[s * per_step:(s + 1) * per_step]) for s in range(len(kinds) // per_step)]
    assert steps[0][0] in ("q", "k") and steps.count(steps[0]) == 1
    for combo in dict.fromkeys(steps):
        s_lo, n = steps.index(combo), steps.count(combo)
        assert steps[s_lo:s_lo + n] == [combo] * n
        pl.when((j >= s_lo) & (j < s_lo + n))(functools.partial(run_step, combo, s_lo == 0))


def _input_projection(x, norm_w, shift, scale, w_in_bf, col_block0, kinds, seg_mean, qk_w, tables, tm, tn,
                      per_step):
    B, L, D = x.shape
    wide = per_step * tn
    assert len(kinds) % per_step == 0
    vec = lambda: pl.BlockSpec((1, 1, D), lambda i, b, j: (b, 0, 0))
    blocks = (2 * (_nbytes((tm, D), F32) + _nbytes((D, wide), BF16) + _nbytes((3, tm, V7X_LANES), F32)
                   + _nbytes((tm, wide), BF16))
              + _nbytes((tm, D), BF16) + 2 * _nbytes((tm, tn), F32))
    return pl.pallas_call(
        functools.partial(_inproj_kernel, kinds=tuple(kinds), tn=tn),
        grid=(L // tm, B, len(kinds) // per_step),
        in_specs=[
            pl.BlockSpec((1, tm, D), lambda i, b, j: (b, i, 0)),
            pl.BlockSpec((1, D), lambda i, b, j: (0, 0)),
            vec(), vec(),
            pl.BlockSpec((D, wide), lambda i, b, j: (0, col_block0 + j)),
            pl.BlockSpec(seg_mean.shape, lambda i, b, j: (0, 0)),
            pl.BlockSpec(qk_w.shape, lambda i, b, j: (0, 0)),
            pl.BlockSpec((3, tm, V7X_LANES), lambda i, b, j: (0, i, 0)),
        ],
        out_specs=pl.BlockSpec((1, tm, wide), lambda i, b, j: (b, i, j)),
        out_shape=jax.ShapeDtypeStruct((B, L, tn * len(kinds)), BF16),
        scratch_shapes=[pltpu.VMEM((tm, D), BF16)],
        compiler_params=pltpu.CompilerParams(
            dimension_semantics=("parallel", "parallel", "arbitrary"),
            vmem_limit_bytes=_vmem_limit(blocks)),
        name="input_projection",
    )(x, norm_w, shift, scale, w_in_bf, seg_mean, qk_w, tables)


def _attn_kernel(fast_ref, lq1_ref, lk1_ref, lq2_ref, lk2_ref, q_ref, k_ref, v_ref, kc_ref, vc_ref,
                 sw_ref, o_ref, rhs_ref, acc_ref, *, tq, tk, lam_init):
    subs = q_ref.shape[1] // tq
    lane = lax.broadcasted_iota(jnp.int32, (tq, VALUE_DIM), 1)
    for s in range(subs):
        q = q_ref[0, pl.ds(s * tq, tq), :]
        zero = jnp.zeros_like(q)
        rhs_ref[s, :tq, :] = jnp.where(lane < HEAD_DIM, q, zero)
        rhs_ref[s, tq:, :] = jnp.where(lane >= HEAD_DIM, q, zero)

    def scores(s, kt):
        return lax.dot_general(kt, rhs_ref[s], (((1,), (1,)), ((), ())), preferred_element_type=F32)

    def values_t_dot(v, p):
        return lax.dot_general(v, p, (((0,), (0,)), ((), ())), preferred_element_type=F32)

    lam = (jnp.exp(jnp.sum(lq1_ref[...] * lk1_ref[...])) - jnp.exp(jnp.sum(lq2_ref[...] * lk2_ref[...]))
           + lam_init)

    def finish(s, acc, l):
        o = acc * (1.0 / l)
        od = o[:, :tq] - lam * o[:, tq:]
        y = od * lax.rsqrt(jnp.mean(od * od, axis=0, keepdims=True) + EPS) * sw_ref[...]
        o_ref[0, pl.ds(s * tq, tq), :] = (y * (1.0 - lam_init)).T.astype(BF16)

    n_tiles = k_ref.shape[1] // tk

    @pl.when(fast_ref[0] == 1)
    def _():
        tiles = [(k_ref.at[0, pl.ds(t * tk, tk), :], v_ref.at[0, pl.ds(t * tk, tk), :])
                 for t in range(n_tiles)] + [(kc_ref.at[0], vc_ref.at[0])]

        def run(s, some_tiles, state):
            acc, lsum = state
            for kt_ref, vtt_ref in some_tiles:
                p = jnp.exp2(scores(s, kt_ref[...]))
                part = jnp.sum(p.reshape(p.shape[0] // V7X_SUBLANES, V7X_SUBLANES, p.shape[1]), axis=0)
                pv = values_t_dot(vtt_ref[...], p.astype(BF16))
                acc = pv if acc is None else acc + pv
                lsum = part if lsum is None else lsum + part
            return acc, lsum

        state = run(0, tiles, (None, None))
        for s in range(1, subs):
            started = run(s, tiles[:1], (None, None))
            finish(s - 1, state[0], jnp.sum(state[1], axis=0, keepdims=True))
            state = run(s, tiles[1:], started)
        finish(subs - 1, state[0], jnp.sum(state[1], axis=0, keepdims=True))

    @pl.when(fast_ref[0] != 1)
    def _():
        for s in range(subs):
            def update(carry, kt, vtt):
                m_prev, l_prev = carry
                sc = scores(s, kt)
                m_new = jnp.maximum(m_prev, jnp.max(sc, axis=0, keepdims=True))
                alpha = jnp.exp2(m_prev - m_new)
                p = jnp.exp2(sc - m_new)
                acc_ref[...] = alpha * acc_ref[...] + values_t_dot(vtt, p.astype(BF16))
                return m_new, alpha * l_prev + jnp.sum(p, axis=0, keepdims=True)

            def body(t, carry):
                off = pl.multiple_of(t * tk, tk)
                return update(carry, k_ref[0, pl.ds(off, tk), :], v_ref[0, pl.ds(off, tk), :])

            acc_ref[...] = jnp.zeros_like(acc_ref)
            init = (jnp.full((1, 2 * tq), -jnp.inf, F32), jnp.zeros((1, 2 * tq), F32))
            carry = lax.fori_loop(0, n_tiles, body, init)
            _, l = update(carry, kc_ref[0], vc_ref[0])
            finish(s, acc_ref[...], l)


def _diff_attention(fast, p, p_c, col0, lams, subln_w, lam_init, tq, tk, subs):
    B, L, _ = p.shape
    Lc = p_c.shape[1]
    H = ATTN_HEADS
    assert L % tk == 0
    lam_spec = pl.BlockSpec((1, HEAD_DIM), lambda b, h, i: (0, 0))
    keys = lambda rows, c0: pl.BlockSpec((1, rows, VALUE_DIM), lambda b, h, i: (b, 0, c0 + h))
    blocks = (2 * (2 * _nbytes((tq, VALUE_DIM), BF16) + 2 * _nbytes((L, VALUE_DIM), BF16)
                   + 2 * _nbytes((Lc, VALUE_DIM), BF16))
              + 3 * _nbytes((2 * tq, VALUE_DIM), F32) + 4 * _nbytes((2 * tq, tk), F32))
    return pl.pallas_call(
        functools.partial(_attn_kernel, tq=tq, tk=tk, lam_init=lam_init),
        grid=(B, H, L // (subs * tq)),
        in_specs=[
            pl.BlockSpec(memory_space=pltpu.SMEM),
            lam_spec, lam_spec, lam_spec, lam_spec,
            pl.BlockSpec((1, subs * tq, VALUE_DIM), lambda b, h, i: (b, i, col0["q"] + h)),
            keys(L, col0["k"]), keys(L, col0["v"]), keys(Lc, col0["kc"]), keys(Lc, col0["vc"]),
            pl.BlockSpec((VALUE_DIM, 1), lambda b, h, i: (0, 0)),
        ],
        out_specs=pl.BlockSpec((1, subs * tq, VALUE_DIM), lambda b, h, i: (b, i, h)),
        out_shape=jax.ShapeDtypeStruct((B, L, H * VALUE_DIM), BF16),
        scratch_shapes=[pltpu.VMEM((subs, 2 * tq, VALUE_DIM), BF16),
                        pltpu.VMEM((VALUE_DIM, 2 * tq), F32)],
        compiler_params=pltpu.CompilerParams(
            dimension_semantics=("parallel", "parallel", "parallel"),
            vmem_limit_bytes=_vmem_limit(blocks)),
        name="diff_attention",
    )(fast, *lams, p, p, p, p_c, p_c, subln_w)


def _mixer_kernel(heads_ref, u_ref, uprev_ref, unext_ref, g_ref, x_ref, gate_ref,
                  wa_ref, wb_ref, pw_ref, ps_ref, wo_ref, band_ref, o_ref, ubuf_ref, y_ref, *, tm, seq_len):
    i = pl.program_id(1)
    nt = pl.num_programs(1)
    D = x_ref.shape[2]
    group = u_ref.shape[2] // len(POOL_WINDOWS)

    y_a = jnp.dot(heads_ref[0], wa_ref[...], preferred_element_type=F32)

    ubuf_ref[pl.ds(POOL_HALO, tm), :] = u_ref[0]
    ubuf_ref[pl.ds(0, POOL_HALO), :] = jnp.where(i > 0, uprev_ref[0], jnp.zeros_like(uprev_ref[0]))
    ubuf_ref[pl.ds(POOL_HALO + tm, POOL_HALO), :] = jnp.where(i < nt - 1, unext_ref[0],
                                                             jnp.zeros_like(unext_ref[0]))

    t = i * tm + lax.broadcasted_iota(jnp.int32, (tm, 1), 0)
    totals = [jnp.dot(band_ref[gi], ubuf_ref[:, gi * group:(gi + 1) * group], preferred_element_type=F32)
              for gi in range(len(POOL_WINDOWS))]
    ds = []
    for gi, w in enumerate(POOL_WINDOWS):
        cols = slice(gi * group, (gi + 1) * group)
        lo = jnp.maximum(t - w // 2, 0)
        hi = jnp.minimum(t + (w - w // 2), seq_len)
        d = totals[gi] * (1.0 / (hi - lo).astype(F32)) - u_ref[0, :, cols].astype(F32)
        ds.append(d.astype(BF16))
    for gi in range(len(POOL_WINDOWS)):
        cols = slice(gi * group, (gi + 1) * group)
        yg = jnp.dot(ds[gi], pw_ref[gi], preferred_element_type=F32)
        y_ref[:, cols] = (yg * ps_ref[:, cols]).astype(BF16)

    y_b = jnp.dot(y_ref[...], wb_ref[...], preferred_element_type=F32)
    mixed = g_ref[0, :, :D].astype(F32) * y_a + g_ref[0, :, D:].astype(F32) * y_b
    mix = jnp.dot(mixed.astype(BF16), wo_ref[...], preferred_element_type=F32)
    o_ref[0] = x_ref[0] + gate_ref[0] * mix


def _token_mixer(heads, p_all, u_block, g_block, x, gate, w_a_up, w_b_up, pool_w, pool_scale, w_o, tm):
    B, L, D = x.shape
    AW, PW = heads.shape[2], w_b_up.shape[0]
    hb = tm // POOL_HALO
    offset = jnp.arange(tm + 2 * POOL_HALO)[None, :] - POOL_HALO - jnp.arange(tm)[:, None]
    band = jnp.stack([(offset >= -(w // 2)) & (offset < w - w // 2) for w in POOL_WINDOWS]).astype(BF16)
    n_halo = L // POOL_HALO
    const = lambda shape: pl.BlockSpec(shape, lambda b, i: (0,) * len(shape),
                                       pipeline_mode=pl.Buffered(1))
    weights = (_nbytes((AW, D), BF16) + _nbytes((PW, D), BF16) + _nbytes(pool_w.shape, BF16)
               + _nbytes((D, D), BF16))
    blocks = (weights + 2 * (_nbytes((tm, AW), BF16) + _nbytes((tm, PW), BF16)
                             + _nbytes((tm, 2 * D), BF16) + 2 * _nbytes((tm, D), F32))
              + _nbytes((tm + 2 * POOL_HALO, PW), F32) + _nbytes((tm, PW), BF16)
              + 4 * _nbytes((tm, D), F32))
    return pl.pallas_call(
        functools.partial(_mixer_kernel, tm=tm, seq_len=L),
        grid=(B, L // tm),
        in_specs=[
            pl.BlockSpec((1, tm, AW), lambda b, i: (b, i, 0)),
            pl.BlockSpec((1, tm, PW), lambda b, i: (b, i, u_block)),
            pl.BlockSpec((1, POOL_HALO, PW), lambda b, i: (b, jnp.maximum(i * hb - 1, 0), u_block)),
            pl.BlockSpec((1, POOL_HALO, PW), lambda b, i: (b, jnp.minimum((i + 1) * hb, n_halo - 1), u_block)),
            pl.BlockSpec((1, tm, 2 * D), lambda b, i: (b, i, g_block)),
            pl.BlockSpec((1, tm, D), lambda b, i: (b, i, 0)),
            pl.BlockSpec((1, 1, D), lambda b, i: (b, 0, 0)),
            const((AW, D)), const((PW, D)), const(pool_w.shape), const((1, PW)), const((D, D)),
            const(band.shape),
        ],
        out_specs=pl.BlockSpec((1, tm, D), lambda b, i: (b, i, 0)),
        out_shape=jax.ShapeDtypeStruct((B, L, D), F32),
        scratch_shapes=[pltpu.VMEM((tm + 2 * POOL_HALO, PW), BF16), pltpu.VMEM((tm, PW), BF16)],
        compiler_params=pltpu.CompilerParams(
            dimension_semantics=("parallel", "parallel"),
            vmem_limit_bytes=_vmem_limit(blocks)),
        name="token_mixer",
    )(heads, p_all, p_all, p_all, p_all, x, gate, w_a_up, w_b_up, pool_w, pool_scale, w_o, band)


def _mlp_kernel(x_ref, nw_ref, shift_ref, scale_ref, gate_ref, w1_ref, w2_ref, o_ref, h_ref):
    j = pl.program_id(2)
    tm, tf = h_ref.shape[0], w1_ref.shape[1]

    sub = min(tf, MLP_SUB)

    def ffn(h, rows, assign_first=False, finish=False):
        for s0 in range(0, tf, sub):
            a = jnp.maximum(jnp.dot(h, w1_ref[:, s0:s0 + sub], preferred_element_type=F32), 0.0)
            a2 = (a * a).astype(BF16)
            for n in range(0, o_ref.shape[2], sub):
                cols = slice(n, n + sub)
                part = jnp.dot(a2, w2_ref[s0:s0 + sub, cols], preferred_element_type=F32)
                if assign_first and s0 == 0:
                    o_ref[0, rows, cols] = part
                elif finish and s0 + sub == tf:
                    o_ref[0, rows, cols] = (x_ref[0, rows, cols]
                                            + gate_ref[0, :, cols] * (o_ref[0, rows, cols] + part))
                else:
                    o_ref[0, rows, cols] += part

    @pl.when(j == 0)
    def _():
        chunk = tm // 4
        for r0 in range(0, tm, chunk):
            rows = pl.ds(r0, chunk)
            h = _modulated_norm(x_ref[0, rows, :], nw_ref[...], shift_ref[0], scale_ref[0]).astype(BF16)
            h_ref[rows, :] = h
            ffn(h, rows, assign_first=True)

    last = pl.num_programs(2) - 1

    @pl.when((j > 0) & (j < last))
    def _():
        ffn(h_ref[...], pl.ds(0, tm))

    @pl.when(j == last)
    def _():
        ffn(h_ref[...], pl.ds(0, tm), finish=True)


def _mlp(x, norm_w, shift, scale, gate, w1, w2, tm, tf):
    B, L, D = x.shape
    F = w1.shape[1]
    assert F // tf >= 2
    vec = lambda: pl.BlockSpec((1, 1, D), lambda b, i, j: (b, 0, 0))
    blocks = (2 * (2 * _nbytes((tm, D), F32) + _nbytes((D, tf), BF16) + _nbytes((tf, D), BF16))
              + _nbytes((tm, D), BF16) + 3 * _nbytes((tm, min(tf, MLP_SUB)), F32))
    return pl.pallas_call(
        _mlp_kernel,
        grid=(B, L // tm, F // tf),
        in_specs=[
            pl.BlockSpec((1, tm, D), lambda b, i, j: (b, i, 0)),
            pl.BlockSpec((1, D), lambda b, i, j: (0, 0)),
            vec(), vec(), vec(),
            pl.BlockSpec((D, tf), lambda b, i, j: (0, j)),
            pl.BlockSpec((tf, D), lambda b, i, j: (j, 0)),
        ],
        out_specs=pl.BlockSpec((1, tm, D), lambda b, i, j: (b, i, 0)),
        out_shape=jax.ShapeDtypeStruct((B, L, D), F32),
        scratch_shapes=[pltpu.VMEM((tm, D), BF16)],
        compiler_params=pltpu.CompilerParams(
            dimension_semantics=("parallel", "parallel", "arbitrary"),
            vmem_limit_bytes=_vmem_limit(blocks)),
        name="sq_relu_mlp",
    )(x, norm_w, shift, scale, gate, w1, w2)


def _rope_tables(seq_len, rope):
    if not rope:
        ones = jnp.ones((seq_len, V7X_LANES), F32)
        return jnp.stack([ones, jnp.zeros_like(ones), jnp.zeros_like(ones)])
    t = jnp.arange(seq_len)
    row, col = t // GRID_W, t % GRID_W
    half = HEAD_DIM // 2
    quarter = half // 2
    inv_freq = ROPE_THETA ** (-jnp.arange(0, half, 2, dtype=F32) / half)
    lane = jnp.arange(V7X_LANES)
    e = lane % HEAD_DIM
    pos = jnp.where((e // half)[None, :] == 0, row[:, None], col[:, None]).astype(F32)
    ang = pos * inv_freq[e % quarter][None, :]
    cos, sin = jnp.cos(ang), jnp.sin(ang)
    second = ((e % half) >= quarter)[None, :]
    return jnp.stack([cos, jnp.where(second, sin, 0.0), jnp.where(second, 0.0, -sin)])


def _pick(n, pref):
    t = min(pref, n)
    while n % t:
        t //= 2
    return t


def kernel(x, c, ctx, c_ctx, w_mod, b_mod, norm_attn_w, w_in, q_norm_w, k_norm_w, lambda_q1, lambda_k1,
           lambda_q2, lambda_k2, subln_w, pool_w, pool_scale, w_a_up, w_b_up, w_o, norm_mlp_w, w_ff1, w_ff2):
    B, L, D = x.shape
    Lc = ctx.shape[1]
    depth = w_mod.shape[0]
    qk_w = 2 * ATTN_HEADS * HEAD_DIM
    tn = qk_w
    assert depth == 1 and w_in.shape[2] == 4 * tn + 2 * D and L % GRID_W == 0
    assert 2 * D == 2 * tn * 2 and pool_w.shape[1] * pool_w.shape[2] == tn

    for l in range(depth):
        lam_init = 0.8 - 0.6 * math.exp(-0.3 * l)

        rows = -(-(B + 1) // V7X_SUBLANES) * V7X_SUBLANES
        cvec = jnp.zeros((rows, D), F32).at[:B].set(c).at[B].set(c_ctx)
        mod = _modulation(cvec, w_mod[l], b_mod[l][None, :])
        sa, ca, ga, sm, cm, gm = [mod[:B, None, k * D:(k + 1) * D] for k in range(6)]
        sa_c, ca_c = [mod[B, k * D:(k + 1) * D].reshape(1, 1, D) for k in range(2)]

        w_in_bf = w_in[l].astype(BF16)
        qw128 = jnp.tile(q_norm_w[l], 2) * (HEAD_DIM ** -0.5 * math.log2(math.e))
        kw128 = jnp.tile(k_norm_w[l], 2)
        qk_w128 = jnp.stack([qw128, kw128])
        nw = norm_attn_w[l][None, :]

        seg_w = tn // 4
        seg = jnp.kron(jnp.eye(seg_w // HEAD_DIM, dtype=F32),
                       jnp.full((HEAD_DIM, HEAD_DIM), 1.0 / HEAD_DIM)).astype(BF16)
        pc = _input_projection(ctx.reshape(1, B * Lc, D), nw, sa_c, ca_c, w_in_bf, 1, ("k", "v"), seg,
                               qk_w128, _rope_tables(B * Lc, rope=False), tm=_pick(B * Lc, 1024), tn=tn, per_step=1)
        pc = pc.reshape(B, Lc, pc.shape[2])
        p = _input_projection(x, nw, sa, ca, w_in_bf, 0, ("q", "k", "v", "u", "g", "g", "g", "g"), seg,
                              qk_w128, _rope_tables(L, rope=True), tm=_pick(L, 1024), tn=tn, per_step=2)

        lams = [v[l][None, :] for v in (lambda_q1, lambda_k1, lambda_q2, lambda_k2)]
        score_bound = (HEAD_DIM * jnp.max(jnp.abs(qw128)) * jnp.max(jnp.abs(kw128))) * BF16_SLACK
        fast = (score_bound <= SCORE_BOUND).astype(jnp.int32).reshape(1)
        per_block = tn // VALUE_DIM
        col0 = {"q": 0, "k": per_block, "v": 2 * per_block, "kc": 0, "vc": per_block}
        heads = _diff_attention(fast, p, pc, col0, lams, subln_w[l][:, None], lam_init,
                                tq=_pick(L, 512), tk=_pick(L, 1024), subs=_pick(L // _pick(L, 512), 8))

        x = _token_mixer(heads, p, 3, 1, x, ga, w_a_up[l].astype(BF16), w_b_up[l].astype(BF16),
                         pool_w[l].astype(BF16), pool_scale[l][None, :], w_o[l].astype(BF16),
                         tm=_pick(L, 256))
        x = _mlp(x, norm_mlp_w[l][None, :], sm, cm, gm, w_ff1[l].astype(BF16), w_ff2[l].astype(BF16),
                 tm=_pick(L, 1024), tf=1024)
    return x
```

```python
import functools
import math

import jax
import jax.numpy as jnp
from jax import lax
from jax.experimental import pallas as pl
from jax.experimental.pallas import tpu as pltpu

F32 = jnp.float32
BF16 = jnp.bfloat16

ATTN_HEADS = 8
HEAD_DIM = 64
VALUE_DIM = 2 * HEAD_DIM
SCORE_BOUND = 64.0
BF16_SLACK = 1.0 + 2.0 ** -6
GRID_W = 64
ROPE_THETA = 10000.0
POOL_WINDOWS = (2, 4, 8, 16)
POOL_HALO = 16
MLP_SUB = 512
UNROLLED_SUBS = 2
EPS = 1e-6

V7X_LANES = 128
V7X_SUBLANES = 8
V7X_VMEM_BYTES = 64 * 1024 * 1024
V7X_VMEM_CAP = V7X_VMEM_BYTES - 6 * 1024 * 1024


def _vmem_limit(block_bytes):
    return int(min(V7X_VMEM_CAP, block_bytes * 1.25 + 8 * 1024 * 1024))


def _nbytes(shape, dtype):
    return math.prod(shape) * jnp.dtype(dtype).itemsize


def _mod_kernel(c_ref, w_ref, b_ref, o_ref):
    cv = c_ref[...]
    act = cv * jax.nn.sigmoid(cv)
    o_ref[...] = jnp.dot(act, w_ref[...], preferred_element_type=F32) + b_ref[...]


def _modulation(cvec, w_mod, b_mod, tn=1024):
    rows, d = cvec.shape
    n = w_mod.shape[1]
    blocks = 2 * (_nbytes((d, tn), F32) + _nbytes((rows, tn), F32) * 2) + _nbytes((rows, d), F32)
    return pl.pallas_call(
        _mod_kernel,
        grid=(n // tn,),
        in_specs=[
            pl.BlockSpec((rows, d), lambda j: (0, 0)),
            pl.BlockSpec((d, tn), lambda j: (0, j)),
            pl.BlockSpec((1, tn), lambda j: (0, j)),
        ],
        out_specs=pl.BlockSpec((rows, tn), lambda j: (0, j)),
        out_shape=jax.ShapeDtypeStruct((rows, n), F32),
        compiler_params=pltpu.CompilerParams(
            dimension_semantics=("parallel",), vmem_limit_bytes=_vmem_limit(blocks)),
        name="modulation",
    )(cvec, w_mod, b_mod)


def _modulated_norm(x, norm_w, shift, scale):
    y = x * lax.rsqrt(jnp.mean(x * x, axis=-1, keepdims=True) + EPS)
    return (y * norm_w) * (1.0 + scale) + shift


def _inproj_kernel(x_ref, nw_ref, shift_ref, scale_ref, w_ref, seg_ref, qkw_ref, tab_ref, o_ref, h_ref,
                   *, kinds, tn):
    j = pl.program_id(2)
    tm = o_ref.shape[1]
    per_step = o_ref.shape[2] // tn
    seg_w = seg_ref.shape[0]

    def qk_epilogue(which, first, c0):
        quarter = HEAD_DIM // 4
        chunk = tm // 4
        w128 = qkw_ref[which:which + 1, :]
        for r0 in range(0, tm, chunk):
            rows = pl.ds(r0, chunk)
            if first:
                h = _modulated_norm(x_ref[0, rows, :], nw_ref[...], shift_ref[0], scale_ref[0])
                h_ref[rows, :] = h.astype(BF16)
            cos, sin_prev, sin_next = (tab_ref[i, rows, :] for i in range(3))
            acc = jnp.dot(h_ref[rows, :], w_ref[:, c0:c0 + tn], preferred_element_type=F32)
            for s0 in range(0, tn, seg_w):
                a = acc[:, s0:s0 + seg_w]
                ms = jnp.dot((a * a).astype(BF16), seg_ref[...], preferred_element_type=F32)
                y = a * lax.rsqrt(ms + EPS)
                for c in range(0, seg_w, V7X_LANES):
                    yc = y[:, c:c + V7X_LANES] * w128
                    out = (yc * cos + pltpu.roll(yc, quarter, 1) * sin_prev
                           + pltpu.roll(yc, V7X_LANES - quarter, 1) * sin_next)
                    o_ref[0, rows, c0 + s0 + c:c0 + s0 + c + V7X_LANES] = out.astype(BF16)

    def elementwise_epilogue(fn, c0):
        acc = jnp.dot(h_ref[...], w_ref[:, c0:c0 + tn], preferred_element_type=F32)
        o_ref[0, :, c0:c0 + tn] = fn(acc).astype(BF16)

    def run_step(combo, first_step):
        for n, kind in enumerate(combo):
            if kind in ("q", "k"):
                qk_epilogue(("q", "k").index(kind), first_step and n == 0, n * tn)
            elif kind == "g":
                elementwise_epilogue(lambda z: 0.5 * jnp.tanh(0.5 * z) + 0.5, n * tn)
            else:
                elementwise_epilogue(lambda z: z, n * tn)

    steps = [tuple(kinds[s * per_step:(s + 1) * per_step]) for s in range(len(kinds) // per_step)]
    assert steps[0][0] in ("q", "k") and steps.count(steps[0]) == 1
    for combo in dict.fromkeys(steps):
        s_lo, n = steps.index(combo), steps.count(combo)
        assert steps[s_lo:s_lo + n] == [combo] * n
        pl.when((j >= s_lo) & (j < s_lo + n))(functools.partial(run_step, combo, s_lo == 0))


def _input_projection(x, norm_w, shift, scale, w_in_bf, col_block0, kinds, seg_mean, qk_w, tables, tm, tn,
                      per_step):
    B, L, D = x.shape
    wide = per_step * tn
    assert len(kinds) % per_step == 0
    vec = lambda: pl.BlockSpec((1, 1, D), lambda i, b, j: (b, 0, 0))
    blocks = (2 * (_nbytes((tm, D), F32) + _nbytes((D, wide), BF16) + _nbytes((3, tm, V7X_LANES), F32)
                   + _nbytes((tm, wide), BF16))
              + _nbytes((tm, D), BF16) + 2 * _nbytes((tm, tn), F32))
    return pl.pallas_call(
        functools.partial(_inproj_kernel, kinds=tuple(kinds), tn=tn),
        grid=(L // tm, B, len(kinds) // per_step),
        in_specs=[
            pl.BlockSpec((1, tm, D), lambda i, b, j: (b, i, 0)),
            pl.BlockSpec((1, D), lambda i, b, j: (0, 0)),
            vec(), vec(),
            pl.BlockSpec((D, wide), lambda i, b, j: (0, col_block0 + j)),
            pl.BlockSpec(seg_mean.shape, lambda i, b, j: (0, 0)),
            pl.BlockSpec(qk_w.shape, lambda i, b, j: (0, 0)),
            pl.BlockSpec((3, tm, V7X_LANES), lambda i, b, j: (0, i, 0)),
        ],
        out_specs=pl.BlockSpec((1, tm, wide), lambda i, b, j: (b, i, j)),
        out_shape=jax.ShapeDtypeStruct((B, L, tn * len(kinds)), BF16),
        scratch_shapes=[pltpu.VMEM((tm, D), BF16)],
        compiler_params=pltpu.CompilerParams(
            dimension_semantics=("parallel", "parallel", "arbitrary"),
            vmem_limit_bytes=_vmem_limit(blocks)),
        name="input_projection",
    )(x, norm_w, shift, scale, w_in_bf, seg_mean, qk_w, tables)


def _attn_kernel(fast_ref, lq1_ref, lk1_ref, lq2_ref, lk2_ref, q_ref, k_ref, v_ref, kc_ref, vc_ref,
                 sw_ref, o_ref, rhs_ref, acc_ref, *, tq, tk, lam_init):
    subs = q_ref.shape[1] // tq
    lane = lax.broadcasted_iota(jnp.int32, (tq, VALUE_DIM), 1)
    for s in range(subs):
        q = q_ref[0, pl.ds(s * tq, tq), :]
        zero = jnp.zeros_like(q)
        rhs_ref[s, :tq, :] = jnp.where(lane < HEAD_DIM, q, zero)
        rhs_ref[s, tq:, :] = jnp.where(lane >= HEAD_DIM, q, zero)

    def scores(s, kt):
        return lax.dot_general(kt, rhs_ref[s], (((1,), (1,)), ((), ())), preferred_element_type=F32)

    def values_t_dot(v, p):
        return lax.dot_general(v, p, (((0,), (0,)), ((), ())), preferred_element_type=F32)

    lam = (jnp.exp(jnp.sum(lq1_ref[...] * lk1_ref[...])) - jnp.exp(jnp.sum(lq2_ref[...] * lk2_ref[...]))
           + lam_init)

    def finish(s, acc, l):
        o = acc * (1.0 / l)
        od = o[:, :tq] - lam * o[:, tq:]
        y = od * lax.rsqrt(jnp.mean(od * od, axis=0, keepdims=True) + EPS) * sw_ref[...]
        o_ref[0, pl.ds(s * tq, tq), :] = (y * (1.0 - lam_init)).T.astype(BF16)

    n_tiles = k_ref.shape[1] // tk

    @pl.when(fast_ref[0] == 1)
    def _():
        tiles = [(k_ref.at[0, pl.ds(t * tk, tk), :], v_ref.at[0, pl.ds(t * tk, tk), :])
                 for t in range(n_tiles)] + [(kc_ref.at[0], vc_ref.at[0])]

        def run(s, some_tiles, state):
            acc, lsum = state
            for kt_ref, vtt_ref in some_tiles:
                p = jnp.exp2(scores(s, kt_ref[...]))
                part = jnp.sum(p.reshape(p.shape[0] // V7X_SUBLANES, V7X_SUBLANES, p.shape[1]), axis=0)
                pv = values_t_dot(vtt_ref[...], p.astype(BF16))
                acc = pv if acc is None else acc + pv
                lsum = part if lsum is None else lsum + part
            return acc, lsum

        unrolled = min(subs, UNROLLED_SUBS)
        assert subs % unrolled == 0

        def group(first):
            state = run(first, tiles, (None, None))
            for n in range(1, unrolled):
                started = run(first + n, tiles[:1], (None, None))
                finish(first + n - 1, state[0], jnp.sum(state[1], axis=0, keepdims=True))
                state = run(first + n, tiles[1:], started)
            finish(first + unrolled - 1, state[0], jnp.sum(state[1], axis=0, keepdims=True))

        def body(g, carry):
            group(g * unrolled)
            return carry

        if subs > unrolled:
            lax.fori_loop(0, subs // unrolled, body, 0)
        else:
            group(0)

    @pl.when(fast_ref[0] != 1)
    def _():
        for s in range(subs):
            def update(carry, kt, vtt):
                m_prev, l_prev = carry
                sc = scores(s, kt)
                m_new = jnp.maximum(m_prev, jnp.max(sc, axis=0, keepdims=True))
                alpha = jnp.exp2(m_prev - m_new)
                p = jnp.exp2(sc - m_new)
                acc_ref[...] = alpha * acc_ref[...] + values_t_dot(vtt, p.astype(BF16))
                return m_new, alpha * l_prev + jnp.sum(p, axis=0, keepdims=True)

            def body(t, carry):
                off = pl.multiple_of(t * tk, tk)
                return update(carry, k_ref[0, pl.ds(off, tk), :], v_ref[0, pl.ds(off, tk), :])

            acc_ref[...] = jnp.zeros_like(acc_ref)
            init = (jnp.full((1, 2 * tq), -jnp.inf, F32), jnp.zeros((1, 2 * tq), F32))
            carry = lax.fori_loop(0, n_tiles, body, init)
            _, l = update(carry, kc_ref[0], vc_ref[0])
            finish(s, acc_ref[...], l)


def _diff_attention(fast, p, p_c, col0, lams, subln_w, lam_init, tq, tk, subs):
    B, L, _ = p.shape
    Lc = p_c.shape[1]
    H = ATTN_HEADS
    assert L % tk == 0
    lam_spec = pl.BlockSpec((1, HEAD_DIM), lambda b, h, i: (0, 0))
    keys = lambda rows, c0: pl.BlockSpec((1, rows, VALUE_DIM), lambda b, h, i: (b, 0, c0 + h))
    blocks = (2 * (2 * _nbytes((tq, VALUE_DIM), BF16) + 2 * _nbytes((L, VALUE_DIM), BF16)
                   + 2 * _nbytes((Lc, VALUE_DIM), BF16))
              + 3 * _nbytes((2 * tq, VALUE_DIM), F32) + 4 * _nbytes((2 * tq, tk), F32))
    return pl.pallas_call(
        functools.partial(_attn_kernel, tq=tq, tk=tk, lam_init=lam_init),
        grid=(B, H, L // (subs * tq)),
        in_specs=[
            pl.BlockSpec(memory_space=pltpu.SMEM),
            lam_spec, lam_spec, lam_spec, lam_spec,
            pl.BlockSpec((1, subs * tq, VALUE_DIM), lambda b, h, i: (b, i, col0["q"] + h)),
            keys(L, col0["k"]), keys(L, col0["v"]), keys(Lc, col0["kc"]), keys(Lc, col0["vc"]),
            pl.BlockSpec((VALUE_DIM, 1), lambda b, h, i: (0, 0)),
        ],
        out_specs=pl.BlockSpec((1, subs * tq, VALUE_DIM), lambda b, h, i: (b, i, h)),
        out_shape=jax.ShapeDtypeStruct((B, L, H * VALUE_DIM), BF16),
        scratch_shapes=[pltpu.VMEM((subs, 2 * tq, VALUE_DIM), BF16),
                        pltpu.VMEM((VALUE_DIM, 2 * tq), F32)],
        compiler_params=pltpu.CompilerParams(
            dimension_semantics=("parallel", "parallel", "parallel"),
            vmem_limit_bytes=_vmem_limit(blocks)),
        name="diff_attention",
    )(fast, *lams, p, p, p, p_c, p_c, subln_w)


def _mixer_kernel(heads_ref, u_ref, uprev_ref, unext_ref, g_ref, x_ref, gate_ref,
                  wa_ref, wb_ref, pw_ref, ps_ref, wo_ref, band_ref, o_ref, ubuf_ref, y_ref, *, tm, seq_len):
    i = pl.program_id(1)
    nt = pl.num_programs(1)
    D = x_ref.shape[2]
    group = u_ref.shape[2] // len(POOL_WINDOWS)

    y_a = jnp.dot(heads_ref[0], wa_ref[...], preferred_element_type=F32)

    ubuf_ref[pl.ds(POOL_HALO, tm), :] = u_ref[0]
    ubuf_ref[pl.ds(0, POOL_HALO), :] = jnp.where(i > 0, uprev_ref[0], jnp.zeros_like(uprev_ref[0]))
    ubuf_ref[pl.ds(POOL_HALO + tm, POOL_HALO), :] = jnp.where(i < nt - 1, unext_ref[0],
                                                             jnp.zeros_like(unext_ref[0]))

    t = i * tm + lax.broadcasted_iota(jnp.int32, (tm, 1), 0)
    totals = [jnp.dot(band_ref[gi], ubuf_ref[:, gi * group:(gi + 1) * group], preferred_element_type=F32)
              for gi in range(len(POOL_WINDOWS))]
    ds = []
    for gi, w in enumerate(POOL_WINDOWS):
        cols = slice(gi * group, (gi + 1) * group)
        lo = jnp.maximum(t - w // 2, 0)
        hi = jnp.minimum(t + (w - w // 2), seq_len)
        d = totals[gi] * (1.0 / (hi - lo).astype(F32)) - u_ref[0, :, cols].astype(F32)
        ds.append(d.astype(BF16))
    for gi in range(len(POOL_WINDOWS)):
        cols = slice(gi * group, (gi + 1) * group)
        yg = jnp.dot(ds[gi], pw_ref[gi], preferred_element_type=F32)
        y_ref[:, cols] = (yg * ps_ref[:, cols]).astype(BF16)

    y_b = jnp.dot(y_ref[...], wb_ref[...], preferred_element_type=F32)
    mixed = g_ref[0, :, :D].astype(F32) * y_a + g_ref[0, :, D:].astype(F32) * y_b
    mix = jnp.dot(mixed.astype(BF16), wo_ref[...], preferred_element_type=F32)
    o_ref[0] = x_ref[0] + gate_ref[0] * mix


def _token_mixer(heads, p_all, u_block, g_block, x, gate, w_a_up, w_b_up, pool_w, pool_scale, w_o, tm):
    B, L, D = x.shape
    AW, PW = heads.shape[2], w_b_up.shape[0]
    hb = tm // POOL_HALO
    offset = jnp.arange(tm + 2 * POOL_HALO)[None, :] - POOL_HALO - jnp.arange(tm)[:, None]
    band = jnp.stack([(offset >= -(w // 2)) & (offset < w - w // 2) for w in POOL_WINDOWS]).astype(BF16)
    n_halo = L // POOL_HALO
    const = lambda shape: pl.BlockSpec(shape, lambda b, i: (0,) * len(shape),
                                       pipeline_mode=pl.Buffered(1))
    weights = (_nbytes((AW, D), BF16) + _nbytes((PW, D), BF16) + _nbytes(pool_w.shape, BF16)
               + _nbytes((D, D), BF16))
    blocks = (weights + 2 * (_nbytes((tm, AW), BF16) + _nbytes((tm, PW), BF16)
                             + _nbytes((tm, 2 * D), BF16) + 2 * _nbytes((tm, D), F32))
              + _nbytes((tm + 2 * POOL_HALO, PW), F32) + _nbytes((tm, PW), BF16)
              + 4 * _nbytes((tm, D), F32))
    return pl.pallas_call(
        functools.partial(_mixer_kernel, tm=tm, seq_len=L),
        grid=(B, L // tm),
        in_specs=[
            pl.BlockSpec((1, tm, AW), lambda b, i: (b, i, 0)),
            pl.BlockSpec((1, tm, PW), lambda b, i: (b, i, u_block)),
            pl.BlockSpec((1, POOL_HALO, PW), lambda b, i: (b, jnp.maximum(i * hb - 1, 0), u_block)),
            pl.BlockSpec((1, POOL_HALO, PW), lambda b, i: (b, jnp.minimum((i + 1) * hb, n_halo - 1), u_block)),
            pl.BlockSpec((1, tm, 2 * D), lambda b, i: (b, i, g_block)),
            pl.BlockSpec((1, tm, D), lambda b, i: (b, i, 0)),
            pl.BlockSpec((1, 1, D), lambda b, i: (b, 0, 0)),
            const((AW, D)), const((PW, D)), const(pool_w.shape), const((1, PW)), const((D, D)),
            const(band.shape),
        ],
        out_specs=pl.BlockSpec((1, tm, D), lambda b, i: (b, i, 0)),
        out_shape=jax.ShapeDtypeStruct((B, L, D), F32),
        scratch_shapes=[pltpu.VMEM((tm + 2 * POOL_HALO, PW), BF16), pltpu.VMEM((tm, PW), BF16)],
        compiler_params=pltpu.CompilerParams(
            dimension_semantics=("parallel", "parallel"),
            vmem_limit_bytes=_vmem_limit(blocks)),
        name="token_mixer",
    )(heads, p_all, p_all, p_all, p_all, x, gate, w_a_up, w_b_up, pool_w, pool_scale, w_o, band)


def _mlp_kernel(x_ref, nw_ref, shift_ref, scale_ref, gate_ref, w1_ref, w2_ref, o_ref, h_ref):
    j = pl.program_id(2)
    tm, tf = h_ref.shape[0], w1_ref.shape[1]

    sub = min(tf, MLP_SUB)

    def ffn(h, rows, assign_first=False, finish=False):
        for s0 in range(0, tf, sub):
            a = jnp.maximum(jnp.dot(h, w1_ref[:, s0:s0 + sub], preferred_element_type=F32), 0.0)
            a2 = (a * a).astype(BF16)
            for n in range(0, o_ref.shape[2], sub):
                cols = slice(n, n + sub)
                part = jnp.dot(a2, w2_ref[s0:s0 + sub, cols], preferred_element_type=F32)
                if assign_first and s0 == 0:
                    o_ref[0, rows, cols] = part
                elif finish and s0 + sub == tf:
                    o_ref[0, rows, cols] = (x_ref[0, rows, cols]
                                            + gate_ref[0, :, cols] * (o_ref[0, rows, cols] + part))
                else:
                    o_ref[0, rows, cols] += part

    @pl.when(j == 0)
    def _():
        chunk = tm // 4
        for r0 in range(0, tm, chunk):
            rows = pl.ds(r0, chunk)
            h = _modulated_norm(x_ref[0, rows, :], nw_ref[...], shift_ref[0], scale_ref[0]).astype(BF16)
            h_ref[rows, :] = h
            ffn(h, rows, assign_first=True)

    last = pl.num_programs(2) - 1

    @pl.when((j > 0) & (j < last))
    def _():
        ffn(h_ref[...], pl.ds(0, tm))

    @pl.when(j == last)
    def _():
        ffn(h_ref[...], pl.ds(0, tm), finish=True)


def _mlp(x, norm_w, shift, scale, gate, w1, w2, tm, tf):
    B, L, D = x.shape
    F = w1.shape[1]
    assert F // tf >= 2
    vec = lambda: pl.BlockSpec((1, 1, D), lambda b, i, j: (b, 0, 0))
    blocks = (2 * (2 * _nbytes((tm, D), F32) + _nbytes((D, tf), BF16) + _nbytes((tf, D), BF16))
              + _nbytes((tm, D), BF16) + 3 * _nbytes((tm, min(tf, MLP_SUB)), F32))
    return pl.pallas_call(
        _mlp_kernel,
        grid=(B, L // tm, F // tf),
        in_specs=[
            pl.BlockSpec((1, tm, D), lambda b, i, j: (b, i, 0)),
            pl.BlockSpec((1, D), lambda b, i, j: (0, 0)),
            vec(), vec(), vec(),
            pl.BlockSpec((D, tf), lambda b, i, j: (0, j)),
            pl.BlockSpec((tf, D), lambda b, i, j: (j, 0)),
        ],
        out_specs=pl.BlockSpec((1, tm, D), lambda b, i, j: (b, i, 0)),
        out_shape=jax.ShapeDtypeStruct((B, L, D), F32),
        scratch_shapes=[pltpu.VMEM((tm, D), BF16)],
        compiler_params=pltpu.CompilerParams(
            dimension_semantics=("parallel", "parallel", "arbitrary"),
            vmem_limit_bytes=_vmem_limit(blocks)),
        name="sq_relu_mlp",
    )(x, norm_w, shift, scale, gate, w1, w2)


def _rope_tables(seq_len, rope):
    if not rope:
        ones = jnp.ones((seq_len, V7X_LANES), F32)
        return jnp.stack([ones, jnp.zeros_like(ones), jnp.zeros_like(ones)])
    t = jnp.arange(seq_len)
    row, col = t // GRID_W, t % GRID_W
    half = HEAD_DIM // 2
    quarter = half // 2
    inv_freq = ROPE_THETA ** (-jnp.arange(0, half, 2, dtype=F32) / half)
    lane = jnp.arange(V7X_LANES)
    e = lane % HEAD_DIM
    pos = jnp.where((e // half)[None, :] == 0, row[:, None], col[:, None]).astype(F32)
    ang = pos * inv_freq[e % quarter][None, :]
    cos, sin = jnp.cos(ang), jnp.sin(ang)
    second = ((e % half) >= quarter)[None, :]
    return jnp.stack([cos, jnp.where(second, sin, 0.0), jnp.where(second, 0.0, -sin)])


def _pick(n, pref):
    t = min(pref, n)
    while n % t:
        t //= 2
    return t


def kernel(x, c, ctx, c_ctx, w_mod, b_mod, norm_attn_w, w_in, q_norm_w, k_norm_w, lambda_q1, lambda_k1,
           lambda_q2, lambda_k2, subln_w, pool_w, pool_scale, w_a_up, w_b_up, w_o, norm_mlp_w, w_ff1, w_ff2):
    B, L, D = x.shape
    Lc = ctx.shape[1]
    depth = w_mod.shape[0]
    qk_w = 2 * ATTN_HEADS * HEAD_DIM
    tn = qk_w
    assert depth == 1 and w_in.shape[2] == 4 * tn + 2 * D and L % GRID_W == 0
    assert 2 * D == 2 * tn * 2 and pool_w.shape[1] * pool_w.shape[2] == tn

    for l in range(depth):
        lam_init = 0.8 - 0.6 * math.exp(-0.3 * l)

        rows = -(-(B + 1) // V7X_SUBLANES) * V7X_SUBLANES
        cvec = jnp.zeros((rows, D), F32).at[:B].set(c).at[B].set(c_ctx)
        mod = _modulation(cvec, w_mod[l], b_mod[l][None, :])
        sa, ca, ga, sm, cm, gm = [mod[:B, None, k * D:(k + 1) * D] for k in range(6)]
        sa_c, ca_c = [mod[B, k * D:(k + 1) * D].reshape(1, 1, D) for k in range(2)]

        w_in_bf = w_in[l].astype(BF16)
        qw128 = jnp.tile(q_norm_w[l], 2) * (HEAD_DIM ** -0.5 * math.log2(math.e))
        kw128 = jnp.tile(k_norm_w[l], 2)
        qk_w128 = jnp.stack([qw128, kw128])
        nw = norm_attn_w[l][None, :]

        seg_w = tn // 4
        seg = jnp.kron(jnp.eye(seg_w // HEAD_DIM, dtype=F32),
                       jnp.full((HEAD_DIM, HEAD_DIM), 1.0 / HEAD_DIM)).astype(BF16)
        pc = _input_projection(ctx.reshape(1, B * Lc, D), nw, sa_c, ca_c, w_in_bf, 1, ("k", "v"), seg,
                               qk_w128, _rope_tables(B * Lc, rope=False), tm=_pick(B * Lc, 1024), tn=tn, per_step=1)
        pc = pc.reshape(B, Lc, pc.shape[2])
        p = _input_projection(x, nw, sa, ca, w_in_bf, 0, ("q", "k", "v", "u", "g", "g", "g", "g"), seg,
                              qk_w128, _rope_tables(L, rope=True), tm=_pick(L, 1024), tn=tn, per_step=2)

        lams = [v[l][None, :] for v in (lambda_q1, lambda_k1, lambda_q2, lambda_k2)]
        score_bound = (HEAD_DIM * jnp.max(jnp.abs(qw128)) * jnp.max(jnp.abs(kw128))) * BF16_SLACK
        fast = (score_bound <= SCORE_BOUND).astype(jnp.int32).reshape(1)
        per_block = tn // VALUE_DIM
        col0 = {"q": 0, "k": per_block, "v": 2 * per_block, "kc": 0, "vc": per_block}
        heads = _diff_attention(fast, p, pc, col0, lams, subln_w[l][:, None], lam_init,
                                tq=_pick(L, 512), tk=_pick(L, 1024), subs=_pick(L // _pick(L, 512), 4))

        x = _token_mixer(heads, p, 3, 1, x, ga, w_a_up[l].astype(BF16), w_b_up[l].astype(BF16),
                         pool_w[l].astype(BF16), pool_scale[l][None, :], w_o[l].astype(BF16),
                         tm=_pick(L, 256))
        x = _mlp(x, norm_mlp_w[l][None, :], sm, cm, gm, w_ff1[l].astype(BF16), w_ff2[l].astype(BF16),
                 tm=_pick(L, 1024), tf=1024)
    return x
```
